```python
import math
import jax
import jax.numpy as jnp
from jax import lax
import numpy as np

D_MODEL = 2048
BATCH = 16
SEQ = 256
DEPTH = 2
DEC_BATCH = 8
DEC_SEQ = 2048
PAST_LEN = 256

GRID_W = 64
N_MIXERS = 2
N_HY_LAYERS = (DEPTH + N_MIXERS - 1) // N_MIXERS
N_RET_LAYERS = DEPTH // N_MIXERS
D_FF = ((8 * D_MODEL + 3 * 256 - 1) // (3 * 256)) * 256
HY_ORDER = 2
HY_PROJ = HY_ORDER + 1
HY_SHORT = 3
HY_BANDS = 16
HY_EMB = 1 + 2 * HY_BANDS
HY_HIDDEN = 64
HY_DECAY_PCT_SHORT = 0.3
HY_DECAY_PCT_LONG = 1.5
HY_TARGET = 1e-2
RET_HEADS = 8
RET_DK = D_MODEL // RET_HEADS
RET_DV = 2 * RET_DK
RET_CHUNK = 128
RET_IN = 2 * RET_HEADS * RET_DK + 2 * RET_HEADS * RET_DV
ROPE_BASE = 10000.0
EPS = 1e-6

kernel_name = 'hyena_retention_diffusion_step'


def rmsnorm(x, g):
    xf = x.astype(jnp.float32)
    y = xf * lax.rsqrt(jnp.mean(xf * xf, axis=-1, keepdims=True) + EPS)
    return (y * g.astype(jnp.float32)).astype(x.dtype)


def swiglu(h, w1, w3, w2):
    return (jax.nn.silu(h @ w1) * (h @ w3)) @ w2


def short_conv(u, w, b):
    L = u.shape[1]
    pad = HY_SHORT // 2
    up = jnp.pad(u, ((0, 0), (pad, pad), (0, 0)))
    out = up[:, 0:L] * w[0]
    for t in range(1, HY_SHORT):
        out = out + up[:, t:t + L] * w[t]
    return out + b


def hyena_filter_spectrum(L, w1, b1, freq, w2, b2, w3):
    f32 = jnp.float32
    t = jnp.arange(L, dtype=f32) / L
    ang = 2.0 * math.pi * t[:, None] * jnp.arange(1, HY_BANDS + 1, dtype=f32)[None, :]
    feat = jnp.concatenate([t[:, None], jnp.cos(ang), jnp.sin(ang)], axis=-1)
    z = jnp.sin(freq[0].astype(f32) * (feat @ w1.astype(f32) + b1.astype(f32)))
    z = jnp.sin(freq[1].astype(f32) * (z @ w2.astype(f32) + b2.astype(f32)))
    filt = (z @ w3.astype(f32)).reshape(L, HY_ORDER, 2, D_MODEL)
    min_decay = abs(math.log(HY_TARGET) / HY_DECAY_PCT_LONG)
    max_decay = abs(math.log(HY_TARGET) / HY_DECAY_PCT_SHORT)
    deltas = jnp.linspace(min_decay, max_decay, D_MODEL, dtype=f32)
    window = jnp.exp(-t[:, None] * deltas[None, :])
    filt = filt * window[:, None, None, :]
    h_fwd, h_bwd = filt[:, :, 0], filt[:, :, 1]
    k_circ = jnp.concatenate([h_fwd, jnp.zeros((1, HY_ORDER, D_MODEL), f32), h_bwd[:0:-1]], axis=0)
    return jnp.fft.rfft(k_circ, axis=0)


def long_conv(z, kf):
    L = z.shape[1]
    zf = jnp.fft.rfft(z, n=2 * L, axis=1)
    return jnp.fft.irfft(zf * kf[None], n=2 * L, axis=1)[:, :L]


def hyena_mixer(h, p, j):
    L = h.shape[1]
    u = short_conv(h @ p['hy_w_in'][j] + p['hy_b_in'][j], p['hy_conv_w'][j], p['hy_conv_b'][j])
    v, *gates = jnp.split(u.astype(jnp.float32), HY_PROJ, axis=-1)
    kf = hyena_filter_spectrum(L, p['hy_filt_w1'][j], p['hy_filt_b1'][j], p['hy_filt_freq'][j],
                               p['hy_filt_w2'][j], p['hy_filt_b2'][j], p['hy_filt_w3'][j])
    z = v
    for n in range(HY_ORDER):
        z = gates[n] * (long_conv(z, kf[:, n]) + p['hy_bias_d'][j, n].astype(jnp.float32) * z)
    return z.astype(h.dtype) @ p['hy_w_out'][j] + p['hy_b_out'][j]


def rope1d(x, pos):
    n = x.shape[-1] // 2
    inv = jnp.exp(-math.log(ROPE_BASE) * jnp.arange(n, dtype=jnp.float32) / n)
    ang = pos[:, None] * inv[None, :]
    cos, sin = jnp.cos(ang), jnp.sin(ang)
    x1, x2 = x[..., :n], x[..., n:]
    return jnp.concatenate([x1 * cos - x2 * sin, x1 * sin + x2 * cos], axis=-1)


def rope2d(x, pos_row, pos_col):
    half = x.shape[-1] // 2
    return jnp.concatenate([rope1d(x[..., :half], pos_row), rope1d(x[..., half:], pos_col)], axis=-1)


def retention_scan(q, k, v, log_gamma, s0):
    bsz, nh, L, _ = q.shape
    dv = v.shape[-1]
    nc = L // RET_CHUNK
    idx = jnp.arange(RET_CHUNK, dtype=jnp.float32)
    lg = log_gamma[:, None]
    diff = idx[:, None] - idx[None, :]
    dmat = jnp.where(diff[None] >= 0, jnp.exp(lg[:, :, None] * jnp.maximum(diff, 0.0)[None]), 0.0)
    xi = jnp.exp(lg * (idx[None] + 1.0))[None, :, :, None]
    zeta = jnp.exp(lg * (RET_CHUNK - 1.0 - idx[None]))[None, :, :, None]
    g_chunk = jnp.exp(log_gamma * RET_CHUNK)[None, :, None, None]

    def to_chunks(a):
        return jnp.moveaxis(a.reshape(bsz, nh, nc, RET_CHUNK, a.shape[-1]), 2, 0)

    def step(s, qkv):
        qc, kc, vc = qkv
        scores = jnp.einsum('bhid,bhjd->bhij', qc, kc) * dmat[None]
        inner = jnp.einsum('bhij,bhjv->bhiv', scores, vc)
        cross = jnp.einsum('bhid,bhdv->bhiv', qc * xi, s)
        s_new = g_chunk * s + jnp.einsum('bhjd,bhjv->bhdv', kc * zeta, vc)
        return s_new, inner + cross

    s_fin, out = lax.scan(step, s0, (to_chunks(q), to_chunks(k), to_chunks(v)))
    return jnp.moveaxis(out, 0, 2).reshape(bsz, nh, L, dv), s_fin


def retention_mixer(h, p, r, s0, pos):
    bsz, L, _ = h.shape
    f32 = jnp.float32
    hk = RET_HEADS * RET_DK
    hv = RET_HEADS * RET_DV
    proj = h @ p['ret_w_in'][r]
    q, k, v, g = jnp.split(proj, [hk, 2 * hk, 2 * hk + hv], axis=-1)

    def heads(a, d):
        return a.reshape(bsz, L, RET_HEADS, d).transpose(0, 2, 1, 3).astype(f32)

    q = heads(q, RET_DK)
    k = heads(k, RET_DK) * (RET_DK ** -0.5)
    v = heads(v, RET_DV)
    if pos is not None:
        q = rope2d(q, pos[0], pos[1])
        k = rope2d(k, pos[0], pos[1])
    log_g = -jnp.exp(p['ret_decay'][r].astype(f32))
    o_f, s_f = retention_scan(q, k, v, log_g[0], s0[:, 0])
    o_b, s_b = retention_scan(jnp.flip(q, axis=2), jnp.flip(k, axis=2), jnp.flip(v, axis=2), log_g[1], s0[:, 1])
    o = o_f + jnp.flip(o_b, axis=2)
    mu = jnp.mean(o, axis=-1, keepdims=True)
    var = jnp.mean(jnp.square(o - mu), axis=-1, keepdims=True)
    o = (o - mu) * lax.rsqrt(var + EPS)
    o = o.transpose(0, 2, 1, 3).reshape(bsz, L, hv) * p['ret_gn'][r].astype(f32)
    out = (o.astype(h.dtype) * jax.nn.silu(g)) @ p['ret_w_out'][r]
    return out, jnp.stack([s_f, s_b], axis=1)


def trunk(x, cond, pos, s_cache, p):
    bsz = x.shape[0]
    states = []
    for i in range(DEPTH):
        mod = jax.nn.silu(cond) @ p['w_ada'][i] + p['b_ada'][i]
        sh1, sc1, g1, sh2, sc2, g2 = jnp.split(mod[:, None, :], 6, axis=-1)
        h = rmsnorm(x, p['norm_mix'][i]) * (1.0 + sc1) + sh1
        j = i // N_MIXERS
        if i % N_MIXERS == 0:
            o = hyena_mixer(h, p, j)
        else:
            if s_cache is None:
                s0 = jnp.zeros((bsz, 2, RET_HEADS, RET_DK, RET_DV), jnp.float32)
            else:
                s0 = s_cache[:, j].astype(jnp.float32)
            o, s_fin = retention_mixer(h, p, j, s0, pos)
            states.append(s_fin)
        x = x + g1 * o
        h = rmsnorm(x, p['norm_ffn'][i]) * (1.0 + sc2) + sh2
        x = x + g2 * swiglu(h, p['ffn_w1'][i], p['ffn_w3'][i], p['ffn_w2'][i])
    return rmsnorm(x, p['norm_final']), states


def setup_inputs(seed: int = 0) -> dict:
    key = jax.random.key(seed)
    keys = iter(jax.random.split(key, 40))
    f32 = jnp.float32

    def nrm(shape, scale):
        return jax.random.normal(next(keys), shape, f32) * scale

    D = D_MODEL
    base_decay = jnp.log(-jnp.log1p(-jnp.exp2(-5.0 - jnp.arange(RET_HEADS, dtype=f32))))
    return {
        'x_prompt': nrm((BATCH, SEQ, D), 1.0),
        'x_sample': nrm((DEC_BATCH, DEC_SEQ, D), 1.0),
        'state_ret': nrm((DEC_BATCH, N_RET_LAYERS, 2, RET_HEADS, RET_DK, RET_DV), 0.05),
        'c': nrm((DEC_BATCH, D), 1.0),
        'c_ctx': nrm((D,), 1.0),
        'w_ada': nrm((DEPTH, D, 6 * D), 0.5 * D ** -0.5),
        'b_ada': nrm((DEPTH, 6 * D), 0.01),
        'norm_mix': 1.0 + nrm((DEPTH, D), 0.01),
        'norm_ffn': 1.0 + nrm((DEPTH, D), 0.01),
        'norm_final': 1.0 + nrm((D,), 0.01),
        'ffn_w1': nrm((DEPTH, D, D_FF), D ** -0.5),
        'ffn_w3': nrm((DEPTH, D, D_FF), D ** -0.5),
        'ffn_w2': nrm((DEPTH, D_FF, D), D_FF ** -0.5),
        'hy_w_in': nrm((N_HY_LAYERS, D, HY_PROJ * D), D ** -0.5),
        'hy_b_in': nrm((N_HY_LAYERS, HY_PROJ * D), 0.01),
        'hy_conv_w': nrm((N_HY_LAYERS, HY_SHORT, HY_PROJ * D), HY_SHORT ** -0.5),
        'hy_conv_b': nrm((N_HY_LAYERS, HY_PROJ * D), 0.01),
        'hy_filt_w1': nrm((N_HY_LAYERS, HY_EMB, HY_HIDDEN), HY_EMB ** -0.5),
        'hy_filt_b1': nrm((N_HY_LAYERS, HY_HIDDEN), 0.01),
        'hy_filt_freq': 1.0 + nrm((N_HY_LAYERS, 2, HY_HIDDEN), 0.01),
        'hy_filt_w2': nrm((N_HY_LAYERS, HY_HIDDEN, HY_HIDDEN), HY_HIDDEN ** -0.5),
        'hy_filt_b2': nrm((N_HY_LAYERS, HY_HIDDEN), 0.01),
        'hy_filt_w3': nrm((N_HY_LAYERS, HY_HIDDEN, HY_ORDER * 2 * D), 0.01),
        'hy_bias_d': nrm((N_HY_LAYERS, HY_ORDER, D), 0.1),
        'hy_w_out': nrm((N_HY_LAYERS, D, D), D ** -0.5),
        'hy_b_out': nrm((N_HY_LAYERS, D), 0.01),
        'ret_w_in': nrm((N_RET_LAYERS, D, RET_IN), D ** -0.5),
        'ret_decay': base_decay[None, None, :] + nrm((N_RET_LAYERS, 2, RET_HEADS), 0.01),
        'ret_gn': 1.0 + nrm((N_RET_LAYERS, RET_HEADS * RET_DV), 0.01),
        'ret_w_out': nrm((N_RET_LAYERS, RET_HEADS * RET_DV, D), (RET_HEADS * RET_DV) ** -0.5),
    }


def reference(x_prompt, x_sample, state_ret, c, c_ctx, w_ada, b_ada, norm_mix, norm_ffn, norm_final,
              ffn_w1, ffn_w3, ffn_w2, hy_w_in, hy_b_in, hy_conv_w, hy_conv_b, hy_filt_w1, hy_filt_b1,
              hy_filt_freq, hy_filt_w2, hy_filt_b2, hy_filt_w3, hy_bias_d, hy_w_out, hy_b_out,
              ret_w_in, ret_decay, ret_gn, ret_w_out):
    p = dict(w_ada=w_ada, b_ada=b_ada, norm_mix=norm_mix, norm_ffn=norm_ffn, norm_final=norm_final,
             ffn_w1=ffn_w1, ffn_w3=ffn_w3, ffn_w2=ffn_w2, hy_w_in=hy_w_in, hy_b_in=hy_b_in,
             hy_conv_w=hy_conv_w, hy_conv_b=hy_conv_b, hy_filt_w1=hy_filt_w1, hy_filt_b1=hy_filt_b1,
             hy_filt_freq=hy_filt_freq, hy_filt_w2=hy_filt_w2, hy_filt_b2=hy_filt_b2,
             hy_filt_w3=hy_filt_w3, hy_bias_d=hy_bias_d, hy_w_out=hy_w_out, hy_b_out=hy_b_out,
             ret_w_in=ret_w_in, ret_decay=ret_decay, ret_gn=ret_gn, ret_w_out=ret_w_out)
    y_prompt, ctx_states = trunk(x_prompt, c_ctx[None, :], None, None, p)
    new_state_ret = jnp.stack(ctx_states, axis=1).astype(x_prompt.dtype)
    n_lat = x_sample.shape[1]
    rows = n_lat // GRID_W
    pos_row = jnp.repeat(jnp.arange(rows, dtype=jnp.float32), GRID_W)
    pos_col = jnp.tile(jnp.arange(GRID_W, dtype=jnp.float32), rows)
    y_sample, _ = trunk(x_sample, c, (pos_row, pos_col), state_ret, p)
    return (y_prompt, y_sample, new_state_ret)
```

```python
import functools
import math

import numpy as np
import jax
import jax.numpy as jnp
from jax import lax
from jax.experimental import pallas as pl
from jax.experimental.pallas import tpu as pltpu

F32 = jnp.float32
BF16 = jnp.bfloat16

EPS = 1e-6
GRID_W = 64
HY_BANDS = 16
HY_DECAY_PCT_SHORT = 0.3
HY_DECAY_PCT_LONG = 1.5
HY_TARGET = 1e-2
RET_HEADS = 8
RET_CHUNK = 128
ROPE_BASE = 10000.0

V7X_VMEM_BYTES = 64 * 1024 * 1024
VMEM_LIMIT = V7X_VMEM_BYTES - 8 * 1024 * 1024
LANES = 128
HIGHEST = lax.Precision.HIGHEST


def _params(n_axes):
    return pltpu.CompilerParams(
        dimension_semantics=("arbitrary",) * n_axes, vmem_limit_bytes=VMEM_LIMIT)


def _dot(a, b):
    return jnp.dot(a, b, preferred_element_type=F32)


def _ada_kernel(c_ref, w_ref, b_ref, o_ref):
    s = jax.nn.silu(c_ref[...]).astype(BF16)
    o_ref[...] = _dot(s, w_ref[...].astype(BF16)) + b_ref[...]


def _ada_mod(cond, w_ada, b_ada, tn=1536):
    depth, d, n = w_ada.shape
    rows = cond.shape[0]
    return pl.pallas_call(
        _ada_kernel,
        grid=(depth, n // tn),
        in_specs=[
            pl.BlockSpec((rows, d), lambda l, j: (0, 0)),
            pl.BlockSpec((None, d, tn), lambda l, j: (l, 0, j)),
            pl.BlockSpec((None, 1, tn), lambda l, j: (l, 0, j)),
        ],
        out_specs=pl.BlockSpec((None, rows, tn), lambda l, j: (l, 0, j)),
        out_shape=jax.ShapeDtypeStruct((depth, rows, n), F32),
        compiler_params=_params(2),
        name="ada_mod",
    )(cond, w_ada, b_ada.reshape(depth, 1, n))


def _seq_of_tile(tm, seq_len, n_mod):
    if n_mod == 1:
        return lambda i: 0
    assert seq_len % tm == 0
    return lambda i: (i * tm) // seq_len


def _norm_mod_kernel(x_ref, g_ref, mod_ref, o_ref, *, shift_idx, scale_idx):
    x = x_ref[...]
    y = x * lax.rsqrt(jnp.mean(x * x, axis=-1, keepdims=True) + EPS)
    y = y * g_ref[...]
    scale = mod_ref[scale_idx:scale_idx + 1, :]
    shift = mod_ref[shift_idx:shift_idx + 1, :]
    o_ref[...] = (y * (1.0 + scale) + shift).astype(o_ref.dtype)


def _norm_mod(x, g, mod, seq_len, shift_idx, scale_idx, tm=256):
    m, d = x.shape
    seq = _seq_of_tile(tm, seq_len, mod.shape[0])
    return pl.pallas_call(
        functools.partial(_norm_mod_kernel, shift_idx=shift_idx, scale_idx=scale_idx),
        grid=(m // tm,),
        in_specs=[
            pl.BlockSpec((tm, d), lambda i: (i, 0)),
            pl.BlockSpec((1, d), lambda i: (0, 0)),
            pl.BlockSpec((None, 6, d), lambda i: (seq(i), 0, 0)),
        ],
        out_specs=pl.BlockSpec((tm, d), lambda i: (i, 0)),
        out_shape=jax.ShapeDtypeStruct((m, d), BF16),
        compiler_params=_params(1),
        name="norm_mod",
    )(x, g.reshape(1, d), mod)


def _final_norm_kernel(x_ref, g_ref, o_ref):
    x = x_ref[...]
    y = x * lax.rsqrt(jnp.mean(x * x, axis=-1, keepdims=True) + EPS)
    o_ref[...] = y * g_ref[...]


def _final_norm(x, g, tm=256):
    m, d = x.shape
    return pl.pallas_call(
        _final_norm_kernel,
        grid=(m // tm,),
        in_specs=[pl.BlockSpec((tm, d), lambda i: (i, 0)),
                  pl.BlockSpec((1, d), lambda i: (0, 0))],
        out_specs=pl.BlockSpec((tm, d), lambda i: (i, 0)),
        out_shape=jax.ShapeDtypeStruct((m, d), F32),
        compiler_params=_params(1),
        name="final_norm",
    )(x, g.reshape(1, d))


def _cast_weights_once(pairs):
    @pl.when(pl.program_id(1) == 0)
    def _():
        for w_ref, wb_ref in pairs:
            wb_ref[...] = w_ref[...].astype(BF16)


def _proj_kernel(a_ref, w_ref, o_ref, wb_ref):
    _cast_weights_once([(w_ref, wb_ref)])
    o_ref[...] = _dot(a_ref[...], wb_ref[...]).astype(o_ref.dtype)


def _proj(a, w, tm=1024, tn=512, out_dtype=F32):
    m, k = a.shape
    n = w.shape[1]
    return pl.pallas_call(
        _proj_kernel,
        grid=(n // tn, m // tm),
        in_specs=[pl.BlockSpec((tm, k), lambda j, i: (i, 0)),
                  pl.BlockSpec((k, tn), lambda j, i: (0, j))],
        out_specs=pl.BlockSpec((tm, tn), lambda j, i: (i, j)),
        out_shape=jax.ShapeDtypeStruct((m, n), out_dtype),
        scratch_shapes=[pltpu.VMEM((k, tn), BF16)],
        compiler_params=_params(2),
        name="proj",
    )(a, w)


def _hy_in_kernel(a_ref, w_ref, b_ref, cw_ref, cb_ref, o_ref, wb_ref, *, seq_len):
    _cast_weights_once([(w_ref, wb_ref)])
    u = _dot(a_ref[...], wb_ref[...]) + b_ref[...]
    tm, tn = u.shape
    pos = lax.broadcasted_iota(jnp.int32, (tm, tn), 0) % seq_len
    prev = jnp.where(pos == 0, 0.0, pltpu.roll(u, 1, 0))
    nxt = jnp.where(pos == seq_len - 1, 0.0, pltpu.roll(u, tm - 1, 0))
    o_ref[...] = (prev * cw_ref[0:1, :] + u * cw_ref[1:2, :] + nxt * cw_ref[2:3, :]
                  + cb_ref[...])


def _hy_in(a, w, b, cw, cb, seq_len, tm, tn=256):
    m, k = a.shape
    n = w.shape[1]
    assert tm % seq_len == 0
    return pl.pallas_call(
        functools.partial(_hy_in_kernel, seq_len=seq_len),
        grid=(n // tn, m // tm),
        in_specs=[
            pl.BlockSpec((tm, k), lambda j, i: (i, 0)),
            pl.BlockSpec((k, tn), lambda j, i: (0, j)),
            pl.BlockSpec((1, tn), lambda j, i: (0, j)),
            pl.BlockSpec((cw.shape[0], tn), lambda j, i: (0, j)),
            pl.BlockSpec((1, tn), lambda j, i: (0, j)),
        ],
        out_specs=pl.BlockSpec((tm, tn), lambda j, i: (i, j)),
        out_shape=jax.ShapeDtypeStruct((m, n), F32),
        scratch_shapes=[pltpu.VMEM((k, tn), BF16)],
        compiler_params=_params(2),
        name="hy_in",
    )(a, w, b.reshape(1, n), cw, cb.reshape(1, n))


def _swiglu_kernel(a_ref, w1_ref, w3_ref, o_ref, w1b_ref, w3b_ref):
    _cast_weights_once([(w1_ref, w1b_ref), (w3_ref, w3b_ref)])
    a = a_ref[...]
    o_ref[...] = (jax.nn.silu(_dot(a, w1b_ref[...])) * _dot(a, w3b_ref[...])).astype(BF16)


def _swiglu_up(a, w1, w3, tm=1024, tn=512):
    m, k = a.shape
    n = w1.shape[1]
    wspec = pl.BlockSpec((k, tn), lambda j, i: (0, j))
    return pl.pallas_call(
        _swiglu_kernel,
        grid=(n // tn, m // tm),
        in_specs=[pl.BlockSpec((tm, k), lambda j, i: (i, 0)), wspec, wspec],
        out_specs=pl.BlockSpec((tm, tn), lambda j, i: (i, j)),
        out_shape=jax.ShapeDtypeStruct((m, n), BF16),
        scratch_shapes=[pltpu.VMEM((k, tn), BF16), pltpu.VMEM((k, tn), BF16)],
        compiler_params=_params(2),
        name="swiglu_up",
    )(a, w1, w3)


def _resid_kernel(*refs, gate_idx, has_bias):
    if has_bias:
        a_ref, w_ref, b_ref, x_ref, mod_ref, o_ref, wb_ref = refs
    else:
        a_ref, w_ref, x_ref, mod_ref, o_ref, wb_ref = refs
    _cast_weights_once([(w_ref, wb_ref)])
    y = _dot(a_ref[...], wb_ref[...])
    if has_bias:
        y = y + b_ref[...]
    o_ref[...] = x_ref[...] + mod_ref[gate_idx:gate_idx + 1, :] * y


def _resid_proj(a, w, bias, x, mod, seq_len, gate_idx, tm, tn=512):
    m, k = a.shape
    n = w.shape[1]
    seq = _seq_of_tile(tm, seq_len, mod.shape[0])
    in_specs = [pl.BlockSpec((tm, k), lambda j, i: (i, 0)),
                pl.BlockSpec((k, tn), lambda j, i: (0, j))]
    args = [a, w]
    if bias is not None:
        in_specs.append(pl.BlockSpec((1, tn), lambda j, i: (0, j)))
        args.append(bias.reshape(1, n))
    in_specs += [pl.BlockSpec((tm, tn), lambda j, i: (i, j)),
                 pl.BlockSpec((None, 6, tn), lambda j, i: (seq(i), 0, j))]
    args += [x, mod]
    return pl.pallas_call(
        functools.partial(_resid_kernel, gate_idx=gate_idx, has_bias=bias is not None),
        grid=(n // tn, m // tm),
        in_specs=in_specs,
        out_specs=pl.BlockSpec((tm, tn), lambda j, i: (i, j)),
        out_shape=jax.ShapeDtypeStruct((m, n), F32),
        scratch_shapes=[pltpu.VMEM((k, tn), BF16)],
        compiler_params=_params(2),
        name="resid_proj",
    )(*args)


def _odd_dft_mats(p):
    f = np.arange(p, dtype=np.int64)
    m = np.arange(2 * p, dtype=np.int64)
    phase = ((2 * f[:, None] + 1) * m[None, :]) % (4 * p)
    ang = np.pi * phase.astype(np.float64) / (2 * p)
    fwd = np.concatenate([np.cos(ang), -np.sin(ang)], axis=0)
    fwd_lo = fwd[:, :p]
    fwd_hi = fwd[:, p:].copy()
    fwd_hi[:, 0] = 0.0
    t = np.arange(p, dtype=np.int64)
    phase_i = (t[:, None] * (2 * f[None, :] + 1)) % (4 * p)
    ang_i = np.pi * phase_i.astype(np.float64) / (2 * p)
    inv = np.concatenate([np.cos(ang_i), -np.sin(ang_i)], axis=1) / p
    as_bf16 = lambda a: jnp.asarray(a, F32).astype(BF16)
    return as_bf16(fwd_lo), as_bf16(fwd_hi), as_bf16(inv)


def _filter_mlp_kernel(w1_ref, b1_ref, fr_ref, w2_ref, b2_ref, o_ref, *, seq_len):
    rows = 2 * seq_len
    r = lax.broadcasted_iota(jnp.int32, (rows, LANES), 0)
    lane = lax.broadcasted_iota(jnp.int32, (rows, LANES), 1)
    t = jnp.abs(r - seq_len).astype(F32) / seq_len
    band = jnp.where(lane <= HY_BANDS, lane, lane - HY_BANDS).astype(F32)
    ang = 2.0 * math.pi * t * band
    feat = jnp.where(lane == 0, t,
                     jnp.where(lane <= HY_BANDS, jnp.cos(ang),
                               jnp.where(lane <= 2 * HY_BANDS, jnp.sin(ang), 0.0)))
    z = jnp.dot(feat, w1_ref[...], precision=HIGHEST, preferred_element_type=F32)
    z = jnp.sin(fr_ref[0:1, :] * (z + b1_ref[...]))
    z = jnp.dot(z, w2_ref[...], precision=HIGHEST, preferred_element_type=F32)
    o_ref[...] = jnp.sin(fr_ref[1:2, :] * (z + b2_ref[...]))


def _filter_time_kernel(z_ref, wf_ref, wb_ref, dl_ref, o_ref, *, seq_len):
    z = z_ref[...]
    ff = jnp.dot(z, wf_ref[...], precision=HIGHEST, preferred_element_type=F32)
    fb = jnp.dot(z, wb_ref[...], precision=HIGHEST, preferred_element_type=F32)
    n = lax.broadcasted_iota(jnp.int32, ff.shape, 0) - seq_len
    t = jnp.abs(n).astype(F32) / seq_len
    window = jnp.exp(-t * dl_ref[...])
    kk = jnp.where(n >= 0, ff, fb) * window
    o_ref[...] = jnp.where(n == -seq_len, 0.0, kk)


def _filter_spec_kernel(ka_ref, kb_ref, flo_ref, fhi_ref, o_ref):
    o_ref[...] = (_dot(flo_ref[...], ka_ref[...].astype(BF16))
                  - _dot(fhi_ref[...], kb_ref[...].astype(BF16)))


def _hyena_filter_spectra(seq_len, p, w1, b1, freq, w2, b2, w3, fwd_lo, fwd_hi, ct=512):
    hidden = w1.shape[1]
    n_order = 2
    d = w3.shape[1] // (2 * n_order)
    nb = seq_len // p
    nd = 2 * nb - 1
    rows = 2 * seq_len
    w1p = jnp.pad(w1, ((0, LANES - w1.shape[0]), (0, 0)))
    full = lambda shape: pl.BlockSpec(shape, lambda *_: (0,) * len(shape))
    z = pl.pallas_call(
        functools.partial(_filter_mlp_kernel, seq_len=seq_len),
        grid=(1,),
        in_specs=[full((LANES, hidden)), full((1, hidden)), full((2, hidden)),
                  full((hidden, hidden)), full((1, hidden))],
        out_specs=full((rows, hidden)),
        out_shape=jax.ShapeDtypeStruct((rows, hidden), F32),
        compiler_params=_params(1),
        name="filter_mlp",
    )(w1p, b1.reshape(1, hidden), freq, w2, b2.reshape(1, hidden))

    min_decay = abs(math.log(HY_TARGET) / HY_DECAY_PCT_LONG)
    max_decay = abs(math.log(HY_TARGET) / HY_DECAY_PCT_SHORT)
    deltas = jnp.linspace(min_decay, max_decay, d, dtype=F32).reshape(1, d)
    nct = d // ct
    kk = pl.pallas_call(
        functools.partial(_filter_time_kernel, seq_len=seq_len),
        grid=(n_order, nct),
        in_specs=[
            pl.BlockSpec((rows, hidden), lambda o, j: (0, 0)),
            pl.BlockSpec((hidden, ct), lambda o, j: (0, (2 * o) * nct + j)),
            pl.BlockSpec((hidden, ct), lambda o, j: (0, (2 * o + 1) * nct + j)),
            pl.BlockSpec((1, ct), lambda o, j: (0, j)),
        ],
        out_specs=pl.BlockSpec((None, rows, ct), lambda o, j: (o, 0, j)),
        out_shape=jax.ShapeDtypeStruct((n_order, rows, d), F32),
        compiler_params=_params(2),
        name="filter_time",
    )(z, w3, w3, deltas)

    return pl.pallas_call(
        _filter_spec_kernel,
        grid=(n_order, d // ct, nd),
        in_specs=[
            pl.BlockSpec((None, p, ct), lambda o, j, e: (o, e + 1, j)),
            pl.BlockSpec((None, p, ct), lambda o, j, e: (o, e, j)),
            pl.BlockSpec((2 * p, p), lambda o, j, e: (0, 0)),
            pl.BlockSpec((2 * p, p), lambda o, j, e: (0, 0)),
        ],
        out_specs=pl.BlockSpec((None, None, 2 * p, ct), lambda o, j, e: (o, e, 0, j)),
        out_shape=jax.ShapeDtypeStruct((n_order, nd, 2 * p, d), F32),
        compiler_params=_params(3),
        name="filter_spec",
    )(kk, kk, fwd_lo, fwd_hi)


def _long_conv_kernel(z_ref, gate_ref, k_ref, bias_ref, fwd_ref, inv_ref, o_ref,
                      zs_ref, ys_ref, *, p, nb):
    fwd = fwd_ref[...]
    inv = inv_ref[...]
    for b in range(nb):
        zs_ref[b] = _dot(fwd, z_ref[b * p:(b + 1) * p, :].astype(BF16))
    for a in range(nb):
        yr = None
        yi = None
        for b in range(nb):
            e = a - b + nb - 1
            kr = k_ref[e, 0:p, :]
            ki = k_ref[e, p:2 * p, :]
            zr = zs_ref[b, 0:p, :]
            zi = zs_ref[b, p:2 * p, :]
            tr = kr * zr - ki * zi
            ti = kr * zi + ki * zr
            yr = tr if yr is None else yr + tr
            yi = ti if yi is None else yi + ti
        ys_ref[0:p, :] = yr.astype(BF16)
        ys_ref[p:2 * p, :] = yi.astype(BF16)
        y = _dot(inv, ys_ref[...])
        rows = slice(a * p, (a + 1) * p)
        za = z_ref[rows, :].astype(F32)
        o_ref[rows, :] = (gate_ref[rows, :] * (y + bias_ref[...] * za)).astype(o_ref.dtype)


def _long_conv(z, z_col, gate, gate_col, spectra, order, bias, fwd_lo, inv, seq_len, p,
               out_dtype, ct=256):
    m = z.shape[0]
    d = spectra.shape[-1]
    nb = seq_len // p
    nd = 2 * nb - 1
    zc, gc = z_col // ct, gate_col // ct
    return pl.pallas_call(
        functools.partial(_long_conv_kernel, p=p, nb=nb),
        grid=(d // ct, m // seq_len),
        in_specs=[
            pl.BlockSpec((seq_len, ct), lambda j, b: (b, zc + j)),
            pl.BlockSpec((seq_len, ct), lambda j, b: (b, gc + j)),
            pl.BlockSpec((None, nd, 2 * p, ct), lambda j, b: (order, 0, 0, j)),
            pl.BlockSpec((1, ct), lambda j, b: (0, j)),
            pl.BlockSpec((2 * p, p), lambda j, b: (0, 0)),
            pl.BlockSpec((p, 2 * p), lambda j, b: (0, 0)),
        ],
        out_specs=pl.BlockSpec((seq_len, ct), lambda j, b: (b, j)),
        out_shape=jax.ShapeDtypeStruct((m, d), out_dtype),
        scratch_shapes=[pltpu.VMEM((nb, 2 * p, ct), F32), pltpu.VMEM((2 * p, ct), BF16)],
        compiler_params=_params(2),
        name="long_conv",
    )(z, gate, spectra, bias.reshape(1, d), fwd_lo, inv)


def _retention_kernel(*refs, n_chunks, use_rope, has_s0, want_state, dk_scale):
    it = iter(refs)
    dec_ref, q_ref, k_ref, v_ref, g_ref, gn_ref = (next(it) for _ in range(6))
    cos_ref = next(it) if use_rope else None
    sin_ref = next(it) if use_rope else None
    s0_ref = next(it) if has_s0 else None
    o_ref = next(it)
    sfin_ref = next(it) if want_state else None
    qr_ref, kr_ref, acc_ref, s_ref = (next(it) for _ in range(4))

    c_len = RET_CHUNK
    half = q_ref.shape[1] // 2

    def chunk(c):
        return pl.ds(pl.multiple_of(c * c_len, c_len), c_len)

    def rope(x, rows):
        if not use_rope:
            return x
        swapped = jnp.concatenate(
            [pltpu.roll(x[:, :half], half // 2, 1), pltpu.roll(x[:, half:], half // 2, 1)],
            axis=1)
        return x * cos_ref[rows, :] + swapped * sin_ref[rows, :]

    def prep(c, carry):
        rows = chunk(c)
        qr_ref[rows, :] = rope(q_ref[rows, :], rows)
        kr_ref[rows, :] = rope(k_ref[rows, :] * dk_scale, rows)
        return carry

    lax.fori_loop(0, n_chunks, prep, 0)

    log_g = -jnp.exp(dec_ref[...])
    ii = lax.broadcasted_iota(jnp.int32, (c_len, c_len), 0)
    jj = lax.broadcasted_iota(jnp.int32, (c_len, c_len), 1)
    idx = lax.broadcasted_iota(jnp.int32, (c_len, 1), 0).astype(F32)

    for direction in range(2):
        lg = log_g[direction]
        diff = (ii - jj) if direction == 0 else (jj - ii)
        dmat = jnp.where(diff >= 0, jnp.exp(lg * jnp.maximum(diff, 0).astype(F32)), 0.0)
        if direction == 0:
            xi = jnp.exp(lg * (idx + 1.0))
            zeta = jnp.exp(lg * (c_len - 1.0 - idx))
        else:
            xi = jnp.exp(lg * (c_len - idx))
            zeta = jnp.exp(lg * idx)
        g_chunk = jnp.exp(lg * c_len)

        if has_s0:
            s_ref[...] = s0_ref[direction].astype(F32)
        else:
            s_ref[...] = jnp.zeros(s_ref.shape, F32)

        def body(t, carry, direction=direction, dmat=dmat, xi=xi, zeta=zeta, g_chunk=g_chunk):
            c = t if direction == 0 else n_chunks - 1 - t
            rows = chunk(c)
            qc = qr_ref[rows, :]
            kc = kr_ref[rows, :]
            vc = v_ref[rows, :].astype(BF16)
            s = s_ref[...]
            scores = lax.dot_general(qc.astype(BF16), kc.astype(BF16),
                                     (((1,), (1,)), ((), ())),
                                     preferred_element_type=F32) * dmat
            inner = _dot(scores.astype(BF16), vc)
            cross = _dot((qc * xi).astype(BF16), s.astype(BF16))
            upd = lax.dot_general((kc * zeta).astype(BF16), vc,
                                  (((0,), (0,)), ((), ())), preferred_element_type=F32)
            s_ref[...] = g_chunk * s + upd
            if direction == 0:
                acc_ref[rows, :] = inner + cross
            else:
                acc_ref[rows, :] = acc_ref[rows, :] + (inner + cross)
            return carry

        lax.fori_loop(0, n_chunks, body, 0)
        if want_state:
            sfin_ref[direction] = s_ref[...].astype(sfin_ref.dtype)

    def finish(c, carry):
        rows = chunk(c)
        o = acc_ref[rows, :]
        mu = jnp.mean(o, axis=-1, keepdims=True)
        var = jnp.mean(jnp.square(o - mu), axis=-1, keepdims=True)
        o = (o - mu) * lax.rsqrt(var + EPS)
        o = o * gn_ref[...]
        o_ref[rows, :] = (o * jax.nn.silu(g_ref[rows, :])).astype(o_ref.dtype)
        return carry

    lax.fori_loop(0, n_chunks, finish, 0)


def _retention(proj, decay, gn, seq_len, rope_tabs, s0, want_state, state_dtype):
    m = proj.shape[0]
    n_seq = m // seq_len
    nh = RET_HEADS
    dv = gn.shape[0] // nh
    dk = (proj.shape[1] - 2 * nh * dv) // (2 * nh)
    q_blk0, k_blk0 = 0, nh
    v_blk0 = (2 * nh * dk) // dv
    g_blk0 = v_blk0 + nh
    use_rope = rope_tabs is not None
    has_s0 = s0 is not None

    in_specs = [
        pl.BlockSpec((2, None, 1, 1), lambda b, h: (0, h, 0, 0)),
        pl.BlockSpec((seq_len, dk), lambda b, h: (b, q_blk0 + h)),
        pl.BlockSpec((seq_len, dk), lambda b, h: (b, k_blk0 + h)),
        pl.BlockSpec((seq_len, dv), lambda b, h: (b, v_blk0 + h)),
        pl.BlockSpec((seq_len, dv), lambda b, h: (b, g_blk0 + h)),
        pl.BlockSpec((1, dv), lambda b, h: (0, h)),
    ]
    args = [decay.reshape(2, nh, 1, 1), proj, proj, proj, proj, gn.reshape(1, nh * dv)]
    if use_rope:
        tab_spec = pl.BlockSpec((seq_len, dk), lambda b, h: (0, 0))
        in_specs += [tab_spec, tab_spec]
        args += list(rope_tabs)
    state_spec = pl.BlockSpec((None, None, 2, None, dk, dv), lambda b, h: (b, 0, 0, h, 0, 0))
    if has_s0:
        in_specs.append(state_spec)
        args.append(s0)
    out_specs = [pl.BlockSpec((seq_len, dv), lambda b, h: (b, h))]
    out_shape = [jax.ShapeDtypeStruct((m, nh * dv), BF16)]
    if want_state:
        out_specs.append(state_spec)
        out_shape.append(jax.ShapeDtypeStruct((n_seq, 1, 2, nh, dk, dv), state_dtype))
    outs = pl.pallas_call(
        functools.partial(_retention_kernel, n_chunks=seq_len // RET_CHUNK, use_rope=use_rope,
                          has_s0=has_s0, want_state=want_state, dk_scale=dk ** -0.5),
        grid=(n_seq, nh),
        in_specs=in_specs,
        out_specs=out_specs,
        out_shape=out_shape,
        scratch_shapes=[pltpu.VMEM((seq_len, dk), F32), pltpu.VMEM((seq_len, dk), F32),
                        pltpu.VMEM((seq_len, dv), F32), pltpu.VMEM((dk, dv), F32)],
        compiler_params=_params(2),
        name="retention",
    )(*args)
    return (outs[0], outs[1]) if want_state else (outs[0], None)


def _rope_tables(seq_len, dk):
    rows = seq_len // GRID_W
    pos_row = jnp.repeat(jnp.arange(rows, dtype=F32), GRID_W)
    pos_col = jnp.tile(jnp.arange(GRID_W, dtype=F32), rows)
    n = dk // 4
    inv = jnp.exp(-math.log(ROPE_BASE) * jnp.arange(n, dtype=F32) / n)
    ang_r = pos_row[:, None] * inv[None, :]
    ang_c = pos_col[:, None] * inv[None, :]
    cos = jnp.concatenate([jnp.cos(ang_r)] * 2 + [jnp.cos(ang_c)] * 2, axis=-1)
    sin = jnp.concatenate([-jnp.sin(ang_r), jnp.sin(ang_r), -jnp.sin(ang_c), jnp.sin(ang_c)],
                          axis=-1)
    return cos, sin


def _trunk(x, mod, seq_len, conv_block, tm, rope_tabs, s0, want_state, p):
    m, d = x.shape
    mod0, mod1 = mod[0], mod[1]

    h = _norm_mod(x, p['norm_mix'][0], mod0, seq_len, 0, 1)
    u = _hy_in(h, p['hy_w_in'][0], p['hy_b_in'][0], p['hy_conv_w'][0], p['hy_conv_b'][0],
               seq_len, tm=max(tm, seq_len))
    fwd_lo, fwd_hi, inv = _odd_dft_mats(conv_block)
    spectra = _hyena_filter_spectra(
        seq_len, conv_block, p['hy_filt_w1'][0], p['hy_filt_b1'][0], p['hy_filt_freq'][0],
        p['hy_filt_w2'][0], p['hy_filt_b2'][0], p['hy_filt_w3'][0], fwd_lo, fwd_hi)
    z = _long_conv(u, 0, u, d, spectra, 0, p['hy_bias_d'][0, 0], fwd_lo, inv, seq_len,
                   conv_block, F32)
    z = _long_conv(z, 0, u, 2 * d, spectra, 1, p['hy_bias_d'][0, 1], fwd_lo, inv, seq_len,
                   conv_block, BF16)
    x = _resid_proj(z, p['hy_w_out'][0], p['hy_b_out'][0], x, mod0, seq_len, 2, tm)
    h = _norm_mod(x, p['norm_ffn'][0], mod0, seq_len, 3, 4)
    a = _swiglu_up(h, p['ffn_w1'][0], p['ffn_w3'][0], tm=tm)
    x = _resid_proj(a, p['ffn_w2'][0], None, x, mod0, seq_len, 5, tm // 2)

    h = _norm_mod(x, p['norm_mix'][1], mod1, seq_len, 0, 1)
    proj = _proj(h, p['ret_w_in'][0], tm=tm)
    og, s_fin = _retention(proj, p['ret_decay'][0], p['ret_gn'][0], seq_len, rope_tabs, s0,
                           want_state, x.dtype)
    x = _resid_proj(og, p['ret_w_out'][0], None, x, mod1, seq_len, 2, tm)
    h = _norm_mod(x, p['norm_ffn'][1], mod1, seq_len, 3, 4)
    a = _swiglu_up(h, p['ffn_w1'][1], p['ffn_w3'][1], tm=tm)
    x = _resid_proj(a, p['ffn_w2'][1], None, x, mod1, seq_len, 5, tm // 2)

    return _final_norm(x, p['norm_final']), s_fin


def kernel(x_prompt, x_sample, state_ret, c, c_ctx, w_ada, b_ada, norm_mix, norm_ffn, norm_final,
           ffn_w1, ffn_w3, ffn_w2, hy_w_in, hy_b_in, hy_conv_w, hy_conv_b, hy_filt_w1, hy_filt_b1,
           hy_filt_freq, hy_filt_w2, hy_filt_b2, hy_filt_w3, hy_bias_d, hy_w_out, hy_b_out,
           ret_w_in, ret_decay, ret_gn, ret_w_out):
    p = dict(norm_mix=norm_mix, norm_ffn=norm_ffn, norm_final=norm_final,
             ffn_w1=ffn_w1, ffn_w3=ffn_w3, ffn_w2=ffn_w2, hy_w_in=hy_w_in, hy_b_in=hy_b_in,
             hy_conv_w=hy_conv_w, hy_conv_b=hy_conv_b, hy_filt_w1=hy_filt_w1,
             hy_filt_b1=hy_filt_b1, hy_filt_freq=hy_filt_freq, hy_filt_w2=hy_filt_w2,
             hy_filt_b2=hy_filt_b2, hy_filt_w3=hy_filt_w3, hy_bias_d=hy_bias_d,
             hy_w_out=hy_w_out, hy_b_out=hy_b_out, ret_w_in=ret_w_in, ret_decay=ret_decay,
             ret_gn=ret_gn, ret_w_out=ret_w_out)
    n_ctx, ctx_len, d = x_prompt.shape
    n_dec, dec_len, _ = x_sample.shape
    depth = w_ada.shape[0]

    cond_rows = 16
    cond = jnp.concatenate(
        [c_ctx[None, :], c, jnp.zeros((cond_rows - 1 - n_dec, d), c.dtype)], axis=0)
    mod = _ada_mod(cond, w_ada, b_ada).reshape(depth, cond_rows, 6, d)
    mod_ctx = mod[:, 0:1]
    mod_dec = mod[:, 1:1 + n_dec]

    y_prompt, ctx_state = _trunk(
        x_prompt.reshape(n_ctx * ctx_len, d), mod_ctx, ctx_len, conv_block=ctx_len, tm=1024,
        rope_tabs=None, s0=None, want_state=True, p=p)
    dk = state_ret.shape[-2]
    y_sample, _ = _trunk(
        x_sample.reshape(n_dec * dec_len, d), mod_dec, dec_len, conv_block=512, tm=1024,
        rope_tabs=_rope_tables(dec_len, dk), s0=state_ret, want_state=False, p=p)

    return (y_prompt.reshape(x_prompt.shape), y_sample.reshape(x_sample.shape), ctx_state)
```

```python
import functools
import math

import numpy as np
import jax
import jax.numpy as jnp
from jax import lax
from jax.experimental import pallas as pl
from jax.experimental.pallas import tpu as pltpu

F32 = jnp.float32
BF16 = jnp.bfloat16

EPS = 1e-6
GRID_W = 64
HY_BANDS = 16
HY_DECAY_PCT_SHORT = 0.3
HY_DECAY_PCT_LONG = 1.5
HY_TARGET = 1e-2
RET_HEADS = 8
ROPE_BASE = 10000.0
RET_CHUNK = 256
NORM_ROWS = 512

V7X_VMEM_BYTES = 64 * 1024 * 1024
VMEM_LIMIT = V7X_VMEM_BYTES - 8 * 1024 * 1024
LANES = 128
HIGHEST = lax.Precision.HIGHEST


def _params(n_axes):
    return pltpu.CompilerParams(
        dimension_semantics=("arbitrary",) * n_axes, vmem_limit_bytes=VMEM_LIMIT)


def _dot(a, b):
    return jnp.dot(a, b, preferred_element_type=F32)


def _ada_kernel(c_ref, w_ref, b_ref, o_ref):
    s = jax.nn.silu(c_ref[...]).astype(BF16)
    o_ref[...] = _dot(s, w_ref[...].astype(BF16)) + b_ref[...]


def _ada_mod(cond, w_ada, b_ada, tn=1536):
    depth, d, n = w_ada.shape
    rows = cond.shape[0]
    return pl.pallas_call(
        _ada_kernel,
        grid=(depth, n // tn),
        in_specs=[
            pl.BlockSpec((rows, d), lambda l, j: (0, 0)),
            pl.BlockSpec((None, d, tn), lambda l, j: (l, 0, j)),
            pl.BlockSpec((None, 1, tn), lambda l, j: (l, 0, j)),
        ],
        out_specs=pl.BlockSpec((None, rows, tn), lambda l, j: (l, 0, j)),
        out_shape=jax.ShapeDtypeStruct((depth, rows, n), F32),
        compiler_params=_params(2),
        name="ada_mod",
    )(cond, w_ada, b_ada.reshape(depth, 1, n))


def _seq_of_tile(tm, seq_len, n_mod):
    if n_mod == 1:
        return lambda i: 0
    assert seq_len % tm == 0
    return lambda i: (i * tm) // seq_len


def _norm_mod_kernel(x_ref, g_ref, mod_ref, o_ref, *, shift_idx, scale_idx):
    x = x_ref[...]
    y = x * lax.rsqrt(jnp.mean(x * x, axis=-1, keepdims=True) + EPS)
    y = y * g_ref[...]
    scale = mod_ref[scale_idx:scale_idx + 1, :]
    shift = mod_ref[shift_idx:shift_idx + 1, :]
    o_ref[...] = (y * (1.0 + scale) + shift).astype(o_ref.dtype)


def _norm_mod(x, g, layer, mod, seq_len, shift_idx, scale_idx, tm=512):
    m, d = x.shape
    seq = _seq_of_tile(tm, seq_len, mod.shape[0])
    return pl.pallas_call(
        functools.partial(_norm_mod_kernel, shift_idx=shift_idx, scale_idx=scale_idx),
        grid=(m // tm,),
        in_specs=[
            pl.BlockSpec((tm, d), lambda i: (i, 0)),
            pl.BlockSpec((None, 1, d), lambda i: (layer, 0, 0)),
            pl.BlockSpec((None, 6, d), lambda i: (seq(i), 0, 0)),
        ],
        out_specs=pl.BlockSpec((tm, d), lambda i: (i, 0)),
        out_shape=jax.ShapeDtypeStruct((m, d), BF16),
        compiler_params=_params(1),
        name="norm_mod",
    )(x, g.reshape(g.shape[0], 1, d), mod)


def _final_norm_kernel(x_ref, g_ref, o_ref):
    x = x_ref[...]
    y = x * lax.rsqrt(jnp.mean(x * x, axis=-1, keepdims=True) + EPS)
    o_ref[...] = y * g_ref[...]


def _final_norm(x, g, tm=512):
    m, d = x.shape
    return pl.pallas_call(
        _final_norm_kernel,
        grid=(m // tm,),
        in_specs=[pl.BlockSpec((tm, d), lambda i: (i, 0)),
                  pl.BlockSpec((1, d), lambda i: (0, 0))],
        out_specs=pl.BlockSpec((tm, d), lambda i: (i, 0)),
        out_shape=jax.ShapeDtypeStruct((m, d), F32),
        compiler_params=_params(1),
        name="final_norm",
    )(x, g.reshape(1, d))


def _cast_weights_once(pairs):
    @pl.when(pl.program_id(1) == 0)
    def _():
        for w_ref, wb_ref in pairs:
            wb_ref[...] = w_ref[...].astype(BF16)


def _w_spec(k, tn, layer, col0=0):
    assert col0 % tn == 0
    return pl.BlockSpec((None, k, tn), lambda j, i: (layer, 0, col0 // tn + j))


def _vec_spec(tn, layer, col0=0):
    return pl.BlockSpec((None, 1, tn), lambda j, i: (layer, 0, col0 // tn + j))


def _rotate_pairs(x, half):
    parts = [pltpu.roll(x[:, g * half:(g + 1) * half], half // 2, 1)
             for g in range(x.shape[1] // half)]
    return jnp.concatenate(parts, axis=1)


def _ret_qk_kernel(*refs, dk, k_tile0, k_scale, use_rope):
    if use_rope:
        a_ref, w_ref, cos_ref, sin_ref, o_ref, wb_ref = refs
    else:
        a_ref, w_ref, o_ref, wb_ref = refs
    _cast_weights_once([(w_ref, wb_ref)])
    y = _dot(a_ref[...], wb_ref[...])
    y = y * jnp.where(pl.program_id(0) >= k_tile0, k_scale, 1.0).astype(F32)
    if use_rope:
        cos = cos_ref[...]
        sin = sin_ref[...]
        for hb in range(y.shape[1] // dk):
            x = y[:, hb * dk:(hb + 1) * dk]
            o_ref[:, hb * dk:(hb + 1) * dk] = x * cos + _rotate_pairs(x, dk // 2) * sin
    else:
        o_ref[...] = y


def _ret_qk_proj(a, w, layer, nh, dk, seq_len, rope_tabs, tm=1024, tn=1024):
    m, k = a.shape
    n = 2 * nh * dk
    use_rope = rope_tabs is not None
    in_specs = [pl.BlockSpec((tm, k), lambda j, i: (i, 0)), _w_spec(k, tn, layer)]
    args = [a, w]
    if use_rope:
        assert seq_len % tm == 0
        tiles_per_seq = seq_len // tm
        tab_spec = pl.BlockSpec((tm, dk), lambda j, i: (i % tiles_per_seq, 0))
        in_specs += [tab_spec, tab_spec]
        args += list(rope_tabs)
    return pl.pallas_call(
        functools.partial(_ret_qk_kernel, dk=dk, k_tile0=(nh * dk) // tn, k_scale=dk ** -0.5,
                          use_rope=use_rope),
        grid=(n // tn, m // tm),
        in_specs=in_specs,
        out_specs=pl.BlockSpec((tm, tn), lambda j, i: (i, j)),
        out_shape=jax.ShapeDtypeStruct((m, n), F32),
        scratch_shapes=[pltpu.VMEM((k, tn), BF16)],
        compiler_params=_params(2),
        name="ret_qk_proj",
    )(*args)


def _proj_kernel(a_ref, w_ref, o_ref, wb_ref, *, silu):
    _cast_weights_once([(w_ref, wb_ref)])
    y = _dot(a_ref[...], wb_ref[...])
    if silu:
        y = jax.nn.silu(y)
    o_ref[...] = y.astype(o_ref.dtype)


def _proj(a, w, layer, col0, n, out_dtype, silu=False, tm=1024, tn=1024):
    m, k = a.shape
    return pl.pallas_call(
        functools.partial(_proj_kernel, silu=silu),
        grid=(n // tn, m // tm),
        in_specs=[pl.BlockSpec((tm, k), lambda j, i: (i, 0)), _w_spec(k, tn, layer, col0)],
        out_specs=pl.BlockSpec((tm, tn), lambda j, i: (i, j)),
        out_shape=jax.ShapeDtypeStruct((m, n), out_dtype),
        scratch_shapes=[pltpu.VMEM((k, tn), BF16)],
        compiler_params=_params(2),
        name="proj",
    )(a, w)


def _hy_in_kernel(a_ref, w_ref, b_ref, cw_ref, cb_ref, o_ref, wb_ref, *, seq_len, n_split):
    _cast_weights_once([(w_ref, wb_ref)])
    tm, tn = o_ref.shape
    ts = tn // n_split
    a = a_ref[...]
    for s in range(n_split):
        cols = slice(s * ts, (s + 1) * ts)
        u = _dot(a, wb_ref[:, cols]) + b_ref[:, cols]
        w0, w1, w2 = cw_ref[0:1, cols], cw_ref[1:2, cols], cw_ref[2:3, cols]
        cb = cb_ref[:, cols]
        o_ref[:, cols] = (pltpu.roll(u, 1, 0) * w0 + u * w1 + pltpu.roll(u, tm - 1, 0) * w2
                          + cb)
        for r0 in range(0, tm, seq_len):
            r1 = r0 + seq_len - 1
            o_ref[r0:r0 + 1, cols] = u[r0:r0 + 1] * w1 + u[r0 + 1:r0 + 2] * w2 + cb
            o_ref[r1:r1 + 1, cols] = u[r1 - 1:r1] * w0 + u[r1:r1 + 1] * w1 + cb


def _hy_in(a, w, b, cw, cb, layer, seq_len, tm, tn=512, n_split=2):
    m, k = a.shape
    n = w.shape[2]
    assert tm % seq_len == 0
    taps = cw.shape[1]
    return pl.pallas_call(
        functools.partial(_hy_in_kernel, seq_len=seq_len, n_split=n_split),
        grid=(n // tn, m // tm),
        in_specs=[
            pl.BlockSpec((tm, k), lambda j, i: (i, 0)),
            _w_spec(k, tn, layer),
            _vec_spec(tn, layer),
            pl.BlockSpec((None, taps, tn), lambda j, i: (layer, 0, j)),
            _vec_spec(tn, layer),
        ],
        out_specs=pl.BlockSpec((tm, tn), lambda j, i: (i, j)),
        out_shape=jax.ShapeDtypeStruct((m, n), F32),
        scratch_shapes=[pltpu.VMEM((k, tn), BF16)],
        compiler_params=_params(2),
        name="hy_in",
    )(a, w, b.reshape(b.shape[0], 1, n), cw, cb.reshape(cb.shape[0], 1, n))


def _swiglu_kernel(a_ref, w1_ref, w3_ref, o_ref, w1b_ref, w3b_ref):
    _cast_weights_once([(w1_ref, w1b_ref), (w3_ref, w3b_ref)])
    a = a_ref[...]
    o_ref[...] = (jax.nn.silu(_dot(a, w1b_ref[...])) * _dot(a, w3b_ref[...])).astype(BF16)


def _swiglu_up(a, w1, w3, layer, tm=1024, tn=512):
    m, k = a.shape
    n = w1.shape[2]
    return pl.pallas_call(
        _swiglu_kernel,
        grid=(n // tn, m // tm),
        in_specs=[pl.BlockSpec((tm, k), lambda j, i: (i, 0)),
                  _w_spec(k, tn, layer), _w_spec(k, tn, layer)],
        out_specs=pl.BlockSpec((tm, tn), lambda j, i: (i, j)),
        out_shape=jax.ShapeDtypeStruct((m, n), BF16),
        scratch_shapes=[pltpu.VMEM((k, tn), BF16), pltpu.VMEM((k, tn), BF16)],
        compiler_params=_params(2),
        name="swiglu_up",
    )(a, w1, w3)


def _resid_kernel(*refs, gate_idx, has_bias):
    if has_bias:
        a_ref, w_ref, b_ref, x_ref, mod_ref, o_ref, wb_ref = refs
    else:
        a_ref, w_ref, x_ref, mod_ref, o_ref, wb_ref = refs
    _cast_weights_once([(w_ref, wb_ref)])
    y = _dot(a_ref[...], wb_ref[...])
    if has_bias:
        y = y + b_ref[...]
    o_ref[...] = x_ref[...] + mod_ref[gate_idx:gate_idx + 1, :] * y


def _resid_proj(a, w, bias, layer, x, mod, seq_len, gate_idx, tm, tn):
    m, k = a.shape
    n = w.shape[2]
    seq = _seq_of_tile(tm, seq_len, mod.shape[0])
    in_specs = [pl.BlockSpec((tm, k), lambda j, i: (i, 0)), _w_spec(k, tn, layer)]
    args = [a, w]
    if bias is not None:
        in_specs.append(_vec_spec(tn, layer))
        args.append(bias.reshape(bias.shape[0], 1, n))
    in_specs += [pl.BlockSpec((tm, tn), lambda j, i: (i, j)),
                 pl.BlockSpec((None, 6, tn), lambda j, i: (seq(i), 0, j))]
    args += [x, mod]
    return pl.pallas_call(
        functools.partial(_resid_kernel, gate_idx=gate_idx, has_bias=bias is not None),
        grid=(n // tn, m // tm),
        in_specs=in_specs,
        out_specs=pl.BlockSpec((tm, tn), lambda j, i: (i, j)),
        out_shape=jax.ShapeDtypeStruct((m, n), F32),
        scratch_shapes=[pltpu.VMEM((k, tn), BF16)],
        compiler_params=_params(2),
        name="resid_proj",
    )(*args)


def _odd_dft_mats(p):
    f = np.arange(p, dtype=np.int64)
    m = np.arange(2 * p, dtype=np.int64)
    phase = ((2 * f[:, None] + 1) * m[None, :]) % (4 * p)
    ang = np.pi * phase.astype(np.float64) / (2 * p)
    fwd = np.concatenate([np.cos(ang), -np.sin(ang)], axis=0)
    fwd_lo = fwd[:, :p]
    fwd_hi = fwd[:, p:].copy()
    fwd_hi[:, 0] = 0.0
    t = np.arange(p, dtype=np.int64)
    phase_i = (t[:, None] * (2 * f[None, :] + 1)) % (4 * p)
    ang_i = np.pi * phase_i.astype(np.float64) / (2 * p)
    inv = np.concatenate([np.cos(ang_i), -np.sin(ang_i)], axis=1) / p
    as_bf16 = lambda a: jnp.asarray(a, F32).astype(BF16)
    return as_bf16(fwd_lo), as_bf16(fwd_hi), as_bf16(inv)


def _filter_mlp_kernel(w1_ref, b1_ref, fr_ref, w2_ref, b2_ref, o_ref, *, seq_len):
    rows = 2 * seq_len
    r = lax.broadcasted_iota(jnp.int32, (rows, LANES), 0)
    lane = lax.broadcasted_iota(jnp.int32, (rows, LANES), 1)
    t = jnp.abs(r - seq_len).astype(F32) / seq_len
    band = jnp.where(lane <= HY_BANDS, lane, lane - HY_BANDS).astype(F32)
    ang = 2.0 * math.pi * t * band
    feat = jnp.where(lane == 0, t,
                     jnp.where(lane <= HY_BANDS, jnp.cos(ang),
                               jnp.where(lane <= 2 * HY_BANDS, jnp.sin(ang), 0.0)))
    z = jnp.dot(feat, w1_ref[...], precision=HIGHEST, preferred_element_type=F32)
    z = jnp.sin(fr_ref[0:1, :] * (z + b1_ref[...]))
    z = jnp.dot(z, w2_ref[...], precision=HIGHEST, preferred_element_type=F32)
    o_ref[...] = jnp.sin(fr_ref[1:2, :] * (z + b2_ref[...]))


def _filter_time_kernel(z_ref, wf_ref, wb_ref, dl_ref, o_ref, *, seq_len):
    z = z_ref[...]
    ff = jnp.dot(z, wf_ref[...], precision=HIGHEST, preferred_element_type=F32)
    fb = jnp.dot(z, wb_ref[...], precision=HIGHEST, preferred_element_type=F32)
    n = lax.broadcasted_iota(jnp.int32, ff.shape, 0) - seq_len
    t = jnp.abs(n).astype(F32) / seq_len
    window = jnp.exp(-t * dl_ref[...])
    kk = jnp.where(n >= 0, ff, fb) * window
    o_ref[...] = jnp.where(n == -seq_len, 0.0, kk)


def _filter_spec_kernel(ka_ref, kb_ref, flo_ref, fhi_ref, o_ref):
    o_ref[...] = (_dot(flo_ref[...], ka_ref[...].astype(BF16))
                  - _dot(fhi_ref[...], kb_ref[...].astype(BF16)))


def _hyena_filter_spectra(seq_len, p, w1, b1, freq, w2, b2, w3, fwd_lo, fwd_hi, ct=512):
    hidden = w1.shape[1]
    n_order = 2
    d = w3.shape[1] // (2 * n_order)
    nb = seq_len // p
    nd = 2 * nb - 1
    rows = 2 * seq_len
    w1p = jnp.pad(w1, ((0, LANES - w1.shape[0]), (0, 0)))
    full = lambda shape: pl.BlockSpec(shape, lambda *_: (0,) * len(shape))
    z = pl.pallas_call(
        functools.partial(_filter_mlp_kernel, seq_len=seq_len),
        grid=(1,),
        in_specs=[full((LANES, hidden)), full((1, hidden)), full((2, hidden)),
                  full((hidden, hidden)), full((1, hidden))],
        out_specs=full((rows, hidden)),
        out_shape=jax.ShapeDtypeStruct((rows, hidden), F32),
        compiler_params=_params(1),
        name="filter_mlp",
    )(w1p, b1.reshape(1, hidden), freq, w2, b2.reshape(1, hidden))

    min_decay = abs(math.log(HY_TARGET) / HY_DECAY_PCT_LONG)
    max_decay = abs(math.log(HY_TARGET) / HY_DECAY_PCT_SHORT)
    deltas = jnp.linspace(min_decay, max_decay, d, dtype=F32).reshape(1, d)
    nct = d // ct
    kk = pl.pallas_call(
        functools.partial(_filter_time_kernel, seq_len=seq_len),
        grid=(n_order, nct),
        in_specs=[
            pl.BlockSpec((rows, hidden), lambda o, j: (0, 0)),
            pl.BlockSpec((hidden, ct), lambda o, j: (0, (2 * o) * nct + j)),
            pl.BlockSpec((hidden, ct), lambda o, j: (0, (2 * o + 1) * nct + j)),
            pl.BlockSpec((1, ct), lambda o, j: (0, j)),
        ],
        out_specs=pl.BlockSpec((None, rows, ct), lambda o, j: (o, 0, j)),
        out_shape=jax.ShapeDtypeStruct((n_order, rows, d), F32),
        compiler_params=_params(2),
        name="filter_time",
    )(z, w3, w3, deltas)

    return pl.pallas_call(
        _filter_spec_kernel,
        grid=(n_order, d // ct, nd),
        in_specs=[
            pl.BlockSpec((None, p, ct), lambda o, j, e: (o, e + 1, j)),
            pl.BlockSpec((None, p, ct), lambda o, j, e: (o, e, j)),
            pl.BlockSpec((2 * p, p), lambda o, j, e: (0, 0)),
            pl.BlockSpec((2 * p, p), lambda o, j, e: (0, 0)),
        ],
        out_specs=pl.BlockSpec((None, None, 2 * p, ct), lambda o, j, e: (o, e, 0, j)),
        out_shape=jax.ShapeDtypeStruct((n_order, nd, 2 * p, d), F32),
        compiler_params=_params(3),
        name="filter_spec",
    )(kk, kk, fwd_lo, fwd_hi)


def _long_conv_kernel(z_ref, gate_ref, k_ref, bias_ref, fwd_ref, inv_ref, o_ref,
                      zs_ref, ys_ref, *, p, nb):
    fwd = fwd_ref[...]
    inv = inv_ref[...]
    for b in range(nb):
        zs_ref[b] = _dot(fwd, z_ref[b * p:(b + 1) * p, :].astype(BF16))
    for a in range(nb):
        yr = None
        yi = None
        for b in range(nb):
            e = a - b + nb - 1
            kr = k_ref[e, 0:p, :]
            ki = k_ref[e, p:2 * p, :]
            zr = zs_ref[b, 0:p, :]
            zi = zs_ref[b, p:2 * p, :]
            tr = kr * zr - ki * zi
            ti = kr * zi + ki * zr
            yr = tr if yr is None else yr + tr
            yi = ti if yi is None else yi + ti
        ys_ref[0:p, :] = yr.astype(BF16)
        ys_ref[p:2 * p, :] = yi.astype(BF16)
        y = _dot(inv, ys_ref[...])
        rows = slice(a * p, (a + 1) * p)
        za = z_ref[rows, :].astype(F32)
        o_ref[rows, :] = (gate_ref[rows, :] * (y + bias_ref[...] * za)).astype(o_ref.dtype)


def _long_conv(z, z_col, gate, gate_col, spectra, order, bias, layer, fwd_lo, inv, seq_len, p,
               out_dtype, ct=256):
    m = z.shape[0]
    d = spectra.shape[-1]
    nb = seq_len // p
    nd = 2 * nb - 1
    zc, gc = z_col // ct, gate_col // ct
    return pl.pallas_call(
        functools.partial(_long_conv_kernel, p=p, nb=nb),
        grid=(d // ct, m // seq_len),
        in_specs=[
            pl.BlockSpec((seq_len, ct), lambda j, b: (b, zc + j)),
            pl.BlockSpec((seq_len, ct), lambda j, b: (b, gc + j)),
            pl.BlockSpec((None, nd, 2 * p, ct), lambda j, b: (order, 0, 0, j)),
            pl.BlockSpec((None, None, 1, ct), lambda j, b: (layer, order, 0, j)),
            pl.BlockSpec((2 * p, p), lambda j, b: (0, 0)),
            pl.BlockSpec((p, 2 * p), lambda j, b: (0, 0)),
        ],
        out_specs=pl.BlockSpec((seq_len, ct), lambda j, b: (b, j)),
        out_shape=jax.ShapeDtypeStruct((m, d), out_dtype),
        scratch_shapes=[pltpu.VMEM((nb, 2 * p, ct), F32), pltpu.VMEM((2 * p, ct), BF16)],
        compiler_params=_params(2),
        name="long_conv",
    )(z, gate, spectra, bias.reshape(bias.shape[0], bias.shape[1], 1, d), fwd_lo, inv)


def _retention_kernel(*refs, n_chunks, has_s0, want_state):
    it = iter(refs)
    dec_ref, q_ref, k_ref, v_ref, sg_ref, gn_ref = (next(it) for _ in range(6))
    s0_ref = next(it) if has_s0 else None
    o_ref = next(it)
    sfin_ref = next(it) if want_state else None
    acc_ref, s_ref, dmat_ref, xi_ref, zeta_ref = (next(it) for _ in range(5))

    c_len = RET_CHUNK
    dk = q_ref.shape[1]

    log_g = -jnp.exp(dec_ref[...])
    ii = lax.broadcasted_iota(jnp.int32, (c_len, c_len), 0)
    jj = lax.broadcasted_iota(jnp.int32, (c_len, c_len), 1)
    idx = lax.broadcasted_iota(jnp.int32, (c_len, 1), 0).astype(F32)
    g_chunk = []
    for direction in range(2):
        lg = log_g[direction]
        diff = (ii - jj) if direction == 0 else (jj - ii)
        dmat_ref[direction] = jnp.where(
            diff >= 0, jnp.exp(lg * jnp.maximum(diff, 0).astype(F32)), 0.0)
        if direction == 0:
            xi = jnp.exp(lg * (idx + 1.0))
            zeta = jnp.exp(lg * (c_len - 1.0 - idx))
        else:
            xi = jnp.exp(lg * (c_len - idx))
            zeta = jnp.exp(lg * idx)
        xi_ref[direction] = jnp.broadcast_to(xi, (c_len, dk))
        zeta_ref[direction] = jnp.broadcast_to(zeta, (c_len, dk))
        g_chunk.append(jnp.exp(lg * c_len))
        if has_s0:
            s_ref[direction] = s0_ref[direction].astype(F32)
        else:
            s_ref[direction] = jnp.zeros(s_ref.shape[1:], F32)

    def chunk(c):
        return pl.ds(pl.multiple_of(c * c_len, c_len), c_len)

    def scan_chunk(c, direction):
        rows = chunk(c)
        qc = q_ref[rows, :]
        kc = k_ref[rows, :]
        vc = v_ref[rows, :]
        s = s_ref[direction]
        scores = lax.dot_general(qc.astype(BF16), kc.astype(BF16), (((1,), (1,)), ((), ())),
                                 preferred_element_type=F32) * dmat_ref[direction]
        inner = _dot(scores.astype(BF16), vc)
        cross = _dot((qc * xi_ref[direction]).astype(BF16), s.astype(BF16))
        upd = lax.dot_general((kc * zeta_ref[direction]).astype(BF16), vc,
                              (((0,), (0,)), ((), ())), preferred_element_type=F32)
        s_ref[direction] = g_chunk[direction] * s + upd
        return inner + cross

    if n_chunks == 1:
        acc_ref[...] = scan_chunk(0, 0) + scan_chunk(0, 1)
    else:
        assert n_chunks % 2 == 0

        def first_visit(t, carry):
            acc_ref[chunk(t), :] = scan_chunk(t, 0)
            acc_ref[chunk(n_chunks - 1 - t), :] = scan_chunk(n_chunks - 1 - t, 1)
            return carry

        def second_visit(t, carry):
            cf, cb = chunk(t), chunk(n_chunks - 1 - t)
            acc_ref[cf, :] = acc_ref[cf, :] + scan_chunk(t, 0)
            acc_ref[cb, :] = acc_ref[cb, :] + scan_chunk(n_chunks - 1 - t, 1)
            return carry

        lax.fori_loop(0, n_chunks // 2, first_visit, 0)
        lax.fori_loop(n_chunks // 2, n_chunks, second_visit, 0)

    if want_state:
        for direction in range(2):
            sfin_ref[direction] = s_ref[direction].astype(sfin_ref.dtype)

    seq_len = q_ref.shape[0]
    norm_rows = min(NORM_ROWS, seq_len)

    def finish(r, carry):
        rows = pl.ds(pl.multiple_of(r * norm_rows, norm_rows), norm_rows)
        o = acc_ref[rows, :]
        mu = jnp.mean(o, axis=-1, keepdims=True)
        var = jnp.mean(jnp.square(o - mu), axis=-1, keepdims=True)
        o = (o - mu) * lax.rsqrt(var + EPS)
        o = o * gn_ref[...]
        o_ref[rows, :] = (o * sg_ref[rows, :]).astype(o_ref.dtype)
        return carry

    lax.fori_loop(0, seq_len // norm_rows, finish, 0)


def _retention(qk, v, sg, decay, gn, layer, seq_len, s0, want_state, state_dtype):
    m = qk.shape[0]
    n_seq = m // seq_len
    nh = RET_HEADS
    dv = v.shape[1] // nh
    dk = qk.shape[1] // (2 * nh)
    has_s0 = s0 is not None
    c_len = RET_CHUNK

    in_specs = [
        pl.BlockSpec((None, 2, None, 1, 1), lambda b, h: (layer, 0, h, 0, 0)),
        pl.BlockSpec((seq_len, dk), lambda b, h: (b, h)),
        pl.BlockSpec((seq_len, dk), lambda b, h: (b, nh + h)),
        pl.BlockSpec((seq_len, dv), lambda b, h: (b, h)),
        pl.BlockSpec((seq_len, dv), lambda b, h: (b, h)),
        pl.BlockSpec((None, 1, dv), lambda b, h: (layer, 0, h)),
    ]
    args = [decay.reshape(decay.shape[0], 2, nh, 1, 1), qk, qk, v, sg,
            gn.reshape(gn.shape[0], 1, nh * dv)]
    state_spec = pl.BlockSpec((None, None, 2, None, dk, dv),
                              lambda b, h: (b, layer, 0, h, 0, 0))
    if has_s0:
        in_specs.append(state_spec)
        args.append(s0)
    out_specs = [pl.BlockSpec((seq_len, dv), lambda b, h: (b, h))]
    out_shape = [jax.ShapeDtypeStruct((m, nh * dv), BF16)]
    if want_state:
        out_specs.append(state_spec)
        out_shape.append(jax.ShapeDtypeStruct((n_seq, decay.shape[0], 2, nh, dk, dv),
                                              state_dtype))
    outs = pl.pallas_call(
        functools.partial(_retention_kernel, n_chunks=seq_len // c_len, has_s0=has_s0,
                          want_state=want_state),
        grid=(n_seq, nh),
        in_specs=in_specs,
        out_specs=out_specs,
        out_shape=out_shape,
        scratch_shapes=[pltpu.VMEM((seq_len, dv), F32), pltpu.VMEM((2, dk, dv), F32),
                        pltpu.VMEM((2, c_len, c_len), F32), pltpu.VMEM((2, c_len, dk), F32),
                        pltpu.VMEM((2, c_len, dk), F32)],
        compiler_params=_params(2),
        name="retention",
    )(*args)
    return (outs[0], outs[1]) if want_state else (outs[0], None)


def _rope_tables(seq_len, dk):
    rows = seq_len // GRID_W
    pos_row = jnp.repeat(jnp.arange(rows, dtype=F32), GRID_W)
    pos_col = jnp.tile(jnp.arange(GRID_W, dtype=F32), rows)
    n = dk // 4
    inv = jnp.exp(-math.log(ROPE_BASE) * jnp.arange(n, dtype=F32) / n)
    ang_r = pos_row[:, None] * inv[None, :]
    ang_c = pos_col[:, None] * inv[None, :]
    cos = jnp.concatenate([jnp.cos(ang_r)] * 2 + [jnp.cos(ang_c)] * 2, axis=-1)
    sin = jnp.concatenate([-jnp.sin(ang_r), jnp.sin(ang_r), -jnp.sin(ang_c), jnp.sin(ang_c)],
                          axis=-1)
    return cos, sin


def _trunk(x, mod, seq_len, conv_block, tm, rope_tabs, s0, want_state, p):
    m, d = x.shape
    mod0, mod1 = mod[0], mod[1]
    nh = RET_HEADS
    dv = p['ret_gn'].shape[1] // nh
    dk = (p['ret_w_in'].shape[2] - 2 * nh * dv) // (2 * nh)

    h = _norm_mod(x, p['norm_mix'], 0, mod0, seq_len, 0, 1)
    u = _hy_in(h, p['hy_w_in'], p['hy_b_in'], p['hy_conv_w'], p['hy_conv_b'], 0, seq_len,
               tm=max(tm, seq_len))
    fwd_lo, fwd_hi, inv = _odd_dft_mats(conv_block)
    spectra = _hyena_filter_spectra(
        seq_len, conv_block, p['hy_filt_w1'][0], p['hy_filt_b1'][0], p['hy_filt_freq'][0],
        p['hy_filt_w2'][0], p['hy_filt_b2'][0], p['hy_filt_w3'][0], fwd_lo, fwd_hi)
    z = _long_conv(u, 0, u, d, spectra, 0, p['hy_bias_d'], 0, fwd_lo, inv, seq_len,
                   conv_block, F32)
    z = _long_conv(z, 0, u, 2 * d, spectra, 1, p['hy_bias_d'], 0, fwd_lo, inv, seq_len,
                   conv_block, BF16)
    x = _resid_proj(z, p['hy_w_out'], p['hy_b_out'], 0, x, mod0, seq_len, 2, tm, 1024)
    h = _norm_mod(x, p['norm_ffn'], 0, mod0, seq_len, 3, 4)
    a = _swiglu_up(h, p['ffn_w1'], p['ffn_w3'], 0, tm=tm)
    x = _resid_proj(a, p['ffn_w2'], None, 0, x, mod0, seq_len, 5, tm // 2, 512)

    h = _norm_mod(x, p['norm_mix'], 1, mod1, seq_len, 0, 1)
    qk = _ret_qk_proj(h, p['ret_w_in'], 0, nh, dk, seq_len, rope_tabs, tm=tm)
    v = _proj(h, p['ret_w_in'], 0, 2 * nh * dk, nh * dv, BF16, tm=tm)
    sg = _proj(h, p['ret_w_in'], 0, 2 * nh * dk + nh * dv, nh * dv, F32, silu=True, tm=tm)
    og, s_fin = _retention(qk, v, sg, p['ret_decay'], p['ret_gn'], 0, seq_len, s0,
                           want_state, x.dtype)
    x = _resid_proj(og, p['ret_w_out'], None, 0, x, mod1, seq_len, 2, tm, 512)
    h = _norm_mod(x, p['norm_ffn'], 1, mod1, seq_len, 3, 4)
    a = _swiglu_up(h, p['ffn_w1'], p['ffn_w3'], 1, tm=tm)
    x = _resid_proj(a, p['ffn_w2'], None, 1, x, mod1, seq_len, 5, tm // 2, 512)

    return _final_norm(x, p['norm_final']), s_fin


def kernel(x_prompt, x_sample, state_ret, c, c_ctx, w_ada, b_ada, norm_mix, norm_ffn, norm_final,
           ffn_w1, ffn_w3, ffn_w2, hy_w_in, hy_b_in, hy_conv_w, hy_conv_b, hy_filt_w1, hy_filt_b1,
           hy_filt_freq, hy_filt_w2, hy_filt_b2, hy_filt_w3, hy_bias_d, hy_w_out, hy_b_out,
           ret_w_in, ret_decay, ret_gn, ret_w_out):
    p = dict(norm_mix=norm_mix, norm_ffn=norm_ffn, norm_final=norm_final,
             ffn_w1=ffn_w1, ffn_w3=ffn_w3, ffn_w2=ffn_w2, hy_w_in=hy_w_in, hy_b_in=hy_b_in,
             hy_conv_w=hy_conv_w, hy_conv_b=hy_conv_b, hy_filt_w1=hy_filt_w1,
             hy_filt_b1=hy_filt_b1, hy_filt_freq=hy_filt_freq, hy_filt_w2=hy_filt_w2,
             hy_filt_b2=hy_filt_b2, hy_filt_w3=hy_filt_w3, hy_bias_d=hy_bias_d,
             hy_w_out=hy_w_out, hy_b_out=hy_b_out, ret_w_in=ret_w_in, ret_decay=ret_decay,
             ret_gn=ret_gn, ret_w_out=ret_w_out)
    n_ctx, ctx_len, d = x_prompt.shape
    n_dec, dec_len, _ = x_sample.shape
    depth = w_ada.shape[0]

    cond_rows = 16
    cond = jnp.concatenate(
        [c_ctx[None, :], c, jnp.zeros((cond_rows - 1 - n_dec, d), c.dtype)], axis=0)
    mod = _ada_mod(cond, w_ada, b_ada).reshape(depth, cond_rows, 6, d)
    mod_ctx = mod[:, 0:1]
    mod_dec = mod[:, 1:1 + n_dec]

    y_prompt, ctx_state = _trunk(
        x_prompt.reshape(n_ctx * ctx_len, d), mod_ctx, ctx_len, conv_block=ctx_len, tm=1024,
        rope_tabs=None, s0=None, want_state=True, p=p)
    dk = state_ret.shape[-2]
    y_sample, _ = _trunk(
        x_sample.reshape(n_dec * dec_len, d), mod_dec, dec_len, conv_block=512, tm=1024,
        rope_tabs=_rope_tables(dec_len, dk), s0=state_ret, want_state=False, p=p)

    return (y_prompt.reshape(x_prompt.shape), y_sample.reshape(x_sample.shape), ctx_state)
```

```python
import functools
import math

import numpy as np
import jax
import jax.numpy as jnp
from jax import lax
from jax.experimental import pallas as pl
from jax.experimental.pallas import tpu as pltpu

F32 = jnp.float32
BF16 = jnp.bfloat16

EPS = 1e-6
GRID_W = 64
HY_BANDS = 16
HY_DECAY_PCT_SHORT = 0.3
HY_DECAY_PCT_LONG = 1.5
HY_TARGET = 1e-2
RET_HEADS = 8
ROPE_BASE = 10000.0
RET_CHUNK = 256
NORM_ROWS = 512

V7X_VMEM_BYTES = 64 * 1024 * 1024
VMEM_LIMIT = V7X_VMEM_BYTES - 8 * 1024 * 1024
LANES = 128
HIGHEST = lax.Precision.HIGHEST


def _params(n_axes):
    return pltpu.CompilerParams(
        dimension_semantics=("arbitrary",) * n_axes, vmem_limit_bytes=VMEM_LIMIT)


def _dot(a, b):
    return jnp.dot(a, b, preferred_element_type=F32)


def _ada_kernel(c_ref, w_ref, b_ref, o_ref):
    s = jax.nn.silu(c_ref[...]).astype(BF16)
    o_ref[...] = _dot(s, w_ref[...].astype(BF16)) + b_ref[...]


def _ada_mod(cond, w_ada, b_ada, tn=1536):
    depth, d, n = w_ada.shape
    rows = cond.shape[0]
    return pl.pallas_call(
        _ada_kernel,
        grid=(depth, n // tn),
        in_specs=[
            pl.BlockSpec((rows, d), lambda l, j: (0, 0)),
            pl.BlockSpec((None, d, tn), lambda l, j: (l, 0, j)),
            pl.BlockSpec((None, 1, tn), lambda l, j: (l, 0, j)),
        ],
        out_specs=pl.BlockSpec((None, rows, tn), lambda l, j: (l, 0, j)),
        out_shape=jax.ShapeDtypeStruct((depth, rows, n), F32),
        compiler_params=_params(2),
        name="ada_mod",
    )(cond, w_ada, b_ada.reshape(depth, 1, n))


def _seq_of_tile(tm, seq_len, n_mod):
    if n_mod == 1:
        return lambda i: 0
    assert seq_len % tm == 0
    return lambda i: (i * tm) // seq_len


def _norm_mod_kernel(x_ref, g_ref, mod_ref, o_ref, *, shift_idx, scale_idx):
    x = x_ref[...]
    y = x * lax.rsqrt(jnp.mean(x * x, axis=-1, keepdims=True) + EPS)
    y = y * g_ref[...]
    scale = mod_ref[scale_idx:scale_idx + 1, :]
    shift = mod_ref[shift_idx:shift_idx + 1, :]
    o_ref[...] = (y * (1.0 + scale) + shift).astype(o_ref.dtype)


def _norm_mod(x, g, layer, mod, seq_len, shift_idx, scale_idx, tm=512):
    m, d = x.shape
    seq = _seq_of_tile(tm, seq_len, mod.shape[0])
    return pl.pallas_call(
        functools.partial(_norm_mod_kernel, shift_idx=shift_idx, scale_idx=scale_idx),
        grid=(m // tm,),
        in_specs=[
            pl.BlockSpec((tm, d), lambda i: (i, 0)),
            pl.BlockSpec((None, 1, d), lambda i: (layer, 0, 0)),
            pl.BlockSpec((None, 6, d), lambda i: (seq(i), 0, 0)),
        ],
        out_specs=pl.BlockSpec((tm, d), lambda i: (i, 0)),
        out_shape=jax.ShapeDtypeStruct((m, d), BF16),
        compiler_params=_params(1),
        name="norm_mod",
    )(x, g.reshape(g.shape[0], 1, d), mod)


def _final_norm_kernel(x_ref, g_ref, o_ref):
    x = x_ref[...]
    y = x * lax.rsqrt(jnp.mean(x * x, axis=-1, keepdims=True) + EPS)
    o_ref[...] = y * g_ref[...]


def _final_norm(x, g, tm=512):
    m, d = x.shape
    return pl.pallas_call(
        _final_norm_kernel,
        grid=(m // tm,),
        in_specs=[pl.BlockSpec((tm, d), lambda i: (i, 0)),
                  pl.BlockSpec((1, d), lambda i: (0, 0))],
        out_specs=pl.BlockSpec((tm, d), lambda i: (i, 0)),
        out_shape=jax.ShapeDtypeStruct((m, d), F32),
        compiler_params=_params(1),
        name="final_norm",
    )(x, g.reshape(1, d))


def _cast_weights_once(pairs):
    @pl.when(pl.program_id(1) == 0)
    def _():
        for w_ref, wb_ref in pairs:
            wb_ref[...] = w_ref[...].astype(BF16)


def _w_spec(k, tn, layer, col0=0):
    assert col0 % tn == 0
    return pl.BlockSpec((None, k, tn), lambda j, i: (layer, 0, col0 // tn + j))


def _vec_spec(tn, layer, col0=0):
    return pl.BlockSpec((None, 1, tn), lambda j, i: (layer, 0, col0 // tn + j))


def _rotate_pairs(x, half):
    parts = [pltpu.roll(x[:, g * half:(g + 1) * half], half // 2, 1)
             for g in range(x.shape[1] // half)]
    return jnp.concatenate(parts, axis=1)


def _ret_qk_kernel(*refs, dk, k_tile0, k_scale, use_rope):
    if use_rope:
        a_ref, w_ref, cos_ref, sin_ref, o_ref, wb_ref = refs
    else:
        a_ref, w_ref, o_ref, wb_ref = refs
    _cast_weights_once([(w_ref, wb_ref)])
    y = _dot(a_ref[...], wb_ref[...])
    y = y * jnp.where(pl.program_id(0) >= k_tile0, k_scale, 1.0).astype(F32)
    for hb in range(y.shape[1] // dk):
        x = y[:, hb * dk:(hb + 1) * dk]
        if use_rope:
            x = x * cos_ref[...] + _rotate_pairs(x, dk // 2) * sin_ref[...]
        o_ref[hb] = x


def _ret_qk_proj(a, w, layer, nh, dk, seq_len, rope_tabs, tm=1024, tn=1024):
    m, k = a.shape
    n = 2 * nh * dk
    use_rope = rope_tabs is not None
    in_specs = [pl.BlockSpec((tm, k), lambda j, i: (i, 0)), _w_spec(k, tn, layer)]
    args = [a, w]
    if use_rope:
        assert seq_len % tm == 0
        tiles_per_seq = seq_len // tm
        tab_spec = pl.BlockSpec((tm, dk), lambda j, i: (i % tiles_per_seq, 0))
        in_specs += [tab_spec, tab_spec]
        args += list(rope_tabs)
    return pl.pallas_call(
        functools.partial(_ret_qk_kernel, dk=dk, k_tile0=(nh * dk) // tn, k_scale=dk ** -0.5,
                          use_rope=use_rope),
        grid=(n // tn, m // tm),
        in_specs=in_specs,
        out_specs=pl.BlockSpec((tn // dk, tm, dk), lambda j, i: (j, i, 0)),
        out_shape=jax.ShapeDtypeStruct((n // dk, m, dk), F32),
        scratch_shapes=[pltpu.VMEM((k, tn), BF16)],
        compiler_params=_params(2),
        name="ret_qk_proj",
    )(*args)


def _proj_kernel(a_ref, w_ref, o_ref, wb_ref, *, silu):
    _cast_weights_once([(w_ref, wb_ref)])
    y = _dot(a_ref[...], wb_ref[...])
    if silu:
        y = jax.nn.silu(y)
    oc = o_ref.shape[2]
    for c in range(o_ref.shape[0]):
        o_ref[c] = y[:, c * oc:(c + 1) * oc].astype(o_ref.dtype)


def _proj(a, w, layer, col0, n, oc, out_dtype, silu=False, tm=1024, tn=1024):
    m, k = a.shape
    return pl.pallas_call(
        functools.partial(_proj_kernel, silu=silu),
        grid=(n // tn, m // tm),
        in_specs=[pl.BlockSpec((tm, k), lambda j, i: (i, 0)), _w_spec(k, tn, layer, col0)],
        out_specs=pl.BlockSpec((tn // oc, tm, oc), lambda j, i: (j, i, 0)),
        out_shape=jax.ShapeDtypeStruct((n // oc, m, oc), out_dtype),
        scratch_shapes=[pltpu.VMEM((k, tn), BF16)],
        compiler_params=_params(2),
        name="proj",
    )(a, w)


def _hy_in_kernel(a_ref, w_ref, b_ref, cw_ref, cb_ref, o_ref, wb_ref, *, seq_len, row_blocks):
    _cast_weights_once([(w_ref, wb_ref)])
    n_slab, tm, ct = o_ref.shape
    rq = tm // row_blocks
    wb = wb_ref[...]
    bias = b_ref[...]
    us = [_dot(a_ref[q * rq:(q + 1) * rq, :], wb) + bias for q in range(row_blocks)]
    w0, w1, w2, cb = cw_ref[0:1, :], cw_ref[1:2, :], cw_ref[2:3, :], cb_ref[...]

    def tap_sum(prev, cur, nxt):
        acc = cur * w1
        if prev is not None:
            acc = prev * w0 + acc
        if nxt is not None:
            acc = acc + nxt * w2
        return acc + cb

    def store(rows, val):
        for s in range(n_slab):
            o_ref[s, rows, :] = val[:, s * ct:(s + 1) * ct]

    for q, u in enumerate(us):
        r0 = q * rq
        store(slice(r0, r0 + rq),
              tap_sum(pltpu.roll(u, 1, 0), u, pltpu.roll(u, rq - 1, 0)))
        prev = None if r0 % seq_len == 0 else us[q - 1][rq - 1:rq]
        store(slice(r0, r0 + 1), tap_sum(prev, u[0:1], u[1:2]))
        nxt = None if (r0 + rq) % seq_len == 0 else us[q + 1][0:1]
        store(slice(r0 + rq - 1, r0 + rq), tap_sum(u[rq - 2:rq - 1], u[rq - 1:rq], nxt))
        for r in range(seq_len, rq, seq_len):
            store(slice(r0 + r, r0 + r + 1), tap_sum(None, u[r:r + 1], u[r + 1:r + 2]))
            store(slice(r0 + r - 1, r0 + r), tap_sum(u[r - 2:r - 1], u[r - 1:r], None))


def _hy_in(a, w, b, cw, cb, layer, seq_len, tm, tn=512, ct=256, row_blocks=4):
    m, k = a.shape
    n = w.shape[2]
    assert tm % seq_len == 0
    taps = cw.shape[1]
    return pl.pallas_call(
        functools.partial(_hy_in_kernel, seq_len=seq_len, row_blocks=row_blocks),
        grid=(n // tn, m // tm),
        in_specs=[
            pl.BlockSpec((tm, k), lambda j, i: (i, 0)),
            _w_spec(k, tn, layer),
            _vec_spec(tn, layer),
            pl.BlockSpec((None, taps, tn), lambda j, i: (layer, 0, j)),
            _vec_spec(tn, layer),
        ],
        out_specs=pl.BlockSpec((tn // ct, tm, ct), lambda j, i: (j, i, 0)),
        out_shape=jax.ShapeDtypeStruct((n // ct, m, ct), F32),
        scratch_shapes=[pltpu.VMEM((k, tn), BF16)],
        compiler_params=_params(2),
        name="hy_in",
    )(a, w, b.reshape(b.shape[0], 1, n), cw, cb.reshape(cb.shape[0], 1, n))


def _swiglu_kernel(a_ref, w1_ref, w3_ref, o_ref, w1b_ref, w3b_ref):
    _cast_weights_once([(w1_ref, w1b_ref), (w3_ref, w3b_ref)])
    a = a_ref[...]
    o_ref[...] = (jax.nn.silu(_dot(a, w1b_ref[...])) * _dot(a, w3b_ref[...])).astype(BF16)


def _swiglu_up(a, w1, w3, layer, tm=1024, tn=512):
    m, k = a.shape
    n = w1.shape[2]
    return pl.pallas_call(
        _swiglu_kernel,
        grid=(n // tn, m // tm),
        in_specs=[pl.BlockSpec((tm, k), lambda j, i: (i, 0)),
                  _w_spec(k, tn, layer), _w_spec(k, tn, layer)],
        out_specs=pl.BlockSpec((tm, tn), lambda j, i: (i, j)),
        out_shape=jax.ShapeDtypeStruct((m, n), BF16),
        scratch_shapes=[pltpu.VMEM((k, tn), BF16), pltpu.VMEM((k, tn), BF16)],
        compiler_params=_params(2),
        name="swiglu_up",
    )(a, w1, w3)


def _resid_kernel(*refs, gate_idx, has_bias):
    if has_bias:
        a_ref, w_ref, b_ref, x_ref, mod_ref, o_ref, wb_ref = refs
    else:
        a_ref, w_ref, x_ref, mod_ref, o_ref, wb_ref = refs
    _cast_weights_once([(w_ref, wb_ref)])
    if len(a_ref.shape) == 3:
        a = jnp.concatenate([a_ref[c] for c in range(a_ref.shape[0])], axis=1)
    else:
        a = a_ref[...]
    y = _dot(a, wb_ref[...])
    if has_bias:
        y = y + b_ref[...]
    o_ref[...] = x_ref[...] + mod_ref[gate_idx:gate_idx + 1, :] * y


def _resid_proj(a, w, bias, layer, x, mod, seq_len, gate_idx, tm, tn):
    n = w.shape[2]
    seq = _seq_of_tile(tm, seq_len, mod.shape[0])
    if a.ndim == 3:
        n_slab, m, cw = a.shape
        k = n_slab * cw
        a_spec = pl.BlockSpec((n_slab, tm, cw), lambda j, i: (0, i, 0))
    else:
        m, k = a.shape
        a_spec = pl.BlockSpec((tm, k), lambda j, i: (i, 0))
    in_specs = [a_spec, _w_spec(k, tn, layer)]
    args = [a, w]
    if bias is not None:
        in_specs.append(_vec_spec(tn, layer))
        args.append(bias.reshape(bias.shape[0], 1, n))
    in_specs += [pl.BlockSpec((tm, tn), lambda j, i: (i, j)),
                 pl.BlockSpec((None, 6, tn), lambda j, i: (seq(i), 0, j))]
    args += [x, mod]
    return pl.pallas_call(
        functools.partial(_resid_kernel, gate_idx=gate_idx, has_bias=bias is not None),
        grid=(n // tn, m // tm),
        in_specs=in_specs,
        out_specs=pl.BlockSpec((tm, tn), lambda j, i: (i, j)),
        out_shape=jax.ShapeDtypeStruct((m, n), F32),
        scratch_shapes=[pltpu.VMEM((k, tn), BF16)],
        compiler_params=_params(2),
        name="resid_proj",
    )(*args)


def _odd_dft_mats(p):
    f = np.arange(p, dtype=np.int64)
    m = np.arange(2 * p, dtype=np.int64)
    phase = ((2 * f[:, None] + 1) * m[None, :]) % (4 * p)
    ang = np.pi * phase.astype(np.float64) / (2 * p)
    fwd = np.concatenate([np.cos(ang), -np.sin(ang)], axis=0)
    fwd_lo = fwd[:, :p]
    fwd_hi = fwd[:, p:].copy()
    fwd_hi[:, 0] = 0.0
    t = np.arange(p, dtype=np.int64)
    phase_i = (t[:, None] * (2 * f[None, :] + 1)) % (4 * p)
    ang_i = np.pi * phase_i.astype(np.float64) / (2 * p)
    inv = np.concatenate([np.cos(ang_i), -np.sin(ang_i)], axis=1) / p
    as_bf16 = lambda a: jnp.asarray(a, F32).astype(BF16)
    return as_bf16(fwd_lo), as_bf16(fwd_hi), as_bf16(inv)


def _filter_mlp_kernel(w1_ref, b1_ref, fr_ref, w2_ref, b2_ref, o_ref, *, seq_len):
    rows = 2 * seq_len
    r = lax.broadcasted_iota(jnp.int32, (rows, LANES), 0)
    lane = lax.broadcasted_iota(jnp.int32, (rows, LANES), 1)
    t = jnp.abs(r - seq_len).astype(F32) / seq_len
    band = jnp.where(lane <= HY_BANDS, lane, lane - HY_BANDS).astype(F32)
    ang = 2.0 * math.pi * t * band
    feat = jnp.where(lane == 0, t,
                     jnp.where(lane <= HY_BANDS, jnp.cos(ang),
                               jnp.where(lane <= 2 * HY_BANDS, jnp.sin(ang), 0.0)))
    z = jnp.dot(feat, w1_ref[...], precision=HIGHEST, preferred_element_type=F32)
    z = jnp.sin(fr_ref[0:1, :] * (z + b1_ref[...]))
    z = jnp.dot(z, w2_ref[...], precision=HIGHEST, preferred_element_type=F32)
    o_ref[...] = jnp.sin(fr_ref[1:2, :] * (z + b2_ref[...]))


def _filter_time_kernel(z_ref, wf_ref, wb_ref, dl_ref, o_ref, *, seq_len):
    z = z_ref[...].astype(BF16)
    ff = _dot(z, wf_ref[...].astype(BF16))
    fb = _dot(z, wb_ref[...].astype(BF16))
    n = lax.broadcasted_iota(jnp.int32, ff.shape, 0) - seq_len
    t = jnp.abs(n).astype(F32) / seq_len
    window = jnp.exp(-t * dl_ref[...])
    kk = jnp.where(n >= 0, ff, fb) * window
    o_ref[...] = jnp.where(n == -seq_len, 0.0, kk)


def _filter_spec_kernel(ka_ref, kb_ref, flo_ref, fhi_ref, o_ref):
    o_ref[...] = (_dot(flo_ref[...], ka_ref[...].astype(BF16))
                  - _dot(fhi_ref[...], kb_ref[...].astype(BF16)))


def _hyena_filter_spectra(seq_len, p, w1, b1, freq, w2, b2, w3, fwd_lo, fwd_hi, ct=512):
    hidden = w1.shape[1]
    n_order = 2
    d = w3.shape[1] // (2 * n_order)
    nb = seq_len // p
    nd = 2 * nb - 1
    rows = 2 * seq_len
    w1p = jnp.pad(w1, ((0, LANES - w1.shape[0]), (0, 0)))
    full = lambda shape: pl.BlockSpec(shape, lambda *_: (0,) * len(shape))
    z = pl.pallas_call(
        functools.partial(_filter_mlp_kernel, seq_len=seq_len),
        grid=(1,),
        in_specs=[full((LANES, hidden)), full((1, hidden)), full((2, hidden)),
                  full((hidden, hidden)), full((1, hidden))],
        out_specs=full((rows, hidden)),
        out_shape=jax.ShapeDtypeStruct((rows, hidden), F32),
        compiler_params=_params(1),
        name="filter_mlp",
    )(w1p, b1.reshape(1, hidden), freq, w2, b2.reshape(1, hidden))

    min_decay = abs(math.log(HY_TARGET) / HY_DECAY_PCT_LONG)
    max_decay = abs(math.log(HY_TARGET) / HY_DECAY_PCT_SHORT)
    deltas = jnp.linspace(min_decay, max_decay, d, dtype=F32).reshape(1, d)
    nct = d // ct
    kk = pl.pallas_call(
        functools.partial(_filter_time_kernel, seq_len=seq_len),
        grid=(n_order, nct),
        in_specs=[
            pl.BlockSpec((rows, hidden), lambda o, j: (0, 0)),
            pl.BlockSpec((hidden, ct), lambda o, j: (0, (2 * o) * nct + j)),
            pl.BlockSpec((hidden, ct), lambda o, j: (0, (2 * o + 1) * nct + j)),
            pl.BlockSpec((1, ct), lambda o, j: (0, j)),
        ],
        out_specs=pl.BlockSpec((None, rows, ct), lambda o, j: (o, 0, j)),
        out_shape=jax.ShapeDtypeStruct((n_order, rows, d), F32),
        compiler_params=_params(2),
        name="filter_time",
    )(z, w3, w3, deltas)

    return pl.pallas_call(
        _filter_spec_kernel,
        grid=(n_order, d // ct, nd),
        in_specs=[
            pl.BlockSpec((None, p, ct), lambda o, j, e: (o, e + 1, j)),
            pl.BlockSpec((None, p, ct), lambda o, j, e: (o, e, j)),
            pl.BlockSpec((2 * p, p), lambda o, j, e: (0, 0)),
            pl.BlockSpec((2 * p, p), lambda o, j, e: (0, 0)),
        ],
        out_specs=pl.BlockSpec((None, None, 2 * p, ct), lambda o, j, e: (o, e, 0, j)),
        out_shape=jax.ShapeDtypeStruct((n_order, nd, 2 * p, d), F32),
        compiler_params=_params(3),
        name="filter_spec",
    )(kk, kk, fwd_lo, fwd_hi)


def _long_conv_kernel(z_ref, gate_ref, k_ref, bias_ref, fwd_ref, inv_ref, o_ref,
                      zs_ref, ys_ref, *, p, nb, n_seq):
    fwd = fwd_ref[...]
    inv = inv_ref[...]
    seq_len = p * nb
    for s in range(n_seq):
        base = s * seq_len
        for b in range(nb):
            zs_ref[b] = _dot(fwd, z_ref[base + b * p:base + (b + 1) * p, :].astype(BF16))
        for a in range(nb):
            yr = None
            yi = None
            for b in range(nb):
                e = a - b + nb - 1
                kr = k_ref[e, 0:p, :]
                ki = k_ref[e, p:2 * p, :]
                zr = zs_ref[b, 0:p, :]
                zi = zs_ref[b, p:2 * p, :]
                tr = kr * zr - ki * zi
                ti = kr * zi + ki * zr
                yr = tr if yr is None else yr + tr
                yi = ti if yi is None else yi + ti
            ys_ref[0:p, :] = yr.astype(BF16)
            ys_ref[p:2 * p, :] = yi.astype(BF16)
            y = _dot(inv, ys_ref[...])
            rows = slice(base + a * p, base + (a + 1) * p)
            za = z_ref[rows, :].astype(F32)
            o_ref[rows, :] = (gate_ref[rows, :] * (y + bias_ref[...] * za)).astype(o_ref.dtype)


def _long_conv(z, z_slab, gate, gate_slab, spectra, order, bias, layer, fwd_lo, inv, seq_len, p,
               out_dtype, seqs_per_step=1):
    _, m, ct = z.shape
    d = spectra.shape[-1]
    nb = seq_len // p
    nd = 2 * nb - 1
    rows = seqs_per_step * seq_len
    return pl.pallas_call(
        functools.partial(_long_conv_kernel, p=p, nb=nb, n_seq=seqs_per_step),
        grid=(d // ct, m // rows),
        in_specs=[
            pl.BlockSpec((None, rows, ct), lambda j, b: (z_slab + j, b, 0)),
            pl.BlockSpec((None, rows, ct), lambda j, b: (gate_slab + j, b, 0)),
            pl.BlockSpec((None, nd, 2 * p, ct), lambda j, b: (order, 0, 0, j)),
            pl.BlockSpec((None, None, 1, ct), lambda j, b: (layer, order, 0, j)),
            pl.BlockSpec((2 * p, p), lambda j, b: (0, 0)),
            pl.BlockSpec((p, 2 * p), lambda j, b: (0, 0)),
        ],
        out_specs=pl.BlockSpec((None, rows, ct), lambda j, b: (j, b, 0)),
        out_shape=jax.ShapeDtypeStruct((d // ct, m, ct), out_dtype),
        scratch_shapes=[pltpu.VMEM((nb, 2 * p, ct), F32), pltpu.VMEM((2 * p, ct), BF16)],
        compiler_params=_params(2),
        name="long_conv",
    )(z, gate, spectra, bias.reshape(bias.shape[0], bias.shape[1], 1, d), fwd_lo, inv)


def _retention_kernel(*refs, n_chunks, has_s0, want_state):
    it = iter(refs)
    dec_ref, q_ref, k_ref, v_ref, sg_ref, gn_ref = (next(it) for _ in range(6))
    s0_ref = next(it) if has_s0 else None
    o_ref = next(it)
    sfin_ref = next(it) if want_state else None
    acc_ref, s_ref, dmat_ref, xi_ref, zeta_ref = (next(it) for _ in range(5))

    c_len = RET_CHUNK
    dk = q_ref.shape[1]

    log_g = -jnp.exp(dec_ref[...])
    ii = lax.broadcasted_iota(jnp.int32, (c_len, c_len), 0)
    jj = lax.broadcasted_iota(jnp.int32, (c_len, c_len), 1)
    idx = lax.broadcasted_iota(jnp.int32, (c_len, 1), 0).astype(F32)
    g_chunk = []
    for direction in range(2):
        lg = log_g[direction]
        diff = (ii - jj) if direction == 0 else (jj - ii)
        dmat_ref[direction] = jnp.where(
            diff >= 0, jnp.exp(lg * jnp.maximum(diff, 0).astype(F32)), 0.0)
        if direction == 0:
            xi = jnp.exp(lg * (idx + 1.0))
            zeta = jnp.exp(lg * (c_len - 1.0 - idx))
        else:
            xi = jnp.exp(lg * (c_len - idx))
            zeta = jnp.exp(lg * idx)
        xi_ref[direction] = jnp.broadcast_to(xi, (c_len, dk))
        zeta_ref[direction] = jnp.broadcast_to(zeta, (c_len, dk))
        g_chunk.append(jnp.exp(lg * c_len))
        if has_s0:
            s_ref[direction] = s0_ref[direction].astype(F32)
        else:
            s_ref[direction] = jnp.zeros(s_ref.shape[1:], F32)

    def chunk(c):
        return pl.ds(pl.multiple_of(c * c_len, c_len), c_len)

    def scan_chunk(c, direction):
        rows = chunk(c)
        qc = q_ref[rows, :]
        kc = k_ref[rows, :]
        vc = v_ref[rows, :]
        s = s_ref[direction]
        scores = lax.dot_general(qc.astype(BF16), kc.astype(BF16), (((1,), (1,)), ((), ())),
                                 preferred_element_type=F32) * dmat_ref[direction]
        inner = _dot(scores.astype(BF16), vc)
        cross = _dot((qc * xi_ref[direction]).astype(BF16), s.astype(BF16))
        upd = lax.dot_general((kc * zeta_ref[direction]).astype(BF16), vc,
                              (((0,), (0,)), ((), ())), preferred_element_type=F32)
        s_ref[direction] = g_chunk[direction] * s + upd
        return inner + cross

    if n_chunks == 1:
        acc_ref[...] = scan_chunk(0, 0) + scan_chunk(0, 1)
    else:
        assert n_chunks % 2 == 0

        def first_visit(t, carry):
            acc_ref[chunk(t), :] = scan_chunk(t, 0)
            acc_ref[chunk(n_chunks - 1 - t), :] = scan_chunk(n_chunks - 1 - t, 1)
            return carry

        def second_visit(t, carry):
            cf, cb = chunk(t), chunk(n_chunks - 1 - t)
            acc_ref[cf, :] = acc_ref[cf, :] + scan_chunk(t, 0)
            acc_ref[cb, :] = acc_ref[cb, :] + scan_chunk(n_chunks - 1 - t, 1)
            return carry

        lax.fori_loop(0, n_chunks // 2, first_visit, 0)
        lax.fori_loop(n_chunks // 2, n_chunks, second_visit, 0)

    if want_state:
        for direction in range(2):
            sfin_ref[direction] = s_ref[direction].astype(sfin_ref.dtype)

    seq_len = q_ref.shape[0]
    norm_rows = min(NORM_ROWS, seq_len)

    def finish(r, carry):
        rows = pl.ds(pl.multiple_of(r * norm_rows, norm_rows), norm_rows)
        o = acc_ref[rows, :]
        mu = jnp.mean(o, axis=-1, keepdims=True)
        var = jnp.mean(jnp.square(o - mu), axis=-1, keepdims=True)
        o = (o - mu) * lax.rsqrt(var + EPS)
        o = o * gn_ref[...]
        o_ref[rows, :] = (o * sg_ref[rows, :]).astype(o_ref.dtype)
        return carry

    lax.fori_loop(0, seq_len // norm_rows, finish, 0)


def _retention(qk, v, sg, decay, gn, layer, seq_len, s0, want_state, state_dtype):
    nh, m, dv = v.shape
    assert nh == RET_HEADS and qk.shape[0] == 2 * nh
    dk = qk.shape[2]
    n_seq = m // seq_len
    has_s0 = s0 is not None
    c_len = RET_CHUNK

    in_specs = [
        pl.BlockSpec((None, 2, None, 1, 1), lambda b, h: (layer, 0, h, 0, 0)),
        pl.BlockSpec((None, seq_len, dk), lambda b, h: (h, b, 0)),
        pl.BlockSpec((None, seq_len, dk), lambda b, h: (nh + h, b, 0)),
        pl.BlockSpec((None, seq_len, dv), lambda b, h: (h, b, 0)),
        pl.BlockSpec((None, seq_len, dv), lambda b, h: (h, b, 0)),
        pl.BlockSpec((None, 1, dv), lambda b, h: (layer, 0, h)),
    ]
    args = [decay.reshape(decay.shape[0], 2, nh, 1, 1), qk, qk, v, sg,
            gn.reshape(gn.shape[0], 1, nh * dv)]
    state_spec = pl.BlockSpec((None, None, 2, None, dk, dv),
                              lambda b, h: (b, layer, 0, h, 0, 0))
    if has_s0:
        in_specs.append(state_spec)
        args.append(s0)
    out_specs = [pl.BlockSpec((None, seq_len, dv), lambda b, h: (h, b, 0))]
    out_shape = [jax.ShapeDtypeStruct((nh, m, dv), BF16)]
    if want_state:
        out_specs.append(state_spec)
        out_shape.append(jax.ShapeDtypeStruct((n_seq, decay.shape[0], 2, nh, dk, dv),
                                              state_dtype))
    outs = pl.pallas_call(
        functools.partial(_retention_kernel, n_chunks=seq_len // c_len, has_s0=has_s0,
                          want_state=want_state),
        grid=(n_seq, nh),
        in_specs=in_specs,
        out_specs=out_specs,
        out_shape=out_shape,
        scratch_shapes=[pltpu.VMEM((seq_len, dv), F32), pltpu.VMEM((2, dk, dv), F32),
                        pltpu.VMEM((2, c_len, c_len), F32), pltpu.VMEM((2, c_len, dk), F32),
                        pltpu.VMEM((2, c_len, dk), F32)],
        compiler_params=_params(2),
        name="retention",
    )(*args)
    return (outs[0], outs[1]) if want_state else (outs[0], None)


def _rope_tables(seq_len, dk):
    rows = seq_len // GRID_W
    pos_row = jnp.repeat(jnp.arange(rows, dtype=F32), GRID_W)
    pos_col = jnp.tile(jnp.arange(GRID_W, dtype=F32), rows)
    n = dk // 4
    inv = jnp.exp(-math.log(ROPE_BASE) * jnp.arange(n, dtype=F32) / n)
    ang_r = pos_row[:, None] * inv[None, :]
    ang_c = pos_col[:, None] * inv[None, :]
    cos = jnp.concatenate([jnp.cos(ang_r)] * 2 + [jnp.cos(ang_c)] * 2, axis=-1)
    sin = jnp.concatenate([-jnp.sin(ang_r), jnp.sin(ang_r), -jnp.sin(ang_c), jnp.sin(ang_c)],
                          axis=-1)
    return cos, sin


def _trunk(x, mod, seq_len, conv_block, tm, rope_tabs, s0, want_state, p):
    m, d = x.shape
    mod0, mod1 = mod[0], mod[1]
    nh = RET_HEADS
    dv = p['ret_gn'].shape[1] // nh
    dk = (p['ret_w_in'].shape[2] - 2 * nh * dv) // (2 * nh)

    h = _norm_mod(x, p['norm_mix'], 0, mod0, seq_len, 0, 1)
    u = _hy_in(h, p['hy_w_in'], p['hy_b_in'], p['hy_conv_w'], p['hy_conv_b'], 0, seq_len,
               tm=max(tm, seq_len))
    fwd_lo, fwd_hi, inv = _odd_dft_mats(conv_block)
    spectra = _hyena_filter_spectra(
        seq_len, conv_block, p['hy_filt_w1'][0], p['hy_filt_b1'][0], p['hy_filt_freq'][0],
        p['hy_filt_w2'][0], p['hy_filt_b2'][0], p['hy_filt_w3'][0], fwd_lo, fwd_hi)
    slabs = d // u.shape[2]
    seqs = max(1, 1024 // seq_len)
    z = _long_conv(u, 0, u, slabs, spectra, 0, p['hy_bias_d'], 0, fwd_lo, inv, seq_len,
                   conv_block, F32, seqs)
    z = _long_conv(z, 0, u, 2 * slabs, spectra, 1, p['hy_bias_d'], 0, fwd_lo, inv, seq_len,
                   conv_block, BF16, seqs)
    x = _resid_proj(z, p['hy_w_out'], p['hy_b_out'], 0, x, mod0, seq_len, 2, tm, 1024)
    h = _norm_mod(x, p['norm_ffn'], 0, mod0, seq_len, 3, 4)
    a = _swiglu_up(h, p['ffn_w1'], p['ffn_w3'], 0, tm=tm)
    x = _resid_proj(a, p['ffn_w2'], None, 0, x, mod0, seq_len, 5, tm // 2, 512)

    h = _norm_mod(x, p['norm_mix'], 1, mod1, seq_len, 0, 1)
    qk = _ret_qk_proj(h, p['ret_w_in'], 0, nh, dk, seq_len, rope_tabs, tm=tm)
    v = _proj(h, p['ret_w_in'], 0, 2 * nh * dk, nh * dv, dv, BF16, tm=tm)
    sg = _proj(h, p['ret_w_in'], 0, 2 * nh * dk + nh * dv, nh * dv, dv, F32, silu=True, tm=tm)
    og, s_fin = _retention(qk, v, sg, p['ret_decay'], p['ret_gn'], 0, seq_len, s0,
                           want_state, x.dtype)
    x = _resid_proj(og, p['ret_w_out'], None, 0, x, mod1, seq_len, 2, tm, 512)
    h = _norm_mod(x, p['norm_ffn'], 1, mod1, seq_len, 3, 4)
    a = _swiglu_up(h, p['ffn_w1'], p['ffn_w3'], 1, tm=tm)
    x = _resid_proj(a, p['ffn_w2'], None, 1, x, mod1, seq_len, 5, tm // 2, 512)

    return _final_norm(x, p['norm_final']), s_fin


def kernel(x_prompt, x_sample, state_ret, c, c_ctx, w_ada, b_ada, norm_mix, norm_ffn, norm_final,
           ffn_w1, ffn_w3, ffn_w2, hy_w_in, hy_b_in, hy_conv_w, hy_conv_b, hy_filt_w1, hy_filt_b1,
           hy_filt_freq, hy_filt_w2, hy_filt_b2, hy_filt_w3, hy_bias_d, hy_w_out, hy_b_out,
           ret_w_in, ret_decay, ret_gn, ret_w_out):
    p = dict(norm_mix=norm_mix, norm_ffn=norm_ffn, norm_final=norm_final,
             ffn_w1=ffn_w1, ffn_w3=ffn_w3, ffn_w2=ffn_w2, hy_w_in=hy_w_in, hy_b_in=hy_b_in,
             hy_conv_w=hy_conv_w, hy_conv_b=hy_conv_b, hy_filt_w1=hy_filt_w1,
             hy_filt_b1=hy_filt_b1, hy_filt_freq=hy_filt_freq, hy_filt_w2=hy_filt_w2,
             hy_filt_b2=hy_filt_b2, hy_filt_w3=hy_filt_w3, hy_bias_d=hy_bias_d,
             hy_w_out=hy_w_out, hy_b_out=hy_b_out, ret_w_in=ret_w_in, ret_decay=ret_decay,
             ret_gn=ret_gn, ret_w_out=ret_w_out)
    n_ctx, ctx_len, d = x_prompt.shape
    n_dec, dec_len, _ = x_sample.shape
    depth = w_ada.shape[0]

    cond_rows = 16
    cond = jnp.concatenate(
        [c_ctx[None, :], c, jnp.zeros((cond_rows - 1 - n_dec, d), c.dtype)], axis=0)
    mod = _ada_mod(cond, w_ada, b_ada).reshape(depth, cond_rows, 6, d)
    mod_ctx = mod[:, 0:1]
    mod_dec = mod[:, 1:1 + n_dec]

    y_prompt, ctx_state = _trunk(
        x_prompt.reshape(n_ctx * ctx_len, d), mod_ctx, ctx_len, conv_block=ctx_len, tm=1024,
        rope_tabs=None, s0=None, want_state=True, p=p)
    dk = state_ret.shape[-2]
    y_sample, _ = _trunk(
        x_sample.reshape(n_dec * dec_len, d), mod_dec, dec_len, conv_block=512, tm=1024,
        rope_tabs=_rope_tables(dec_len, dk), s0=state_ret, want_state=False, p=p)

    return (y_prompt.reshape(x_prompt.shape), y_sample.reshape(x_sample.shape), ctx_state)
```

```python
import functools
import math

import numpy as np
import jax
import jax.numpy as jnp
from jax import lax
from jax.experimental import pallas as pl
from jax.experimental.pallas import tpu as pltpu

F32 = jnp.float32
BF16 = jnp.bfloat16

EPS = 1e-6
GRID_W = 64
HY_BANDS = 16
HY_DECAY_PCT_SHORT = 0.3
HY_DECAY_PCT_LONG = 1.5
HY_TARGET = 1e-2
RET_HEADS = 8
ROPE_BASE = 10000.0
RET_CHUNK = 256
MAC_ROWS = 8

V7X_VMEM_BYTES = 64 * 1024 * 1024
VMEM_LIMIT = V7X_VMEM_BYTES - 8 * 1024 * 1024
LANES = 128
HIGHEST = lax.Precision.HIGHEST


def _params(n_axes):
    return pltpu.CompilerParams(
        dimension_semantics=("arbitrary",) * n_axes, vmem_limit_bytes=VMEM_LIMIT)


def _dot(a, b):
    return jnp.dot(a, b, preferred_element_type=F32)


def _ada_kernel(c_ref, w_ref, b_ref, o_ref):
    s = jax.nn.silu(c_ref[...]).astype(BF16)
    o_ref[...] = _dot(s, w_ref[...].astype(BF16)) + b_ref[...]


def _ada_mod(cond, w_ada, b_ada, tn=1536):
    depth, d, n = w_ada.shape
    rows = cond.shape[0]
    return pl.pallas_call(
        _ada_kernel,
        grid=(depth, n // tn),
        in_specs=[
            pl.BlockSpec((rows, d), lambda l, j: (0, 0)),
            pl.BlockSpec((None, d, tn), lambda l, j: (l, 0, j)),
            pl.BlockSpec((None, 1, tn), lambda l, j: (l, 0, j)),
        ],
        out_specs=pl.BlockSpec((None, rows, tn), lambda l, j: (l, 0, j)),
        out_shape=jax.ShapeDtypeStruct((depth, rows, n), F32),
        compiler_params=_params(2),
        name="ada_mod",
    )(cond, w_ada, b_ada.reshape(depth, 1, n))


def _seq_of_tile(tm, seq_len, n_mod):
    if n_mod == 1:
        return lambda i: 0
    assert seq_len % tm == 0
    return lambda i: (i * tm) // seq_len


def _norm_mod_kernel(x_ref, g_ref, mod_ref, o_ref, *, shift_idx, scale_idx):
    x = x_ref[...]
    y = x * lax.rsqrt(jnp.mean(x * x, axis=-1, keepdims=True) + EPS)
    y = y * g_ref[...]
    scale = mod_ref[scale_idx:scale_idx + 1, :]
    shift = mod_ref[shift_idx:shift_idx + 1, :]
    o_ref[...] = (y * (1.0 + scale) + shift).astype(o_ref.dtype)


def _norm_mod(x, g, layer, mod, seq_len, shift_idx, scale_idx, tm=512):
    m, d = x.shape
    seq = _seq_of_tile(tm, seq_len, mod.shape[0])
    return pl.pallas_call(
        functools.partial(_norm_mod_kernel, shift_idx=shift_idx, scale_idx=scale_idx),
        grid=(m // tm,),
        in_specs=[
            pl.BlockSpec((tm, d), lambda i: (i, 0)),
            pl.BlockSpec((None, 1, d), lambda i: (layer, 0, 0)),
            pl.BlockSpec((None, 6, d), lambda i: (seq(i), 0, 0)),
        ],
        out_specs=pl.BlockSpec((tm, d), lambda i: (i, 0)),
        out_shape=jax.ShapeDtypeStruct((m, d), BF16),
        compiler_params=_params(1),
        name="norm_mod",
    )(x, g.reshape(g.shape[0], 1, d), mod)


def _cast_weights_once(pairs):
    @pl.when(pl.program_id(1) == 0)
    def _():
        for w_ref, wb_ref in pairs:
            wb_ref[...] = w_ref[...].astype(BF16)


def _w_spec(k, tn, layer, col0=0):
    assert col0 % tn == 0
    return pl.BlockSpec((None, k, tn), lambda j, i: (layer, 0, col0 // tn + j))


def _vec_spec(tn, layer, col0=0):
    return pl.BlockSpec((None, 1, tn), lambda j, i: (layer, 0, col0 // tn + j))


def _rotate_pairs(x, half):
    parts = [pltpu.roll(x[:, g * half:(g + 1) * half], half // 2, 1)
             for g in range(x.shape[1] // half)]
    return jnp.concatenate(parts, axis=1)


def _ret_qk_kernel(*refs, dk, k_tile0, k_scale, use_rope):
    if use_rope:
        a_ref, w_ref, cos_ref, sin_ref, o_ref, wb_ref = refs
    else:
        a_ref, w_ref, o_ref, wb_ref = refs
    _cast_weights_once([(w_ref, wb_ref)])
    y = _dot(a_ref[...], wb_ref[...])
    y = y * jnp.where(pl.program_id(0) >= k_tile0, k_scale, 1.0).astype(F32)
    for hb in range(y.shape[1] // dk):
        x = y[:, hb * dk:(hb + 1) * dk]
        if use_rope:
            x = x * cos_ref[...] + _rotate_pairs(x, dk // 2) * sin_ref[...]
        o_ref[hb] = x


def _ret_qk_proj(a, w, layer, nh, dk, seq_len, rope_tabs, tm=1024, tn=1024):
    m, k = a.shape
    n = 2 * nh * dk
    use_rope = rope_tabs is not None
    in_specs = [pl.BlockSpec((tm, k), lambda j, i: (i, 0)), _w_spec(k, tn, layer)]
    args = [a, w]
    if use_rope:
        assert seq_len % tm == 0
        tiles_per_seq = seq_len // tm
        tab_spec = pl.BlockSpec((tm, dk), lambda j, i: (i % tiles_per_seq, 0))
        in_specs += [tab_spec, tab_spec]
        args += list(rope_tabs)
    return pl.pallas_call(
        functools.partial(_ret_qk_kernel, dk=dk, k_tile0=(nh * dk) // tn, k_scale=dk ** -0.5,
                          use_rope=use_rope),
        grid=(n // tn, m // tm),
        in_specs=in_specs,
        out_specs=pl.BlockSpec((tn // dk, tm, dk), lambda j, i: (j, i, 0)),
        out_shape=jax.ShapeDtypeStruct((n // dk, m, dk), F32),
        scratch_shapes=[pltpu.VMEM((k, tn), BF16)],
        compiler_params=_params(2),
        name="ret_qk_proj",
    )(*args)


def _proj_kernel(a_ref, w_ref, o_ref, wb_ref, *, silu):
    _cast_weights_once([(w_ref, wb_ref)])
    y = _dot(a_ref[...], wb_ref[...])
    if silu:
        y = jax.nn.silu(y)
    oc = o_ref.shape[2]
    for c in range(o_ref.shape[0]):
        o_ref[c] = y[:, c * oc:(c + 1) * oc].astype(o_ref.dtype)


def _proj(a, w, layer, col0, n, oc, out_dtype, silu=False, tm=1024, tn=1024):
    m, k = a.shape
    return pl.pallas_call(
        functools.partial(_proj_kernel, silu=silu),
        grid=(n // tn, m // tm),
        in_specs=[pl.BlockSpec((tm, k), lambda j, i: (i, 0)), _w_spec(k, tn, layer, col0)],
        out_specs=pl.BlockSpec((tn // oc, tm, oc), lambda j, i: (j, i, 0)),
        out_shape=jax.ShapeDtypeStruct((n // oc, m, oc), out_dtype),
        scratch_shapes=[pltpu.VMEM((k, tn), BF16)],
        compiler_params=_params(2),
        name="proj",
    )(a, w)


def _hy_in_kernel(a_ref, w_ref, b_ref, cw_ref, cb_ref, o_ref, wb_ref, *, seq_len, row_blocks):
    _cast_weights_once([(w_ref, wb_ref)])
    n_slab, tm, ct = o_ref.shape
    rq = tm // row_blocks
    wb = wb_ref[...]
    bias = b_ref[...]
    us = [_dot(a_ref[q * rq:(q + 1) * rq, :], wb) + bias for q in range(row_blocks)]
    w0, w1, w2, cb = cw_ref[0:1, :], cw_ref[1:2, :], cw_ref[2:3, :], cb_ref[...]

    def tap_sum(prev, cur, nxt):
        acc = cur * w1
        if prev is not None:
            acc = prev * w0 + acc
        if nxt is not None:
            acc = acc + nxt * w2
        return acc + cb

    def store(rows, val):
        for s in range(n_slab):
            o_ref[s, rows, :] = val[:, s * ct:(s + 1) * ct]

    for q, u in enumerate(us):
        r0 = q * rq
        store(slice(r0, r0 + rq),
              tap_sum(pltpu.roll(u, 1, 0), u, pltpu.roll(u, rq - 1, 0)))
        prev = None if r0 % seq_len == 0 else us[q - 1][rq - 1:rq]
        store(slice(r0, r0 + 1), tap_sum(prev, u[0:1], u[1:2]))
        nxt = None if (r0 + rq) % seq_len == 0 else us[q + 1][0:1]
        store(slice(r0 + rq - 1, r0 + rq), tap_sum(u[rq - 2:rq - 1], u[rq - 1:rq], nxt))
        for r in range(seq_len, rq, seq_len):
            store(slice(r0 + r, r0 + r + 1), tap_sum(None, u[r:r + 1], u[r + 1:r + 2]))
            store(slice(r0 + r - 1, r0 + r), tap_sum(u[r - 2:r - 1], u[r - 1:r], None))


def _hy_in(a, w, b, cw, cb, layer, seq_len, tm, tn=512, ct=256, row_blocks=4):
    m, k = a.shape
    n = w.shape[2]
    assert tm % seq_len == 0
    taps = cw.shape[1]
    return pl.pallas_call(
        functools.partial(_hy_in_kernel, seq_len=seq_len, row_blocks=row_blocks),
        grid=(n // tn, m // tm),
        in_specs=[
            pl.BlockSpec((tm, k), lambda j, i: (i, 0)),
            _w_spec(k, tn, layer),
            _vec_spec(tn, layer),
            pl.BlockSpec((None, taps, tn), lambda j, i: (layer, 0, j)),
            _vec_spec(tn, layer),
        ],
        out_specs=pl.BlockSpec((tn // ct, tm, ct), lambda j, i: (j, i, 0)),
        out_shape=jax.ShapeDtypeStruct((n // ct, m, ct), F32),
        scratch_shapes=[pltpu.VMEM((k, tn), BF16)],
        compiler_params=_params(2),
        name="hy_in",
    )(a, w, b.reshape(b.shape[0], 1, n), cw, cb.reshape(cb.shape[0], 1, n))


def _swiglu_kernel(a_ref, w1_ref, w3_ref, o_ref, w1b_ref, w3b_ref):
    _cast_weights_once([(w1_ref, w1b_ref), (w3_ref, w3b_ref)])
    a = a_ref[...]
    o_ref[...] = (jax.nn.silu(_dot(a, w1b_ref[...])) * _dot(a, w3b_ref[...])).astype(BF16)


def _swiglu_up(a, w1, w3, layer, tm=1024, tn=512):
    m, k = a.shape
    n = w1.shape[2]
    return pl.pallas_call(
        _swiglu_kernel,
        grid=(n // tn, m // tm),
        in_specs=[pl.BlockSpec((tm, k), lambda j, i: (i, 0)),
                  _w_spec(k, tn, layer), _w_spec(k, tn, layer)],
        out_specs=pl.BlockSpec((tm, tn), lambda j, i: (i, j)),
        out_shape=jax.ShapeDtypeStruct((m, n), BF16),
        scratch_shapes=[pltpu.VMEM((k, tn), BF16), pltpu.VMEM((k, tn), BF16)],
        compiler_params=_params(2),
        name="swiglu_up",
    )(a, w1, w3)


def _resid_norm_kernel(*refs, gate_idx, has_bias, n_k, final, shift_idx, scale_idx,
                       row_blocks):
    it = iter(refs)
    a_ref, w_ref = next(it), next(it)
    b_ref = next(it) if has_bias else None
    x_ref, mod_ref = next(it), next(it)
    nmod_ref = None if final else next(it)
    gain_ref = next(it)
    xo_ref = None if final else next(it)
    h_ref = next(it)
    acc_ref = next(it) if n_k > 1 else None
    k = pl.program_id(1)
    tm = x_ref.shape[0]
    rb = tm // row_blocks

    def finish(rows, y):
        if has_bias:
            y = y + b_ref[...]
        xn = x_ref[rows, :] + mod_ref[gate_idx:gate_idx + 1, :] * y
        if not final:
            xo_ref[rows, :] = xn
        hn = xn * lax.rsqrt(jnp.mean(xn * xn, axis=-1, keepdims=True) + EPS)
        hn = hn * gain_ref[...]
        if not final:
            hn = hn * (1.0 + nmod_ref[scale_idx:scale_idx + 1, :]) \
                + nmod_ref[shift_idx:shift_idx + 1, :]
        h_ref[rows, :] = hn.astype(h_ref.dtype)

    def partial_products():
        w = w_ref[...]
        for r in range(row_blocks):
            rows = slice(r * rb, (r + 1) * rb)
            if len(a_ref.shape) == 3:
                a = jnp.concatenate([a_ref[c, rows, :] for c in range(a_ref.shape[0])],
                                    axis=1)
            else:
                a = a_ref[rows, :]
            yield rows, _dot(a, w)

    if n_k == 1:
        for rows, part in partial_products():
            finish(rows, part)
    else:
        @pl.when(k == 0)
        def _():
            acc_ref[...] = jnp.zeros(acc_ref.shape, F32)

        @pl.when(k < n_k - 1)
        def _():
            for rows, part in partial_products():
                acc_ref[rows, :] = acc_ref[rows, :] + part

        @pl.when(k == n_k - 1)
        def _():
            for rows, part in partial_products():
                finish(rows, acc_ref[rows, :] + part)


def _resid_norm_proj(a, w, bias, layer, x, mod, gate_idx, seq_len, gain, gain_layer,
                     next_mod, shift_idx, scale_idx, tm, tk, row_blocks=2):
    d = w.shape[2]
    final = next_mod is None
    seq = _seq_of_tile(tm, seq_len, mod.shape[0])
    if a.ndim == 3:
        n_slab, m, cw = a.shape
        kk = n_slab * cw
        assert tk % cw == 0
        a_spec = pl.BlockSpec((tk // cw, tm, cw), lambda i, k: (k, i, 0))
    else:
        m, kk = a.shape
        a_spec = pl.BlockSpec((tm, tk), lambda i, k: (i, k))
    n_k = kk // tk
    in_specs = [a_spec, pl.BlockSpec((None, tk, d), lambda i, k: (layer, k, 0))]
    args = [a, w]
    if bias is not None:
        in_specs.append(pl.BlockSpec((None, 1, d), lambda i, k: (layer, 0, 0)))
        args.append(bias.reshape(bias.shape[0], 1, d))
    mod_spec = pl.BlockSpec((None, 6, d), lambda i, k: (seq(i), 0, 0))
    in_specs += [pl.BlockSpec((tm, d), lambda i, k: (i, 0)), mod_spec]
    args += [x, mod]
    if not final:
        in_specs.append(mod_spec)
        args.append(next_mod)
    if gain.ndim == 1:
        gain = gain[None]
    in_specs.append(pl.BlockSpec((None, 1, d), lambda i, k: (gain_layer, 0, 0)))
    args.append(gain.reshape(gain.shape[0], 1, d))
    row_spec = pl.BlockSpec((tm, d), lambda i, k: (i, 0))
    if final:
        out_specs = [row_spec]
        out_shape = [jax.ShapeDtypeStruct((m, d), F32)]
    else:
        out_specs = [row_spec, row_spec]
        out_shape = [jax.ShapeDtypeStruct((m, d), F32), jax.ShapeDtypeStruct((m, d), BF16)]
    outs = pl.pallas_call(
        functools.partial(_resid_norm_kernel, gate_idx=gate_idx, has_bias=bias is not None,
                          n_k=n_k, final=final, shift_idx=shift_idx, scale_idx=scale_idx,
                          row_blocks=row_blocks),
        grid=(m // tm, n_k),
        in_specs=in_specs,
        out_specs=out_specs,
        out_shape=out_shape,
        scratch_shapes=[pltpu.VMEM((tm, d), F32)] if n_k > 1 else [],
        compiler_params=_params(2),
        name="resid_norm_proj",
    )(*args)
    return outs[0] if final else (outs[0], outs[1])


def _odd_dft_mats(p):
    f = np.arange(p, dtype=np.int64)
    m = np.arange(2 * p, dtype=np.int64)
    phase = ((2 * f[:, None] + 1) * m[None, :]) % (4 * p)
    ang = np.pi * phase.astype(np.float64) / (2 * p)
    fwd = np.concatenate([np.cos(ang), -np.sin(ang)], axis=0)
    fwd_lo = fwd[:, :p]
    fwd_hi = fwd[:, p:].copy()
    fwd_hi[:, 0] = 0.0
    t = np.arange(p, dtype=np.int64)
    phase_i = (t[:, None] * (2 * f[None, :] + 1)) % (4 * p)
    ang_i = np.pi * phase_i.astype(np.float64) / (2 * p)
    inv = np.concatenate([np.cos(ang_i), -np.sin(ang_i)], axis=1) / p
    as_bf16 = lambda a: jnp.asarray(a, F32).astype(BF16)
    return as_bf16(fwd_lo), as_bf16(fwd_hi), as_bf16(inv)


def _filter_mlp_kernel(w1_ref, b1_ref, fr_ref, w2_ref, b2_ref, o_ref, *, seq_len):
    rows = 2 * seq_len
    r = lax.broadcasted_iota(jnp.int32, (rows, LANES), 0)
    lane = lax.broadcasted_iota(jnp.int32, (rows, LANES), 1)
    t = jnp.abs(r - seq_len).astype(F32) / seq_len
    band = jnp.where(lane <= HY_BANDS, lane, lane - HY_BANDS).astype(F32)
    ang = 2.0 * math.pi * t * band
    feat = jnp.where(lane == 0, t,
                     jnp.where(lane <= HY_BANDS, jnp.cos(ang),
                               jnp.where(lane <= 2 * HY_BANDS, jnp.sin(ang), 0.0)))
    z = jnp.dot(feat, w1_ref[...], precision=HIGHEST, preferred_element_type=F32)
    z = jnp.sin(fr_ref[0:1, :] * (z + b1_ref[...]))
    z = jnp.dot(z, w2_ref[...], precision=HIGHEST, preferred_element_type=F32)
    o_ref[...] = jnp.sin(fr_ref[1:2, :] * (z + b2_ref[...]))


def _filter_time_kernel(z_ref, wf_ref, wb_ref, dl_ref, o_ref, *, seq_len):
    z = z_ref[...].astype(BF16)
    ff = _dot(z, wf_ref[...].astype(BF16))
    fb = _dot(z, wb_ref[...].astype(BF16))
    n = lax.broadcasted_iota(jnp.int32, ff.shape, 0) - seq_len
    t = jnp.abs(n).astype(F32) / seq_len
    window = jnp.exp(-t * dl_ref[...])
    kk = jnp.where(n >= 0, ff, fb) * window
    o_ref[...] = jnp.where(n == -seq_len, 0.0, kk)


def _filter_spec_kernel(ka_ref, kb_ref, flo_ref, fhi_ref, o_ref):
    o_ref[...] = (_dot(flo_ref[...], ka_ref[...].astype(BF16))
                  - _dot(fhi_ref[...], kb_ref[...].astype(BF16)))


def _hyena_filter_spectra(seq_len, p, w1, b1, freq, w2, b2, w3, fwd_lo, fwd_hi, ct=512):
    hidden = w1.shape[1]
    n_order = 2
    d = w3.shape[1] // (2 * n_order)
    nb = seq_len // p
    nd = 2 * nb - 1
    rows = 2 * seq_len
    w1p = jnp.pad(w1, ((0, LANES - w1.shape[0]), (0, 0)))
    full = lambda shape: pl.BlockSpec(shape, lambda *_: (0,) * len(shape))
    z = pl.pallas_call(
        functools.partial(_filter_mlp_kernel, seq_len=seq_len),
        grid=(1,),
        in_specs=[full((LANES, hidden)), full((1, hidden)), full((2, hidden)),
                  full((hidden, hidden)), full((1, hidden))],
        out_specs=full((rows, hidden)),
        out_shape=jax.ShapeDtypeStruct((rows, hidden), F32),
        compiler_params=_params(1),
        name="filter_mlp",
    )(w1p, b1.reshape(1, hidden), freq, w2, b2.reshape(1, hidden))

    min_decay = abs(math.log(HY_TARGET) / HY_DECAY_PCT_LONG)
    max_decay = abs(math.log(HY_TARGET) / HY_DECAY_PCT_SHORT)
    deltas = jnp.linspace(min_decay, max_decay, d, dtype=F32).reshape(1, d)
    nct = d // ct
    kk = pl.pallas_call(
        functools.partial(_filter_time_kernel, seq_len=seq_len),
        grid=(n_order, nct),
        in_specs=[
            pl.BlockSpec((rows, hidden), lambda o, j: (0, 0)),
            pl.BlockSpec((hidden, ct), lambda o, j: (0, (2 * o) * nct + j)),
            pl.BlockSpec((hidden, ct), lambda o, j: (0, (2 * o + 1) * nct + j)),
            pl.BlockSpec((1, ct), lambda o, j: (0, j)),
        ],
        out_specs=pl.BlockSpec((None, rows, ct), lambda o, j: (o, 0, j)),
        out_shape=jax.ShapeDtypeStruct((n_order, rows, d), F32),
        compiler_params=_params(2),
        name="filter_time",
    )(z, w3, w3, deltas)

    return pl.pallas_call(
        _filter_spec_kernel,
        grid=(n_order, d // ct, nd),
        in_specs=[
            pl.BlockSpec((None, p, ct), lambda o, j, e: (o, e + 1, j)),
            pl.BlockSpec((None, p, ct), lambda o, j, e: (o, e, j)),
            pl.BlockSpec((2 * p, p), lambda o, j, e: (0, 0)),
            pl.BlockSpec((2 * p, p), lambda o, j, e: (0, 0)),
        ],
        out_specs=pl.BlockSpec((None, None, 2 * p, ct), lambda o, j, e: (o, e, 0, j)),
        out_shape=jax.ShapeDtypeStruct((n_order, nd, 2 * p, d), F32),
        compiler_params=_params(3),
        name="filter_spec",
    )(kk, kk, fwd_lo, fwd_hi)


def _long_conv_kernel(z_ref, gate_ref, k_ref, bias_ref, fwd_ref, inv_ref, o_ref,
                      zs_ref, ys_ref, *, p, nb, n_seq):
    fwd = fwd_ref[...]
    inv = inv_ref[...]
    seq_len = p * nb
    for s in range(n_seq):
        base = s * seq_len
        for b in range(nb):
            zs_ref[b] = _dot(fwd, z_ref[base + b * p:base + (b + 1) * p, :].astype(BF16))
        for r in range(0, p, MAC_ROWS):
            re = slice(r, r + MAC_ROWS)
            im = slice(p + r, p + r + MAC_ROWS)
            kr = [k_ref[e, re, :] for e in range(2 * nb - 1)]
            ki = [k_ref[e, im, :] for e in range(2 * nb - 1)]
            yr = [None] * nb
            yi = [None] * nb
            for b in range(nb):
                zr = zs_ref[b, re, :]
                zi = zs_ref[b, im, :]
                for a in range(nb):
                    e = a - b + nb - 1
                    tr = kr[e] * zr - ki[e] * zi
                    ti = kr[e] * zi + ki[e] * zr
                    yr[a] = tr if yr[a] is None else yr[a] + tr
                    yi[a] = ti if yi[a] is None else yi[a] + ti
            for a in range(nb):
                ys_ref[a, re, :] = yr[a]
                ys_ref[a, im, :] = yi[a]
        for a in range(nb):
            y = _dot(inv, ys_ref[a].astype(BF16))
            rows = slice(base + a * p, base + (a + 1) * p)
            za = z_ref[rows, :].astype(F32)
            o_ref[rows, :] = (gate_ref[rows, :] * (y + bias_ref[...] * za)).astype(o_ref.dtype)


def _long_conv(z, z_slab, gate, gate_slab, spectra, order, bias, layer, fwd_lo, inv, seq_len, p,
               out_dtype, seqs_per_step=1):
    _, m, ct = z.shape
    d = spectra.shape[-1]
    nb = seq_len // p
    nd = 2 * nb - 1
    rows = seqs_per_step * seq_len
    return pl.pallas_call(
        functools.partial(_long_conv_kernel, p=p, nb=nb, n_seq=seqs_per_step),
        grid=(d // ct, m // rows),
        in_specs=[
            pl.BlockSpec((None, rows, ct), lambda j, b: (z_slab + j, b, 0)),
            pl.BlockSpec((None, rows, ct), lambda j, b: (gate_slab + j, b, 0)),
            pl.BlockSpec((None, nd, 2 * p, ct), lambda j, b: (order, 0, 0, j)),
            pl.BlockSpec((None, None, 1, ct), lambda j, b: (layer, order, 0, j)),
            pl.BlockSpec((2 * p, p), lambda j, b: (0, 0)),
            pl.BlockSpec((p, 2 * p), lambda j, b: (0, 0)),
        ],
        out_specs=pl.BlockSpec((None, rows, ct), lambda j, b: (j, b, 0)),
        out_shape=jax.ShapeDtypeStruct((d // ct, m, ct), out_dtype),
        scratch_shapes=[pltpu.VMEM((nb, 2 * p, ct), F32), pltpu.VMEM((nb, 2 * p, ct), F32)],
        compiler_params=_params(2),
        name="long_conv",
    )(z, gate, spectra, bias.reshape(bias.shape[0], bias.shape[1], 1, d), fwd_lo, inv)


def _retention_kernel(*refs, n_chunks, has_s0, want_state):
    it = iter(refs)
    dec_ref, q_ref, k_ref, v_ref, sg_ref, gn_ref = (next(it) for _ in range(6))
    s0_ref = next(it) if has_s0 else None
    o_ref = next(it)
    sfin_ref = next(it) if want_state else None
    acc_ref, s_ref, dmat_ref, xi_ref, zeta_ref = (next(it) for _ in range(5))

    c_len = RET_CHUNK
    dk = q_ref.shape[1]

    log_g = -jnp.exp(dec_ref[...])
    ii = lax.broadcasted_iota(jnp.int32, (c_len, c_len), 0)
    jj = lax.broadcasted_iota(jnp.int32, (c_len, c_len), 1)
    idx = lax.broadcasted_iota(jnp.int32, (c_len, 1), 0).astype(F32)
    g_chunk = []
    for direction in range(2):
        lg = log_g[direction]
        diff = (ii - jj) if direction == 0 else (jj - ii)
        dmat_ref[direction] = jnp.where(
            diff >= 0, jnp.exp(lg * jnp.maximum(diff, 0).astype(F32)), 0.0)
        if direction == 0:
            xi = jnp.exp(lg * (idx + 1.0))
            zeta = jnp.exp(lg * (c_len - 1.0 - idx))
        else:
            xi = jnp.exp(lg * (c_len - idx))
            zeta = jnp.exp(lg * idx)
        xi_ref[direction] = jnp.broadcast_to(xi, (c_len, dk))
        zeta_ref[direction] = jnp.broadcast_to(zeta, (c_len, dk))
        g_chunk.append(jnp.exp(lg * c_len))
        if has_s0:
            s_ref[direction] = s0_ref[direction].astype(F32)
        else:
            s_ref[direction] = jnp.zeros(s_ref.shape[1:], F32)

    def chunk(c):
        return pl.ds(pl.multiple_of(c * c_len, c_len), c_len)

    def scan_chunk(c, direction):
        rows = chunk(c)
        qc = q_ref[rows, :]
        kc = k_ref[rows, :]
        vc = v_ref[rows, :]
        s = s_ref[direction]
        scores = lax.dot_general(qc.astype(BF16), kc.astype(BF16), (((1,), (1,)), ((), ())),
                                 preferred_element_type=F32) * dmat_ref[direction]
        inner = _dot(scores.astype(BF16), vc)
        cross = _dot((qc * xi_ref[direction]).astype(BF16), s.astype(BF16))
        upd = lax.dot_general((kc * zeta_ref[direction]).astype(BF16), vc,
                              (((0,), (0,)), ((), ())), preferred_element_type=F32)
        s_ref[direction] = g_chunk[direction] * s + upd
        return inner + cross

    def norm_gate_store(rows, o):
        mu = jnp.mean(o, axis=-1, keepdims=True)
        var = jnp.mean(jnp.square(o - mu), axis=-1, keepdims=True)
        o = (o - mu) * lax.rsqrt(var + EPS)
        o = o * gn_ref[...]
        o_ref[rows, :] = (o * sg_ref[rows, :]).astype(o_ref.dtype)

    if n_chunks == 1:
        norm_gate_store(chunk(0), scan_chunk(0, 0) + scan_chunk(0, 1))
    else:
        assert n_chunks % 2 == 0

        def first_visit(t, carry):
            acc_ref[chunk(t), :] = scan_chunk(t, 0)
            acc_ref[chunk(n_chunks - 1 - t), :] = scan_chunk(n_chunks - 1 - t, 1)
            return carry

        def second_visit(t, carry):
            cf, cb = chunk(t), chunk(n_chunks - 1 - t)
            norm_gate_store(cf, acc_ref[cf, :] + scan_chunk(t, 0))
            norm_gate_store(cb, acc_ref[cb, :] + scan_chunk(n_chunks - 1 - t, 1))
            return carry

        lax.fori_loop(0, n_chunks // 2, first_visit, 0)
        lax.fori_loop(n_chunks // 2, n_chunks, second_visit, 0)

    if want_state:
        for direction in range(2):
            sfin_ref[direction] = s_ref[direction].astype(sfin_ref.dtype)


def _retention(qk, v, sg, decay, gn, layer, seq_len, s0, want_state, state_dtype):
    nh, m, dv = v.shape
    assert nh == RET_HEADS and qk.shape[0] == 2 * nh
    dk = qk.shape[2]
    n_seq = m // seq_len
    has_s0 = s0 is not None
    c_len = RET_CHUNK

    in_specs = [
        pl.BlockSpec((None, 2, None, 1, 1), lambda b, h: (layer, 0, h, 0, 0)),
        pl.BlockSpec((None, seq_len, dk), lambda b, h: (h, b, 0)),
        pl.BlockSpec((None, seq_len, dk), lambda b, h: (nh + h, b, 0)),
        pl.BlockSpec((None, seq_len, dv), lambda b, h: (h, b, 0)),
        pl.BlockSpec((None, seq_len, dv), lambda b, h: (h, b, 0)),
        pl.BlockSpec((None, 1, dv), lambda b, h: (layer, 0, h)),
    ]
    args = [decay.reshape(decay.shape[0], 2, nh, 1, 1), qk, qk, v, sg,
            gn.reshape(gn.shape[0], 1, nh * dv)]
    state_spec = pl.BlockSpec((None, None, 2, None, dk, dv),
                              lambda b, h: (b, layer, 0, h, 0, 0))
    if has_s0:
        in_specs.append(state_spec)
        args.append(s0)
    out_specs = [pl.BlockSpec((None, seq_len, dv), lambda b, h: (h, b, 0))]
    out_shape = [jax.ShapeDtypeStruct((nh, m, dv), BF16)]
    if want_state:
        out_specs.append(state_spec)
        out_shape.append(jax.ShapeDtypeStruct((n_seq, decay.shape[0], 2, nh, dk, dv),
                                              state_dtype))
    outs = pl.pallas_call(
        functools.partial(_retention_kernel, n_chunks=seq_len // c_len, has_s0=has_s0,
                          want_state=want_state),
        grid=(n_seq, nh),
        in_specs=in_specs,
        out_specs=out_specs,
        out_shape=out_shape,
        scratch_shapes=[pltpu.VMEM((seq_len, dv), F32), pltpu.VMEM((2, dk, dv), F32),
                        pltpu.VMEM((2, c_len, c_len), F32), pltpu.VMEM((2, c_len, dk), F32),
                        pltpu.VMEM((2, c_len, dk), F32)],
        compiler_params=_params(2),
        name="retention",
    )(*args)
    return (outs[0], outs[1]) if want_state else (outs[0], None)


def _rope_tables(seq_len, dk):
    rows = seq_len // GRID_W
    pos_row = jnp.repeat(jnp.arange(rows, dtype=F32), GRID_W)
    pos_col = jnp.tile(jnp.arange(GRID_W, dtype=F32), rows)
    n = dk // 4
    inv = jnp.exp(-math.log(ROPE_BASE) * jnp.arange(n, dtype=F32) / n)
    ang_r = pos_row[:, None] * inv[None, :]
    ang_c = pos_col[:, None] * inv[None, :]
    cos = jnp.concatenate([jnp.cos(ang_r)] * 2 + [jnp.cos(ang_c)] * 2, axis=-1)
    sin = jnp.concatenate([-jnp.sin(ang_r), jnp.sin(ang_r), -jnp.sin(ang_c), jnp.sin(ang_c)],
                          axis=-1)
    return cos, sin


def _trunk(x, mod, seq_len, conv_block, tm, rope_tabs, s0, want_state, p):
    m, d = x.shape
    mod0, mod1 = mod[0], mod[1]
    nh = RET_HEADS
    dv = p['ret_gn'].shape[1] // nh
    dk = (p['ret_w_in'].shape[2] - 2 * nh * dv) // (2 * nh)

    h = _norm_mod(x, p['norm_mix'], 0, mod0, seq_len, 0, 1)
    u = _hy_in(h, p['hy_w_in'], p['hy_b_in'], p['hy_conv_w'], p['hy_conv_b'], 0, seq_len,
               tm=max(tm, seq_len))
    fwd_lo, fwd_hi, inv = _odd_dft_mats(conv_block)
    spectra = _hyena_filter_spectra(
        seq_len, conv_block, p['hy_filt_w1'][0], p['hy_filt_b1'][0], p['hy_filt_freq'][0],
        p['hy_filt_w2'][0], p['hy_filt_b2'][0], p['hy_filt_w3'][0], fwd_lo, fwd_hi)
    slabs = d // u.shape[2]
    seqs = max(1, 1024 // seq_len)
    z = _long_conv(u, 0, u, slabs, spectra, 0, p['hy_bias_d'], 0, fwd_lo, inv, seq_len,
                   conv_block, F32, seqs)
    z = _long_conv(z, 0, u, 2 * slabs, spectra, 1, p['hy_bias_d'], 0, fwd_lo, inv, seq_len,
                   conv_block, BF16, seqs)
    tr = tm // 2
    ffn_tk = p['ffn_w2_bf16'].shape[1] // 4
    x, h = _resid_norm_proj(z, p['hy_w_out_bf16'], p['hy_b_out'], 0, x, mod0, 2, seq_len,
                            p['norm_ffn'], 0, mod0, 3, 4, tr, d)
    a = _swiglu_up(h, p['ffn_w1'], p['ffn_w3'], 0, tm=tm)
    x, h = _resid_norm_proj(a, p['ffn_w2_bf16'], None, 0, x, mod0, 5, seq_len,
                            p['norm_mix'], 1, mod1, 0, 1, tr, ffn_tk)

    qk = _ret_qk_proj(h, p['ret_w_in'], 0, nh, dk, seq_len, rope_tabs, tm=tm)
    v = _proj(h, p['ret_w_in'], 0, 2 * nh * dk, nh * dv, dv, BF16, tm=tm)
    sg = _proj(h, p['ret_w_in'], 0, 2 * nh * dk + nh * dv, nh * dv, dv, F32, silu=True, tm=tm)
    og, s_fin = _retention(qk, v, sg, p['ret_decay'], p['ret_gn'], 0, seq_len, s0,
                           want_state, x.dtype)
    x, h = _resid_norm_proj(og, p['ret_w_out_bf16'], None, 0, x, mod1, 2, seq_len,
                            p['norm_ffn'], 1, mod1, 3, 4, tr, d)
    a = _swiglu_up(h, p['ffn_w1'], p['ffn_w3'], 1, tm=tm)
    y = _resid_norm_proj(a, p['ffn_w2_bf16'], None, 1, x, mod1, 5, seq_len,
                         p['norm_final'], 0, None, 0, 0, tr, ffn_tk)
    return y, s_fin


def kernel(x_prompt, x_sample, state_ret, c, c_ctx, w_ada, b_ada, norm_mix, norm_ffn, norm_final,
           ffn_w1, ffn_w3, ffn_w2, hy_w_in, hy_b_in, hy_conv_w, hy_conv_b, hy_filt_w1, hy_filt_b1,
           hy_filt_freq, hy_filt_w2, hy_filt_b2, hy_filt_w3, hy_bias_d, hy_w_out, hy_b_out,
           ret_w_in, ret_decay, ret_gn, ret_w_out):
    p = dict(norm_mix=norm_mix, norm_ffn=norm_ffn, norm_final=norm_final,
             ffn_w1=ffn_w1, ffn_w3=ffn_w3, ffn_w2=ffn_w2, hy_w_in=hy_w_in, hy_b_in=hy_b_in,
             hy_conv_w=hy_conv_w, hy_conv_b=hy_conv_b, hy_filt_w1=hy_filt_w1,
             hy_filt_b1=hy_filt_b1, hy_filt_freq=hy_filt_freq, hy_filt_w2=hy_filt_w2,
             hy_filt_b2=hy_filt_b2, hy_filt_w3=hy_filt_w3, hy_bias_d=hy_bias_d,
             hy_w_out=hy_w_out, hy_b_out=hy_b_out, ret_w_in=ret_w_in, ret_decay=ret_decay,
             ret_gn=ret_gn, ret_w_out=ret_w_out)
    for name in ('hy_w_out', 'ret_w_out', 'ffn_w2'):
        p[name + '_bf16'] = p[name].astype(BF16)
    n_ctx, ctx_len, d = x_prompt.shape
    n_dec, dec_len, _ = x_sample.shape
    depth = w_ada.shape[0]

    cond_rows = 16
    cond = jnp.concatenate(
        [c_ctx[None, :], c, jnp.zeros((cond_rows - 1 - n_dec, d), c.dtype)], axis=0)
    mod = _ada_mod(cond, w_ada, b_ada).reshape(depth, cond_rows, 6, d)
    mod_ctx = mod[:, 0:1]
    mod_dec = mod[:, 1:1 + n_dec]

    y_prompt, ctx_state = _trunk(
        x_prompt.reshape(n_ctx * ctx_len, d), mod_ctx, ctx_len, conv_block=ctx_len, tm=1024,
        rope_tabs=None, s0=None, want_state=True, p=p)
    dk = state_ret.shape[-2]
    y_sample, _ = _trunk(
        x_sample.reshape(n_dec * dec_len, d), mod_dec, dec_len, conv_block=512, tm=1024,
        rope_tabs=_rope_tables(dec_len, dk), s0=state_ret, want_state=False, p=p)

    return (y_prompt.reshape(x_prompt.shape), y_sample.reshape(x_sample.shape), ctx_state)
```

```python
import functools
import math

import numpy as np
import jax
import jax.numpy as jnp
from jax import lax
from jax.experimental import pallas as pl
from jax.experimental.pallas import tpu as pltpu

F32 = jnp.float32
BF16 = jnp.bfloat16

EPS = 1e-6
GRID_W = 64
HY_BANDS = 16
HY_DECAY_PCT_SHORT = 0.3
HY_DECAY_PCT_LONG = 1.5
HY_TARGET = 1e-2
RET_HEADS = 8
ROPE_BASE = 10000.0
RET_CHUNK = 256
MAC_ROWS = 8
K_CHUNK_MAX = 2048

V7X_VMEM_BYTES = 64 * 1024 * 1024
VMEM_LIMIT = V7X_VMEM_BYTES - 8 * 1024 * 1024
LANES = 128
HIGHEST = lax.Precision.HIGHEST


def _params(n_axes):
    return pltpu.CompilerParams(
        dimension_semantics=("arbitrary",) * n_axes, vmem_limit_bytes=VMEM_LIMIT)


def _dot(a, b):
    return jnp.dot(a, b, preferred_element_type=F32)


def _ada_kernel(c_ref, w_ref, b_ref, o_ref):
    s = jax.nn.silu(c_ref[...]).astype(BF16)
    o_ref[...] = _dot(s, w_ref[...].astype(BF16)) + b_ref[...]


def _ada_mod(cond, w_ada, b_ada, tn=1536):
    depth, d, n = w_ada.shape
    rows = cond.shape[0]
    return pl.pallas_call(
        _ada_kernel,
        grid=(depth, n // tn),
        in_specs=[
            pl.BlockSpec((rows, d), lambda l, j: (0, 0)),
            pl.BlockSpec((None, d, tn), lambda l, j: (l, 0, j)),
            pl.BlockSpec((None, 1, tn), lambda l, j: (l, 0, j)),
        ],
        out_specs=pl.BlockSpec((None, rows, tn), lambda l, j: (l, 0, j)),
        out_shape=jax.ShapeDtypeStruct((depth, rows, n), F32),
        compiler_params=_params(2),
        name="ada_mod",
    )(cond, w_ada, b_ada.reshape(depth, 1, n))


def _seq_of_tile(tm, seq_len, n_mod):
    if n_mod == 1:
        return lambda i: 0
    assert seq_len % tm == 0
    return lambda i: (i * tm) // seq_len


def _norm_mod_kernel(x_ref, g_ref, mod_ref, o_ref, *, shift_idx, scale_idx):
    x = x_ref[...]
    y = x * lax.rsqrt(jnp.mean(x * x, axis=-1, keepdims=True) + EPS)
    y = y * g_ref[...]
    scale = mod_ref[scale_idx:scale_idx + 1, :]
    shift = mod_ref[shift_idx:shift_idx + 1, :]
    o_ref[...] = (y * (1.0 + scale) + shift).astype(o_ref.dtype)


def _norm_mod(x, g, layer, mod, seq_len, shift_idx, scale_idx, tm=512):
    m, d = x.shape
    seq = _seq_of_tile(tm, seq_len, mod.shape[0])
    return pl.pallas_call(
        functools.partial(_norm_mod_kernel, shift_idx=shift_idx, scale_idx=scale_idx),
        grid=(m // tm,),
        in_specs=[
            pl.BlockSpec((tm, d), lambda i: (i, 0)),
            pl.BlockSpec((None, 1, d), lambda i: (layer, 0, 0)),
            pl.BlockSpec((None, 6, d), lambda i: (seq(i), 0, 0)),
        ],
        out_specs=pl.BlockSpec((tm, d), lambda i: (i, 0)),
        out_shape=jax.ShapeDtypeStruct((m, d), BF16),
        compiler_params=_params(1),
        name="norm_mod",
    )(x, g.reshape(g.shape[0], 1, d), mod)


def _cast_weights_once(pairs):
    @pl.when(pl.program_id(1) == 0)
    def _():
        for w_ref, wb_ref in pairs:
            wb_ref[...] = w_ref[...].astype(BF16)


def _w_spec(k, tn, layer, col0=0):
    assert col0 % tn == 0
    return pl.BlockSpec((None, k, tn), lambda j, i: (layer, 0, col0 // tn + j))


def _vec_spec(tn, layer, col0=0):
    return pl.BlockSpec((None, 1, tn), lambda j, i: (layer, 0, col0 // tn + j))


def _rotate_pairs(x, half):
    parts = [pltpu.roll(x[:, g * half:(g + 1) * half], half // 2, 1)
             for g in range(x.shape[1] // half)]
    return jnp.concatenate(parts, axis=1)


def _ret_qk_kernel(*refs, dk, k_tile0, k_scale, use_rope):
    if use_rope:
        a_ref, w_ref, cos_ref, sin_ref, o_ref, wb_ref = refs
    else:
        a_ref, w_ref, o_ref, wb_ref = refs
    _cast_weights_once([(w_ref, wb_ref)])
    y = _dot(a_ref[...], wb_ref[...])
    y = y * jnp.where(pl.program_id(0) >= k_tile0, k_scale, 1.0).astype(F32)
    for hb in range(y.shape[1] // dk):
        x = y[:, hb * dk:(hb + 1) * dk]
        if use_rope:
            x = x * cos_ref[...] + _rotate_pairs(x, dk // 2) * sin_ref[...]
        o_ref[hb] = x


def _ret_qk_proj(a, w, layer, nh, dk, seq_len, rope_tabs, tm=1024, tn=1024):
    m, k = a.shape
    n = 2 * nh * dk
    use_rope = rope_tabs is not None
    in_specs = [pl.BlockSpec((tm, k), lambda j, i: (i, 0)), _w_spec(k, tn, layer)]
    args = [a, w]
    if use_rope:
        assert seq_len % tm == 0
        tiles_per_seq = seq_len // tm
        tab_spec = pl.BlockSpec((tm, dk), lambda j, i: (i % tiles_per_seq, 0))
        in_specs += [tab_spec, tab_spec]
        args += list(rope_tabs)
    return pl.pallas_call(
        functools.partial(_ret_qk_kernel, dk=dk, k_tile0=(nh * dk) // tn, k_scale=dk ** -0.5,
                          use_rope=use_rope),
        grid=(n // tn, m // tm),
        in_specs=in_specs,
        out_specs=pl.BlockSpec((tn // dk, tm, dk), lambda j, i: (j, i, 0)),
        out_shape=jax.ShapeDtypeStruct((n // dk, m, dk), F32),
        scratch_shapes=[pltpu.VMEM((k, tn), BF16)],
        compiler_params=_params(2),
        name="ret_qk_proj",
    )(*args)


def _proj_kernel(a_ref, w_ref, o_ref, wb_ref, *, silu):
    _cast_weights_once([(w_ref, wb_ref)])
    y = _dot(a_ref[...], wb_ref[...])
    if silu:
        y = jax.nn.silu(y)
    oc = o_ref.shape[2]
    for c in range(o_ref.shape[0]):
        o_ref[c] = y[:, c * oc:(c + 1) * oc].astype(o_ref.dtype)


def _proj(a, w, layer, col0, n, oc, out_dtype, silu=False, tm=1024, tn=1024):
    m, k = a.shape
    return pl.pallas_call(
        functools.partial(_proj_kernel, silu=silu),
        grid=(n // tn, m // tm),
        in_specs=[pl.BlockSpec((tm, k), lambda j, i: (i, 0)), _w_spec(k, tn, layer, col0)],
        out_specs=pl.BlockSpec((tn // oc, tm, oc), lambda j, i: (j, i, 0)),
        out_shape=jax.ShapeDtypeStruct((n // oc, m, oc), out_dtype),
        scratch_shapes=[pltpu.VMEM((k, tn), BF16)],
        compiler_params=_params(2),
        name="proj",
    )(a, w)


def _hy_in_kernel(a_ref, w_ref, b_ref, cw_ref, cb_ref, o_ref, wb_ref, *, seq_len, row_blocks):
    _cast_weights_once([(w_ref, wb_ref)])
    n_slab, tm, ct = o_ref.shape
    rq = tm // row_blocks
    wb = wb_ref[...]
    bias = b_ref[...]
    us = [_dot(a_ref[q * rq:(q + 1) * rq, :], wb) + bias for q in range(row_blocks)]
    w0, w1, w2, cb = cw_ref[0:1, :], cw_ref[1:2, :], cw_ref[2:3, :], cb_ref[...]

    def tap_sum(prev, cur, nxt):
        acc = cur * w1
        if prev is not None:
            acc = prev * w0 + acc
        if nxt is not None:
            acc = acc + nxt * w2
        return acc + cb

    def store(rows, val):
        for s in range(n_slab):
            o_ref[s, rows, :] = val[:, s * ct:(s + 1) * ct]

    for q, u in enumerate(us):
        r0 = q * rq
        store(slice(r0, r0 + rq),
              tap_sum(pltpu.roll(u, 1, 0), u, pltpu.roll(u, rq - 1, 0)))
        prev = None if r0 % seq_len == 0 else us[q - 1][rq - 1:rq]
        store(slice(r0, r0 + 1), tap_sum(prev, u[0:1], u[1:2]))
        nxt = None if (r0 + rq) % seq_len == 0 else us[q + 1][0:1]
        store(slice(r0 + rq - 1, r0 + rq), tap_sum(u[rq - 2:rq - 1], u[rq - 1:rq], nxt))
        for r in range(seq_len, rq, seq_len):
            store(slice(r0 + r, r0 + r + 1), tap_sum(None, u[r:r + 1], u[r + 1:r + 2]))
            store(slice(r0 + r - 1, r0 + r), tap_sum(u[r - 2:r - 1], u[r - 1:r], None))


def _hy_in(a, w, b, cw, cb, layer, seq_len, tm, tn=512, ct=256, row_blocks=4):
    m, k = a.shape
    n = w.shape[2]
    assert tm % seq_len == 0
    taps = cw.shape[1]
    return pl.pallas_call(
        functools.partial(_hy_in_kernel, seq_len=seq_len, row_blocks=row_blocks),
        grid=(n // tn, m // tm),
        in_specs=[
            pl.BlockSpec((tm, k), lambda j, i: (i, 0)),
            _w_spec(k, tn, layer),
            _vec_spec(tn, layer),
            pl.BlockSpec((None, taps, tn), lambda j, i: (layer, 0, j)),
            _vec_spec(tn, layer),
        ],
        out_specs=pl.BlockSpec((tn // ct, tm, ct), lambda j, i: (j, i, 0)),
        out_shape=jax.ShapeDtypeStruct((n // ct, m, ct), F32),
        scratch_shapes=[pltpu.VMEM((k, tn), BF16)],
        compiler_params=_params(2),
        name="hy_in",
    )(a, w, b.reshape(b.shape[0], 1, n), cw, cb.reshape(cb.shape[0], 1, n))


def _swiglu_kernel(a_ref, w1_ref, w3_ref, o_ref, w1b_ref, w3b_ref):
    _cast_weights_once([(w1_ref, w1b_ref), (w3_ref, w3b_ref)])
    a = a_ref[...]
    o_ref[...] = (jax.nn.silu(_dot(a, w1b_ref[...])) * _dot(a, w3b_ref[...])).astype(BF16)


def _swiglu_up(a, w1, w3, layer, tm=1024, tn=512):
    m, k = a.shape
    n = w1.shape[2]
    return pl.pallas_call(
        _swiglu_kernel,
        grid=(n // tn, m // tm),
        in_specs=[pl.BlockSpec((tm, k), lambda j, i: (i, 0)),
                  _w_spec(k, tn, layer), _w_spec(k, tn, layer)],
        out_specs=pl.BlockSpec((tm, tn), lambda j, i: (i, j)),
        out_shape=jax.ShapeDtypeStruct((m, n), BF16),
        scratch_shapes=[pltpu.VMEM((k, tn), BF16), pltpu.VMEM((k, tn), BF16)],
        compiler_params=_params(2),
        name="swiglu_up",
    )(a, w1, w3)


def _resid_norm_kernel(*refs, layer, gate_idx, has_bias, final, shift_idx, scale_idx,
                       row_blocks, k_chunk):
    it = iter(refs)
    a_ref, w_hbm_ref = next(it), next(it)
    b_ref = next(it) if has_bias else None
    x_ref, mod_ref = next(it), next(it)
    nmod_ref = None if final else next(it)
    gain_ref = next(it)
    xo_ref = None if final else next(it)
    h_ref = next(it)
    w_ref, w_sem = next(it), next(it)
    tm = x_ref.shape[0]
    rb = tm // row_blocks

    @pl.when(pl.program_id(0) == 0)
    def _():
        copy = pltpu.make_async_copy(w_hbm_ref.at[layer], w_ref, w_sem)
        copy.start()
        copy.wait()

    def finish(rows, y):
        if has_bias:
            y = y + b_ref[...]
        xn = x_ref[rows, :] + mod_ref[gate_idx:gate_idx + 1, :] * y
        if not final:
            xo_ref[rows, :] = xn
        hn = xn * lax.rsqrt(jnp.mean(xn * xn, axis=-1, keepdims=True) + EPS)
        hn = hn * gain_ref[...]
        if not final:
            hn = hn * (1.0 + nmod_ref[scale_idx:scale_idx + 1, :]) \
                + nmod_ref[shift_idx:shift_idx + 1, :]
        h_ref[rows, :] = hn.astype(h_ref.dtype)

    kk = w_ref.shape[0]
    for r in range(row_blocks):
        rows = slice(r * rb, (r + 1) * rb)
        y = None
        for k0 in range(0, kk, k_chunk):
            if len(a_ref.shape) == 3:
                cw = a_ref.shape[2]
                a = jnp.concatenate([a_ref[c, rows, :]
                                     for c in range(k0 // cw, (k0 + k_chunk) // cw)], axis=1)
            else:
                a = a_ref[rows, k0:k0 + k_chunk]
            part = _dot(a, w_ref[k0:k0 + k_chunk, :])
            y = part if y is None else y + part
        finish(rows, y)


def _resid_norm_proj(a, w, bias, layer, x, mod, gate_idx, seq_len, gain, gain_layer,
                     next_mod, shift_idx, scale_idx, tm, row_blocks=2):
    _, kk, d = w.shape
    k_chunk = kk if kk <= K_CHUNK_MAX else next(
        c for c in range(K_CHUNK_MAX, 0, -LANES) if kk % c == 0)
    final = next_mod is None
    seq = _seq_of_tile(tm, seq_len, mod.shape[0])
    if a.ndim == 3:
        n_slab, m, cw = a.shape
        assert n_slab * cw == kk
        a_spec = pl.BlockSpec((n_slab, tm, cw), lambda i: (0, i, 0))
    else:
        m = a.shape[0]
        a_spec = pl.BlockSpec((tm, kk), lambda i: (i, 0))
    in_specs = [a_spec, pl.BlockSpec(memory_space=pl.ANY)]
    args = [a, w]
    if bias is not None:
        in_specs.append(pl.BlockSpec((None, 1, d), lambda i: (layer, 0, 0)))
        args.append(bias.reshape(bias.shape[0], 1, d))
    mod_spec = pl.BlockSpec((None, 6, d), lambda i: (seq(i), 0, 0))
    in_specs += [pl.BlockSpec((tm, d), lambda i: (i, 0)), mod_spec]
    args += [x, mod]
    if not final:
        in_specs.append(mod_spec)
        args.append(next_mod)
    if gain.ndim == 1:
        gain = gain[None]
    in_specs.append(pl.BlockSpec((None, 1, d), lambda i: (gain_layer, 0, 0)))
    args.append(gain.reshape(gain.shape[0], 1, d))
    row_spec = pl.BlockSpec((tm, d), lambda i: (i, 0))
    if final:
        out_specs = [row_spec]
        out_shape = [jax.ShapeDtypeStruct((m, d), F32)]
    else:
        out_specs = [row_spec, row_spec]
        out_shape = [jax.ShapeDtypeStruct((m, d), F32), jax.ShapeDtypeStruct((m, d), BF16)]
    outs = pl.pallas_call(
        functools.partial(_resid_norm_kernel, layer=layer, gate_idx=gate_idx,
                          has_bias=bias is not None, final=final, shift_idx=shift_idx,
                          scale_idx=scale_idx, row_blocks=row_blocks, k_chunk=k_chunk),
        grid=(m // tm,),
        in_specs=in_specs,
        out_specs=out_specs,
        out_shape=out_shape,
        scratch_shapes=[pltpu.VMEM((kk, d), BF16), pltpu.SemaphoreType.DMA(())],
        compiler_params=_params(1),
        name="resid_norm_proj",
    )(*args)
    return outs[0] if final else (outs[0], outs[1])


def _odd_dft_mats(p):
    f = np.arange(p, dtype=np.int64)
    m = np.arange(2 * p, dtype=np.int64)
    phase = ((2 * f[:, None] + 1) * m[None, :]) % (4 * p)
    ang = np.pi * phase.astype(np.float64) / (2 * p)
    fwd = np.concatenate([np.cos(ang), -np.sin(ang)], axis=0)
    fwd_lo = fwd[:, :p]
    fwd_hi = fwd[:, p:].copy()
    fwd_hi[:, 0] = 0.0
    t = np.arange(p, dtype=np.int64)
    phase_i = (t[:, None] * (2 * f[None, :] + 1)) % (4 * p)
    ang_i = np.pi * phase_i.astype(np.float64) / (2 * p)
    inv = np.concatenate([np.cos(ang_i), -np.sin(ang_i)], axis=1) / p
    as_bf16 = lambda a: jnp.asarray(a, F32).astype(BF16)
    return as_bf16(fwd_lo), as_bf16(fwd_hi), as_bf16(inv)


def _filter_mlp_kernel(w1_ref, b1_ref, fr_ref, w2_ref, b2_ref, o_ref, *, seq_len):
    rows = 2 * seq_len
    r = lax.broadcasted_iota(jnp.int32, (rows, LANES), 0)
    lane = lax.broadcasted_iota(jnp.int32, (rows, LANES), 1)
    t = jnp.abs(r - seq_len).astype(F32) / seq_len
    band = jnp.where(lane <= HY_BANDS, lane, lane - HY_BANDS).astype(F32)
    ang = 2.0 * math.pi * t * band
    feat = jnp.where(lane == 0, t,
                     jnp.where(lane <= HY_BANDS, jnp.cos(ang),
                               jnp.where(lane <= 2 * HY_BANDS, jnp.sin(ang), 0.0)))
    z = jnp.dot(feat, w1_ref[...], precision=HIGHEST, preferred_element_type=F32)
    z = jnp.sin(fr_ref[0:1, :] * (z + b1_ref[...]))
    z = jnp.dot(z, w2_ref[...], precision=HIGHEST, preferred_element_type=F32)
    o_ref[...] = jnp.sin(fr_ref[1:2, :] * (z + b2_ref[...]))


def _filter_time_kernel(z_ref, wf_ref, wb_ref, dl_ref, o_ref, *, seq_len):
    z = z_ref[...].astype(BF16)
    ff = _dot(z, wf_ref[...].astype(BF16))
    fb = _dot(z, wb_ref[...].astype(BF16))
    n = lax.broadcasted_iota(jnp.int32, ff.shape, 0) - seq_len
    t = jnp.abs(n).astype(F32) / seq_len
    window = jnp.exp(-t * dl_ref[...])
    kk = jnp.where(n >= 0, ff, fb) * window
    o_ref[...] = jnp.where(n == -seq_len, 0.0, kk)


def _filter_spec_kernel(ka_ref, kb_ref, flo_ref, fhi_ref, o_ref):
    o_ref[...] = (_dot(flo_ref[...], ka_ref[...].astype(BF16))
                  - _dot(fhi_ref[...], kb_ref[...].astype(BF16)))


def _hyena_filter_spectra(seq_len, p, w1, b1, freq, w2, b2, w3, fwd_lo, fwd_hi, ct=512):
    hidden = w1.shape[1]
    n_order = 2
    d = w3.shape[1] // (2 * n_order)
    nb = seq_len // p
    nd = 2 * nb - 1
    rows = 2 * seq_len
    w1p = jnp.pad(w1, ((0, LANES - w1.shape[0]), (0, 0)))
    full = lambda shape: pl.BlockSpec(shape, lambda *_: (0,) * len(shape))
    z = pl.pallas_call(
        functools.partial(_filter_mlp_kernel, seq_len=seq_len),
        grid=(1,),
        in_specs=[full((LANES, hidden)), full((1, hidden)), full((2, hidden)),
                  full((hidden, hidden)), full((1, hidden))],
        out_specs=full((rows, hidden)),
        out_shape=jax.ShapeDtypeStruct((rows, hidden), F32),
        compiler_params=_params(1),
        name="filter_mlp",
    )(w1p, b1.reshape(1, hidden), freq, w2, b2.reshape(1, hidden))

    min_decay = abs(math.log(HY_TARGET) / HY_DECAY_PCT_LONG)
    max_decay = abs(math.log(HY_TARGET) / HY_DECAY_PCT_SHORT)
    deltas = jnp.linspace(min_decay, max_decay, d, dtype=F32).reshape(1, d)
    nct = d // ct
    kk = pl.pallas_call(
        functools.partial(_filter_time_kernel, seq_len=seq_len),
        grid=(n_order, nct),
        in_specs=[
            pl.BlockSpec((rows, hidden), lambda o, j: (0, 0)),
            pl.BlockSpec((hidden, ct), lambda o, j: (0, (2 * o) * nct + j)),
            pl.BlockSpec((hidden, ct), lambda o, j: (0, (2 * o + 1) * nct + j)),
            pl.BlockSpec((1, ct), lambda o, j: (0, j)),
        ],
        out_specs=pl.BlockSpec((None, rows, ct), lambda o, j: (o, 0, j)),
        out_shape=jax.ShapeDtypeStruct((n_order, rows, d), F32),
        compiler_params=_params(2),
        name="filter_time",
    )(z, w3, w3, deltas)

    return pl.pallas_call(
        _filter_spec_kernel,
        grid=(n_order, d // ct, nd),
        in_specs=[
            pl.BlockSpec((None, p, ct), lambda o, j, e: (o, e + 1, j)),
            pl.BlockSpec((None, p, ct), lambda o, j, e: (o, e, j)),
            pl.BlockSpec((2 * p, p), lambda o, j, e: (0, 0)),
            pl.BlockSpec((2 * p, p), lambda o, j, e: (0, 0)),
        ],
        out_specs=pl.BlockSpec((None, None, 2 * p, ct), lambda o, j, e: (o, e, 0, j)),
        out_shape=jax.ShapeDtypeStruct((n_order, nd, 2 * p, d), F32),
        compiler_params=_params(3),
        name="filter_spec",
    )(kk, kk, fwd_lo, fwd_hi)


def _long_conv_kernel(z_ref, gate_ref, k_ref, bias_ref, fwd_ref, inv_ref, o_ref,
                      zs_ref, ys_ref, *, p, nb, n_seq):
    fwd = fwd_ref[...]
    inv = inv_ref[...]
    seq_len = p * nb
    for s in range(n_seq):
        base = s * seq_len
        for b in range(nb):
            zs_ref[b] = _dot(fwd, z_ref[base + b * p:base + (b + 1) * p, :].astype(BF16))
        for r in range(0, p, MAC_ROWS):
            re = slice(r, r + MAC_ROWS)
            im = slice(p + r, p + r + MAC_ROWS)
            kr = [k_ref[e, re, :] for e in range(2 * nb - 1)]
            ki = [k_ref[e, im, :] for e in range(2 * nb - 1)]
            yr = [None] * nb
            yi = [None] * nb
            for b in range(nb):
                zr = zs_ref[b, re, :]
                zi = zs_ref[b, im, :]
                for a in range(nb):
                    e = a - b + nb - 1
                    tr = kr[e] * zr - ki[e] * zi
                    ti = kr[e] * zi + ki[e] * zr
                    yr[a] = tr if yr[a] is None else yr[a] + tr
                    yi[a] = ti if yi[a] is None else yi[a] + ti
            for a in range(nb):
                ys_ref[a, re, :] = yr[a]
                ys_ref[a, im, :] = yi[a]
        for a in range(nb):
            y = _dot(inv, ys_ref[a].astype(BF16))
            rows = slice(base + a * p, base + (a + 1) * p)
            za = z_ref[rows, :].astype(F32)
            o_ref[rows, :] = (gate_ref[rows, :] * (y + bias_ref[...] * za)).astype(o_ref.dtype)


def _long_conv(z, z_slab, gate, gate_slab, spectra, order, bias, layer, fwd_lo, inv, seq_len, p,
               out_dtype, seqs_per_step=1):
    _, m, ct = z.shape
    d = spectra.shape[-1]
    nb = seq_len // p
    nd = 2 * nb - 1
    rows = seqs_per_step * seq_len
    return pl.pallas_call(
        functools.partial(_long_conv_kernel, p=p, nb=nb, n_seq=seqs_per_step),
        grid=(d // ct, m // rows),
        in_specs=[
            pl.BlockSpec((None, rows, ct), lambda j, b: (z_slab + j, b, 0)),
            pl.BlockSpec((None, rows, ct), lambda j, b: (gate_slab + j, b, 0)),
            pl.BlockSpec((None, nd, 2 * p, ct), lambda j, b: (order, 0, 0, j)),
            pl.BlockSpec((None, None, 1, ct), lambda j, b: (layer, order, 0, j)),
            pl.BlockSpec((2 * p, p), lambda j, b: (0, 0)),
            pl.BlockSpec((p, 2 * p), lambda j, b: (0, 0)),
        ],
        out_specs=pl.BlockSpec((None, rows, ct), lambda j, b: (j, b, 0)),
        out_shape=jax.ShapeDtypeStruct((d // ct, m, ct), out_dtype),
        scratch_shapes=[pltpu.VMEM((nb, 2 * p, ct), F32), pltpu.VMEM((nb, 2 * p, ct), F32)],
        compiler_params=_params(2),
        name="long_conv",
    )(z, gate, spectra, bias.reshape(bias.shape[0], bias.shape[1], 1, d), fwd_lo, inv)


def _retention_kernel(*refs, n_chunks, has_s0, want_state):
    it = iter(refs)
    dec_ref, q_ref, k_ref, v_ref, sg_ref, gn_ref = (next(it) for _ in range(6))
    s0_ref = next(it) if has_s0 else None
    o_ref = next(it)
    sfin_ref = next(it) if want_state else None
    acc_ref, s_ref, dmat_ref, xi_ref, zeta_ref = (next(it) for _ in range(5))

    c_len = RET_CHUNK
    dk = q_ref.shape[1]

    log_g = -jnp.exp(dec_ref[...])
    ii = lax.broadcasted_iota(jnp.int32, (c_len, c_len), 0)
    jj = lax.broadcasted_iota(jnp.int32, (c_len, c_len), 1)
    idx = lax.broadcasted_iota(jnp.int32, (c_len, 1), 0).astype(F32)
    g_chunk = []
    for direction in range(2):
        lg = log_g[direction]
        diff = (ii - jj) if direction == 0 else (jj - ii)
        dmat_ref[direction] = jnp.where(
            diff >= 0, jnp.exp(lg * jnp.maximum(diff, 0).astype(F32)), 0.0)
        if direction == 0:
            xi = jnp.exp(lg * (idx + 1.0))
            zeta = jnp.exp(lg * (c_len - 1.0 - idx))
        else:
            xi = jnp.exp(lg * (c_len - idx))
            zeta = jnp.exp(lg * idx)
        xi_ref[direction] = jnp.broadcast_to(xi, (c_len, dk))
        zeta_ref[direction] = jnp.broadcast_to(zeta, (c_len, dk))
        g_chunk.append(jnp.exp(lg * c_len))
        if has_s0:
            s_ref[direction] = s0_ref[direction].astype(F32)
        else:
            s_ref[direction] = jnp.zeros(s_ref.shape[1:], F32)

    def chunk(c):
        return pl.ds(pl.multiple_of(c * c_len, c_len), c_len)

    def scan_chunk(c, direction):
        rows = chunk(c)
        qc = q_ref[rows, :]
        kc = k_ref[rows, :]
        vc = v_ref[rows, :]
        s = s_ref[direction]
        scores = lax.dot_general(qc.astype(BF16), kc.astype(BF16), (((1,), (1,)), ((), ())),
                                 preferred_element_type=F32) * dmat_ref[direction]
        inner = _dot(scores.astype(BF16), vc)
        cross = _dot((qc * xi_ref[direction]).astype(BF16), s.astype(BF16))
        upd = lax.dot_general((kc * zeta_ref[direction]).astype(BF16), vc,
                              (((0,), (0,)), ((), ())), preferred_element_type=F32)
        s_ref[direction] = g_chunk[direction] * s + upd
        return inner + cross

    def norm_gate_store(rows, o):
        mu = jnp.mean(o, axis=-1, keepdims=True)
        var = jnp.mean(jnp.square(o - mu), axis=-1, keepdims=True)
        o = (o - mu) * lax.rsqrt(var + EPS)
        o = o * gn_ref[...]
        o_ref[rows, :] = (o * sg_ref[rows, :]).astype(o_ref.dtype)

    if n_chunks == 1:
        norm_gate_store(chunk(0), scan_chunk(0, 0) + scan_chunk(0, 1))
    else:
        assert n_chunks % 2 == 0

        def first_visit(t, carry):
            acc_ref[chunk(t), :] = scan_chunk(t, 0)
            acc_ref[chunk(n_chunks - 1 - t), :] = scan_chunk(n_chunks - 1 - t, 1)
            return carry

        def second_visit(t, carry):
            cf, cb = chunk(t), chunk(n_chunks - 1 - t)
            norm_gate_store(cf, acc_ref[cf, :] + scan_chunk(t, 0))
            norm_gate_store(cb, acc_ref[cb, :] + scan_chunk(n_chunks - 1 - t, 1))
            return carry

        lax.fori_loop(0, n_chunks // 2, first_visit, 0)
        lax.fori_loop(n_chunks // 2, n_chunks, second_visit, 0)

    if want_state:
        for direction in range(2):
            sfin_ref[direction] = s_ref[direction].astype(sfin_ref.dtype)


def _retention(qk, v, sg, decay, gn, layer, seq_len, s0, want_state, state_dtype):
    nh, m, dv = v.shape
    assert nh == RET_HEADS and qk.shape[0] == 2 * nh
    dk = qk.shape[2]
    n_seq = m // seq_len
    has_s0 = s0 is not None
    c_len = RET_CHUNK

    in_specs = [
        pl.BlockSpec((None, 2, None, 1, 1), lambda b, h: (layer, 0, h, 0, 0)),
        pl.BlockSpec((None, seq_len, dk), lambda b, h: (h, b, 0)),
        pl.BlockSpec((None, seq_len, dk), lambda b, h: (nh + h, b, 0)),
        pl.BlockSpec((None, seq_len, dv), lambda b, h: (h, b, 0)),
        pl.BlockSpec((None, seq_len, dv), lambda b, h: (h, b, 0)),
        pl.BlockSpec((None, 1, dv), lambda b, h: (layer, 0, h)),
    ]
    args = [decay.reshape(decay.shape[0], 2, nh, 1, 1), qk, qk, v, sg,
            gn.reshape(gn.shape[0], 1, nh * dv)]
    state_spec = pl.BlockSpec((None, None, 2, None, dk, dv),
                              lambda b, h: (b, layer, 0, h, 0, 0))
    if has_s0:
        in_specs.append(state_spec)
        args.append(s0)
    out_specs = [pl.BlockSpec((None, seq_len, dv), lambda b, h: (h, b, 0))]
    out_shape = [jax.ShapeDtypeStruct((nh, m, dv), BF16)]
    if want_state:
        out_specs.append(state_spec)
        out_shape.append(jax.ShapeDtypeStruct((n_seq, decay.shape[0], 2, nh, dk, dv),
                                              state_dtype))
    outs = pl.pallas_call(
        functools.partial(_retention_kernel, n_chunks=seq_len // c_len, has_s0=has_s0,
                          want_state=want_state),
        grid=(n_seq, nh),
        in_specs=in_specs,
        out_specs=out_specs,
        out_shape=out_shape,
        scratch_shapes=[pltpu.VMEM((seq_len, dv), F32), pltpu.VMEM((2, dk, dv), F32),
                        pltpu.VMEM((2, c_len, c_len), F32), pltpu.VMEM((2, c_len, dk), F32),
                        pltpu.VMEM((2, c_len, dk), F32)],
        compiler_params=_params(2),
        name="retention",
    )(*args)
    return (outs[0], outs[1]) if want_state else (outs[0], None)


def _rope_tables(seq_len, dk):
    rows = seq_len // GRID_W
    pos_row = jnp.repeat(jnp.arange(rows, dtype=F32), GRID_W)
    pos_col = jnp.tile(jnp.arange(GRID_W, dtype=F32), rows)
    n = dk // 4
    inv = jnp.exp(-math.log(ROPE_BASE) * jnp.arange(n, dtype=F32) / n)
    ang_r = pos_row[:, None] * inv[None, :]
    ang_c = pos_col[:, None] * inv[None, :]
    cos = jnp.concatenate([jnp.cos(ang_r)] * 2 + [jnp.cos(ang_c)] * 2, axis=-1)
    sin = jnp.concatenate([-jnp.sin(ang_r), jnp.sin(ang_r), -jnp.sin(ang_c), jnp.sin(ang_c)],
                          axis=-1)
    return cos, sin


def _trunk(x, mod, seq_len, conv_block, tm, rope_tabs, s0, want_state, p):
    m, d = x.shape
    mod0, mod1 = mod[0], mod[1]
    nh = RET_HEADS
    dv = p['ret_gn'].shape[1] // nh
    dk = (p['ret_w_in'].shape[2] - 2 * nh * dv) // (2 * nh)

    h = _norm_mod(x, p['norm_mix'], 0, mod0, seq_len, 0, 1)
    u = _hy_in(h, p['hy_w_in'], p['hy_b_in'], p['hy_conv_w'], p['hy_conv_b'], 0, seq_len,
               tm=max(tm, seq_len))
    fwd_lo, fwd_hi, inv = _odd_dft_mats(conv_block)
    spectra = _hyena_filter_spectra(
        seq_len, conv_block, p['hy_filt_w1'][0], p['hy_filt_b1'][0], p['hy_filt_freq'][0],
        p['hy_filt_w2'][0], p['hy_filt_b2'][0], p['hy_filt_w3'][0], fwd_lo, fwd_hi)
    slabs = d // u.shape[2]
    seqs = max(1, 1024 // seq_len)
    z = _long_conv(u, 0, u, slabs, spectra, 0, p['hy_bias_d'], 0, fwd_lo, inv, seq_len,
                   conv_block, F32, seqs)
    z = _long_conv(z, 0, u, 2 * slabs, spectra, 1, p['hy_bias_d'], 0, fwd_lo, inv, seq_len,
                   conv_block, BF16, seqs)
    tr, tr_ffn = tm // 2, tm // 4
    x, h = _resid_norm_proj(z, p['hy_w_out_bf16'], p['hy_b_out'], 0, x, mod0, 2, seq_len,
                            p['norm_ffn'], 0, mod0, 3, 4, tr)
    a = _swiglu_up(h, p['ffn_w1'], p['ffn_w3'], 0, tm=tm)
    x, h = _resid_norm_proj(a, p['ffn_w2_bf16'], None, 0, x, mod0, 5, seq_len,
                            p['norm_mix'], 1, mod1, 0, 1, tr_ffn)

    qk = _ret_qk_proj(h, p['ret_w_in'], 0, nh, dk, seq_len, rope_tabs, tm=tm)
    v = _proj(h, p['ret_w_in'], 0, 2 * nh * dk, nh * dv, dv, BF16, tm=tm)
    sg = _proj(h, p['ret_w_in'], 0, 2 * nh * dk + nh * dv, nh * dv, dv, F32, silu=True, tm=tm)
    og, s_fin = _retention(qk, v, sg, p['ret_decay'], p['ret_gn'], 0, seq_len, s0,
                           want_state, x.dtype)
    x, h = _resid_norm_proj(og, p['ret_w_out_bf16'], None, 0, x, mod1, 2, seq_len,
                            p['norm_ffn'], 1, mod1, 3, 4, tr)
    a = _swiglu_up(h, p['ffn_w1'], p['ffn_w3'], 1, tm=tm)
    y = _resid_norm_proj(a, p['ffn_w2_bf16'], None, 1, x, mod1, 5, seq_len,
                         p['norm_final'], 0, None, 0, 0, tr_ffn)
    return y, s_fin


def kernel(x_prompt, x_sample, state_ret, c, c_ctx, w_ada, b_ada, norm_mix, norm_ffn, norm_final,
           ffn_w1, ffn_w3, ffn_w2, hy_w_in, hy_b_in, hy_conv_w, hy_conv_b, hy_filt_w1, hy_filt_b1,
           hy_filt_freq, hy_filt_w2, hy_filt_b2, hy_filt_w3, hy_bias_d, hy_w_out, hy_b_out,
           ret_w_in, ret_decay, ret_gn, ret_w_out):
    p = dict(norm_mix=norm_mix, norm_ffn=norm_ffn, norm_final=norm_final,
             ffn_w1=ffn_w1, ffn_w3=ffn_w3, ffn_w2=ffn_w2, hy_w_in=hy_w_in, hy_b_in=hy_b_in,
             hy_conv_w=hy_conv_w, hy_conv_b=hy_conv_b, hy_filt_w1=hy_filt_w1,
             hy_filt_b1=hy_filt_b1, hy_filt_freq=hy_filt_freq, hy_filt_w2=hy_filt_w2,
             hy_filt_b2=hy_filt_b2, hy_filt_w3=hy_filt_w3, hy_bias_d=hy_bias_d,
             hy_w_out=hy_w_out, hy_b_out=hy_b_out, ret_w_in=ret_w_in, ret_decay=ret_decay,
             ret_gn=ret_gn, ret_w_out=ret_w_out)
    for name in ('hy_w_out', 'ret_w_out', 'ffn_w2'):
        p[name + '_bf16'] = p[name].astype(BF16)
    n_ctx, ctx_len, d = x_prompt.shape
    n_dec, dec_len, _ = x_sample.shape
    depth = w_ada.shape[0]

    cond_rows = 16
    cond = jnp.concatenate(
        [c_ctx[None, :], c, jnp.zeros((cond_rows - 1 - n_dec, d), c.dtype)], axis=0)
    mod = _ada_mod(cond, w_ada, b_ada).reshape(depth, cond_rows, 6, d)
    mod_ctx = mod[:, 0:1]
    mod_dec = mod[:, 1:1 + n_dec]

    y_prompt, ctx_state = _trunk(
        x_prompt.reshape(n_ctx * ctx_len, d), mod_ctx, ctx_len, conv_block=ctx_len, tm=1024,
        rope_tabs=None, s0=None, want_state=True, p=p)
    dk = state_ret.shape[-2]
    y_sample, _ = _trunk(
        x_sample.reshape(n_dec * dec_len, d), mod_dec, dec_len, conv_block=512, tm=1024,
        rope_tabs=_rope_tables(dec_len, dk), s0=state_ret, want_state=False, p=p)

    return (y_prompt.reshape(x_prompt.shape), y_sample.reshape(x_sample.shape), ctx_state)
```

```python
import functools
import math

import numpy as np
import jax
import jax.numpy as jnp
from jax import lax
from jax.experimental import pallas as pl
from jax.experimental.pallas import tpu as pltpu

F32 = jnp.float32
BF16 = jnp.bfloat16

EPS = 1e-6
GRID_W = 64
HY_BANDS = 16
HY_DECAY_PCT_SHORT = 0.3
HY_DECAY_PCT_LONG = 1.5
HY_TARGET = 1e-2
RET_HEADS = 8
ROPE_BASE = 10000.0
RET_CHUNK = 256
MAC_ROWS = 8
V7X_MXU_K = 256
K_CHUNK_MAX = 3072

V7X_VMEM_BYTES = 64 * 1024 * 1024
VMEM_LIMIT = V7X_VMEM_BYTES - 8 * 1024 * 1024
LANES = 128
HIGHEST = lax.Precision.HIGHEST


def _params(n_axes):
    return pltpu.CompilerParams(
        dimension_semantics=("arbitrary",) * n_axes, vmem_limit_bytes=VMEM_LIMIT)


def _dot(a, b):
    return jnp.dot(a, b, preferred_element_type=F32)


def _ada_kernel(c_ref, w_ref, b_ref, o_ref):
    s = jax.nn.silu(c_ref[...]).astype(BF16)
    o_ref[...] = _dot(s, w_ref[...].astype(BF16)) + b_ref[...]


def _ada_mod(cond, w_ada, b_ada, tn=1536):
    depth, d, n = w_ada.shape
    rows = cond.shape[0]
    return pl.pallas_call(
        _ada_kernel,
        grid=(depth, n // tn),
        in_specs=[
            pl.BlockSpec((rows, d), lambda l, j: (0, 0)),
            pl.BlockSpec((None, d, tn), lambda l, j: (l, 0, j)),
            pl.BlockSpec((None, 1, tn), lambda l, j: (l, 0, j)),
        ],
        out_specs=pl.BlockSpec((None, rows, tn), lambda l, j: (l, 0, j)),
        out_shape=jax.ShapeDtypeStruct((depth, rows, n), F32),
        compiler_params=_params(2),
        name="ada_mod",
    )(cond, w_ada, b_ada.reshape(depth, 1, n))


def _seq_of_tile(tm, seq_len, n_mod):
    if n_mod == 1:
        return lambda i: 0
    assert seq_len % tm == 0
    return lambda i: (i * tm) // seq_len


def _norm_mod_kernel(x_ref, g_ref, mod_ref, o_ref, *, shift_idx, scale_idx):
    x = x_ref[...]
    y = x * lax.rsqrt(jnp.mean(x * x, axis=-1, keepdims=True) + EPS)
    y = y * g_ref[...]
    scale = mod_ref[scale_idx:scale_idx + 1, :]
    shift = mod_ref[shift_idx:shift_idx + 1, :]
    o_ref[...] = (y * (1.0 + scale) + shift).astype(o_ref.dtype)


def _norm_mod(x, g, layer, mod, seq_len, shift_idx, scale_idx, tm=512):
    m, d = x.shape
    seq = _seq_of_tile(tm, seq_len, mod.shape[0])
    return pl.pallas_call(
        functools.partial(_norm_mod_kernel, shift_idx=shift_idx, scale_idx=scale_idx),
        grid=(m // tm,),
        in_specs=[
            pl.BlockSpec((tm, d), lambda i: (i, 0)),
            pl.BlockSpec((None, 1, d), lambda i: (layer, 0, 0)),
            pl.BlockSpec((None, 6, d), lambda i: (seq(i), 0, 0)),
        ],
        out_specs=pl.BlockSpec((tm, d), lambda i: (i, 0)),
        out_shape=jax.ShapeDtypeStruct((m, d), BF16),
        compiler_params=_params(1),
        name="norm_mod",
    )(x, g.reshape(g.shape[0], 1, d), mod)


def _cast_weights_once(pairs):
    @pl.when(pl.program_id(1) == 0)
    def _():
        for w_ref, wb_ref in pairs:
            wb_ref[...] = w_ref[...].astype(BF16)


def _w_spec(k, tn, layer, col0=0):
    assert col0 % tn == 0
    return pl.BlockSpec((None, k, tn), lambda j, i: (layer, 0, col0 // tn + j))


def _vec_spec(tn, layer, col0=0):
    return pl.BlockSpec((None, 1, tn), lambda j, i: (layer, 0, col0 // tn + j))


def _rotate_pairs(x, half):
    parts = [pltpu.roll(x[:, g * half:(g + 1) * half], half // 2, 1)
             for g in range(x.shape[1] // half)]
    return jnp.concatenate(parts, axis=1)


def _ret_qk_kernel(*refs, dk, k_tile0, k_scale, use_rope):
    if use_rope:
        a_ref, w_ref, cos_ref, sin_ref, o_ref, wb_ref = refs
    else:
        a_ref, w_ref, o_ref, wb_ref = refs
    _cast_weights_once([(w_ref, wb_ref)])
    y = _dot(a_ref[...], wb_ref[...])
    y = y * jnp.where(pl.program_id(0) >= k_tile0, k_scale, 1.0).astype(F32)
    for hb in range(y.shape[1] // dk):
        x = y[:, hb * dk:(hb + 1) * dk]
        if use_rope:
            x = x * cos_ref[...] + _rotate_pairs(x, dk // 2) * sin_ref[...]
        o_ref[hb] = x


def _ret_qk_proj(a, w, layer, nh, dk, seq_len, rope_tabs, tm=1024, tn=1024):
    m, k = a.shape
    n = 2 * nh * dk
    use_rope = rope_tabs is not None
    in_specs = [pl.BlockSpec((tm, k), lambda j, i: (i, 0)), _w_spec(k, tn, layer)]
    args = [a, w]
    if use_rope:
        assert seq_len % tm == 0
        tiles_per_seq = seq_len // tm
        tab_spec = pl.BlockSpec((tm, dk), lambda j, i: (i % tiles_per_seq, 0))
        in_specs += [tab_spec, tab_spec]
        args += list(rope_tabs)
    return pl.pallas_call(
        functools.partial(_ret_qk_kernel, dk=dk, k_tile0=(nh * dk) // tn, k_scale=dk ** -0.5,
                          use_rope=use_rope),
        grid=(n // tn, m // tm),
        in_specs=in_specs,
        out_specs=pl.BlockSpec((tn // dk, tm, dk), lambda j, i: (j, i, 0)),
        out_shape=jax.ShapeDtypeStruct((n // dk, m, dk), F32),
        scratch_shapes=[pltpu.VMEM((k, tn), BF16)],
        compiler_params=_params(2),
        name="ret_qk_proj",
    )(*args)


def _proj_kernel(a_ref, w_ref, o_ref, wb_ref, *, silu):
    _cast_weights_once([(w_ref, wb_ref)])
    y = _dot(a_ref[...], wb_ref[...])
    if silu:
        y = jax.nn.silu(y)
    oc = o_ref.shape[2]
    for c in range(o_ref.shape[0]):
        o_ref[c] = y[:, c * oc:(c + 1) * oc].astype(o_ref.dtype)


def _proj(a, w, layer, col0, n, oc, out_dtype, silu=False, tm=1024, tn=1024):
    m, k = a.shape
    return pl.pallas_call(
        functools.partial(_proj_kernel, silu=silu),
        grid=(n // tn, m // tm),
        in_specs=[pl.BlockSpec((tm, k), lambda j, i: (i, 0)), _w_spec(k, tn, layer, col0)],
        out_specs=pl.BlockSpec((tn // oc, tm, oc), lambda j, i: (j, i, 0)),
        out_shape=jax.ShapeDtypeStruct((n // oc, m, oc), out_dtype),
        scratch_shapes=[pltpu.VMEM((k, tn), BF16)],
        compiler_params=_params(2),
        name="proj",
    )(a, w)


def _hy_in_kernel(a_ref, w_ref, b_ref, cw_ref, cb_ref, o_ref, wb_ref, *, seq_len, row_blocks):
    _cast_weights_once([(w_ref, wb_ref)])
    n_slab, tm, ct = o_ref.shape
    rq = tm // row_blocks
    wb = wb_ref[...]
    bias = b_ref[...]
    us = [_dot(a_ref[q * rq:(q + 1) * rq, :], wb) + bias for q in range(row_blocks)]
    w0, w1, w2, cb = cw_ref[0:1, :], cw_ref[1:2, :], cw_ref[2:3, :], cb_ref[...]

    def tap_sum(prev, cur, nxt):
        acc = cur * w1
        if prev is not None:
            acc = prev * w0 + acc
        if nxt is not None:
            acc = acc + nxt * w2
        return acc + cb

    def store(rows, val):
        for s in range(n_slab):
            o_ref[s, rows, :] = val[:, s * ct:(s + 1) * ct]

    for q, u in enumerate(us):
        r0 = q * rq
        store(slice(r0, r0 + rq),
              tap_sum(pltpu.roll(u, 1, 0), u, pltpu.roll(u, rq - 1, 0)))
        prev = None if r0 % seq_len == 0 else us[q - 1][rq - 1:rq]
        store(slice(r0, r0 + 1), tap_sum(prev, u[0:1], u[1:2]))
        nxt = None if (r0 + rq) % seq_len == 0 else us[q + 1][0:1]
        store(slice(r0 + rq - 1, r0 + rq), tap_sum(u[rq - 2:rq - 1], u[rq - 1:rq], nxt))
        for r in range(seq_len, rq, seq_len):
            store(slice(r0 + r, r0 + r + 1), tap_sum(None, u[r:r + 1], u[r + 1:r + 2]))
            store(slice(r0 + r - 1, r0 + r), tap_sum(u[r - 2:r - 1], u[r - 1:r], None))


def _hy_in(a, w, b, cw, cb, layer, seq_len, tm, tn=512, ct=256, row_blocks=4):
    m, k = a.shape
    n = w.shape[2]
    assert tm % seq_len == 0
    taps = cw.shape[1]
    return pl.pallas_call(
        functools.partial(_hy_in_kernel, seq_len=seq_len, row_blocks=row_blocks),
        grid=(n // tn, m // tm),
        in_specs=[
            pl.BlockSpec((tm, k), lambda j, i: (i, 0)),
            _w_spec(k, tn, layer),
            _vec_spec(tn, layer),
            pl.BlockSpec((None, taps, tn), lambda j, i: (layer, 0, j)),
            _vec_spec(tn, layer),
        ],
        out_specs=pl.BlockSpec((tn // ct, tm, ct), lambda j, i: (j, i, 0)),
        out_shape=jax.ShapeDtypeStruct((n // ct, m, ct), F32),
        scratch_shapes=[pltpu.VMEM((k, tn), BF16)],
        compiler_params=_params(2),
        name="hy_in",
    )(a, w, b.reshape(b.shape[0], 1, n), cw, cb.reshape(cb.shape[0], 1, n))


def _swiglu_kernel(a_ref, w1_ref, w3_ref, o_ref, w1b_ref, w3b_ref):
    _cast_weights_once([(w1_ref, w1b_ref), (w3_ref, w3b_ref)])
    a = a_ref[...]
    o_ref[...] = (jax.nn.silu(_dot(a, w1b_ref[...])) * _dot(a, w3b_ref[...])).astype(BF16)


def _swiglu_up(a, w1, w3, layer, tm=1024, tn=512):
    m, k = a.shape
    n = w1.shape[2]
    return pl.pallas_call(
        _swiglu_kernel,
        grid=(n // tn, m // tm),
        in_specs=[pl.BlockSpec((tm, k), lambda j, i: (i, 0)),
                  _w_spec(k, tn, layer), _w_spec(k, tn, layer)],
        out_specs=pl.BlockSpec((tm, tn), lambda j, i: (i, j)),
        out_shape=jax.ShapeDtypeStruct((m, n), BF16),
        scratch_shapes=[pltpu.VMEM((k, tn), BF16), pltpu.VMEM((k, tn), BF16)],
        compiler_params=_params(2),
        name="swiglu_up",
    )(a, w1, w3)


def _resid_norm_kernel(*refs, layer, gate_idx, has_bias, final, shift_idx, scale_idx,
                       row_blocks, k_chunk):
    it = iter(refs)
    a_ref, w_hbm_ref = next(it), next(it)
    b_ref = next(it) if has_bias else None
    x_ref, mod_ref = next(it), next(it)
    nmod_ref = None if final else next(it)
    gain_ref = next(it)
    xo_ref = None if final else next(it)
    h_ref = next(it)
    w_ref, w_sem = next(it), next(it)
    tm = x_ref.shape[0]
    rb = tm // row_blocks

    @pl.when(pl.program_id(0) == 0)
    def _():
        copy = pltpu.make_async_copy(w_hbm_ref.at[layer], w_ref, w_sem)
        copy.start()
        copy.wait()

    def finish(rows, y):
        if has_bias:
            y = y + b_ref[...]
        xn = x_ref[rows, :] + mod_ref[gate_idx:gate_idx + 1, :] * y
        if not final:
            xo_ref[rows, :] = xn
        hn = xn * lax.rsqrt(jnp.mean(xn * xn, axis=-1, keepdims=True) + EPS)
        hn = hn * gain_ref[...]
        if not final:
            hn = hn * (1.0 + nmod_ref[scale_idx:scale_idx + 1, :]) \
                + nmod_ref[shift_idx:shift_idx + 1, :]
        h_ref[rows, :] = hn.astype(h_ref.dtype)

    kk = w_ref.shape[0]
    for r in range(row_blocks):
        rows = slice(r * rb, (r + 1) * rb)
        y = None
        for k0 in range(0, kk, k_chunk):
            if len(a_ref.shape) == 3:
                cw = a_ref.shape[2]
                a = jnp.concatenate([a_ref[c, rows, :]
                                     for c in range(k0 // cw, (k0 + k_chunk) // cw)], axis=1)
            else:
                a = a_ref[rows, k0:k0 + k_chunk]
            part = _dot(a, w_ref[k0:k0 + k_chunk, :])
            y = part if y is None else y + part
        finish(rows, y)


def _resid_norm_proj(a, w, bias, layer, x, mod, gate_idx, seq_len, gain, gain_layer,
                     next_mod, shift_idx, scale_idx, tm, row_blocks=2):
    _, kk, d = w.shape
    k_chunk = next(c for c in range(min(kk, K_CHUNK_MAX), 0, -V7X_MXU_K) if kk % c == 0)
    final = next_mod is None
    seq = _seq_of_tile(tm, seq_len, mod.shape[0])
    if a.ndim == 3:
        n_slab, m, cw = a.shape
        assert n_slab * cw == kk
        a_spec = pl.BlockSpec((n_slab, tm, cw), lambda i: (0, i, 0))
    else:
        m = a.shape[0]
        a_spec = pl.BlockSpec((tm, kk), lambda i: (i, 0))
    in_specs = [a_spec, pl.BlockSpec(memory_space=pl.ANY)]
    args = [a, w]
    if bias is not None:
        in_specs.append(pl.BlockSpec((None, 1, d), lambda i: (layer, 0, 0)))
        args.append(bias.reshape(bias.shape[0], 1, d))
    mod_spec = pl.BlockSpec((None, 6, d), lambda i: (seq(i), 0, 0))
    in_specs += [pl.BlockSpec((tm, d), lambda i: (i, 0)), mod_spec]
    args += [x, mod]
    if not final:
        in_specs.append(mod_spec)
        args.append(next_mod)
    if gain.ndim == 1:
        gain = gain[None]
    in_specs.append(pl.BlockSpec((None, 1, d), lambda i: (gain_layer, 0, 0)))
    args.append(gain.reshape(gain.shape[0], 1, d))
    row_spec = pl.BlockSpec((tm, d), lambda i: (i, 0))
    if final:
        out_specs = [row_spec]
        out_shape = [jax.ShapeDtypeStruct((m, d), F32)]
    else:
        out_specs = [row_spec, row_spec]
        out_shape = [jax.ShapeDtypeStruct((m, d), F32), jax.ShapeDtypeStruct((m, d), BF16)]
    outs = pl.pallas_call(
        functools.partial(_resid_norm_kernel, layer=layer, gate_idx=gate_idx,
                          has_bias=bias is not None, final=final, shift_idx=shift_idx,
                          scale_idx=scale_idx, row_blocks=row_blocks, k_chunk=k_chunk),
        grid=(m // tm,),
        in_specs=in_specs,
        out_specs=out_specs,
        out_shape=out_shape,
        scratch_shapes=[pltpu.VMEM((kk, d), BF16), pltpu.SemaphoreType.DMA(())],
        compiler_params=_params(1),
        name="resid_norm_proj",
    )(*args)
    return outs[0] if final else (outs[0], outs[1])


def _odd_dft_mats(p):
    f = np.arange(p, dtype=np.int64)
    m = np.arange(2 * p, dtype=np.int64)
    phase = ((2 * f[:, None] + 1) * m[None, :]) % (4 * p)
    ang = np.pi * phase.astype(np.float64) / (2 * p)
    fwd = np.concatenate([np.cos(ang), -np.sin(ang)], axis=0)
    fwd_lo = fwd[:, :p]
    fwd_hi = fwd[:, p:].copy()
    fwd_hi[:, 0] = 0.0
    t = np.arange(p, dtype=np.int64)
    phase_i = (t[:, None] * (2 * f[None, :] + 1)) % (4 * p)
    ang_i = np.pi * phase_i.astype(np.float64) / (2 * p)
    inv = np.concatenate([np.cos(ang_i), -np.sin(ang_i)], axis=1) / p
    as_bf16 = lambda a: jnp.asarray(a, F32).astype(BF16)
    return as_bf16(fwd_lo), as_bf16(fwd_hi), as_bf16(inv)


def _filter_mlp_kernel(w1_ref, b1_ref, fr_ref, w2_ref, b2_ref, o_ref, *, seq_len):
    rows = 2 * seq_len
    r = lax.broadcasted_iota(jnp.int32, (rows, LANES), 0)
    lane = lax.broadcasted_iota(jnp.int32, (rows, LANES), 1)
    t = jnp.abs(r - seq_len).astype(F32) / seq_len
    band = jnp.where(lane <= HY_BANDS, lane, lane - HY_BANDS).astype(F32)
    ang = 2.0 * math.pi * t * band
    feat = jnp.where(lane == 0, t,
                     jnp.where(lane <= HY_BANDS, jnp.cos(ang),
                               jnp.where(lane <= 2 * HY_BANDS, jnp.sin(ang), 0.0)))
    z = jnp.dot(feat, w1_ref[...], precision=HIGHEST, preferred_element_type=F32)
    z = jnp.sin(fr_ref[0:1, :] * (z + b1_ref[...]))
    z = jnp.dot(z, w2_ref[...], precision=HIGHEST, preferred_element_type=F32)
    o_ref[...] = jnp.sin(fr_ref[1:2, :] * (z + b2_ref[...]))


def _filter_time_kernel(z_ref, wf_ref, wb_ref, dl_ref, o_ref, *, seq_len):
    z = z_ref[...].astype(BF16)
    ff = _dot(z, wf_ref[...].astype(BF16))
    fb = _dot(z, wb_ref[...].astype(BF16))
    n = lax.broadcasted_iota(jnp.int32, ff.shape, 0) - seq_len
    t = jnp.abs(n).astype(F32) / seq_len
    window = jnp.exp(-t * dl_ref[...])
    kk = jnp.where(n >= 0, ff, fb) * window
    o_ref[...] = jnp.where(n == -seq_len, 0.0, kk)


def _filter_spec_kernel(ka_ref, kb_ref, flo_ref, fhi_ref, o_ref):
    o_ref[...] = (_dot(flo_ref[...], ka_ref[...].astype(BF16))
                  - _dot(fhi_ref[...], kb_ref[...].astype(BF16)))


def _hyena_filter_spectra(seq_len, p, w1, b1, freq, w2, b2, w3, fwd_lo, fwd_hi, ct=512):
    hidden = w1.shape[1]
    n_order = 2
    d = w3.shape[1] // (2 * n_order)
    nb = seq_len // p
    nd = 2 * nb - 1
    rows = 2 * seq_len
    w1p = jnp.pad(w1, ((0, LANES - w1.shape[0]), (0, 0)))
    full = lambda shape: pl.BlockSpec(shape, lambda *_: (0,) * len(shape))
    z = pl.pallas_call(
        functools.partial(_filter_mlp_kernel, seq_len=seq_len),
        grid=(1,),
        in_specs=[full((LANES, hidden)), full((1, hidden)), full((2, hidden)),
                  full((hidden, hidden)), full((1, hidden))],
        out_specs=full((rows, hidden)),
        out_shape=jax.ShapeDtypeStruct((rows, hidden), F32),
        compiler_params=_params(1),
        name="filter_mlp",
    )(w1p, b1.reshape(1, hidden), freq, w2, b2.reshape(1, hidden))

    min_decay = abs(math.log(HY_TARGET) / HY_DECAY_PCT_LONG)
    max_decay = abs(math.log(HY_TARGET) / HY_DECAY_PCT_SHORT)
    deltas = jnp.linspace(min_decay, max_decay, d, dtype=F32).reshape(1, d)
    nct = d // ct
    kk = pl.pallas_call(
        functools.partial(_filter_time_kernel, seq_len=seq_len),
        grid=(n_order, nct),
        in_specs=[
            pl.BlockSpec((rows, hidden), lambda o, j: (0, 0)),
            pl.BlockSpec((hidden, ct), lambda o, j: (0, (2 * o) * nct + j)),
            pl.BlockSpec((hidden, ct), lambda o, j: (0, (2 * o + 1) * nct + j)),
            pl.BlockSpec((1, ct), lambda o, j: (0, j)),
        ],
        out_specs=pl.BlockSpec((None, rows, ct), lambda o, j: (o, 0, j)),
        out_shape=jax.ShapeDtypeStruct((n_order, rows, d), F32),
        compiler_params=_params(2),
        name="filter_time",
    )(z, w3, w3, deltas)

    cs = min(d, 2 * ct)
    return pl.pallas_call(
        _filter_spec_kernel,
        grid=(n_order, d // cs, nd),
        in_specs=[
            pl.BlockSpec((None, p, cs), lambda o, j, e: (o, e + 1, j)),
            pl.BlockSpec((None, p, cs), lambda o, j, e: (o, e, j)),
            pl.BlockSpec((2 * p, p), lambda o, j, e: (0, 0)),
            pl.BlockSpec((2 * p, p), lambda o, j, e: (0, 0)),
        ],
        out_specs=pl.BlockSpec((None, None, 2 * p, cs), lambda o, j, e: (o, e, 0, j)),
        out_shape=jax.ShapeDtypeStruct((n_order, nd, 2 * p, d), F32),
        compiler_params=_params(3),
        name="filter_spec",
    )(kk, kk, fwd_lo, fwd_hi)


def _long_conv_kernel(z_ref, gate_ref, k_ref, bias_ref, fwd_ref, inv_ref, o_ref,
                      zs_ref, ys_ref, *, p, nb, n_seq):
    fwd = fwd_ref[...]
    inv = inv_ref[...]
    seq_len = p * nb
    for s in range(n_seq):
        base = s * seq_len
        for b in range(nb):
            zs_ref[b] = _dot(fwd, z_ref[base + b * p:base + (b + 1) * p, :].astype(BF16))
        for r in range(0, p, MAC_ROWS):
            re = slice(r, r + MAC_ROWS)
            im = slice(p + r, p + r + MAC_ROWS)
            kr = [k_ref[e, re, :] for e in range(2 * nb - 1)]
            ki = [k_ref[e, im, :] for e in range(2 * nb - 1)]
            yr = [None] * nb
            yi = [None] * nb
            for b in range(nb):
                zr = zs_ref[b, re, :]
                zi = zs_ref[b, im, :]
                for a in range(nb):
                    e = a - b + nb - 1
                    tr = kr[e] * zr - ki[e] * zi
                    ti = kr[e] * zi + ki[e] * zr
                    yr[a] = tr if yr[a] is None else yr[a] + tr
                    yi[a] = ti if yi[a] is None else yi[a] + ti
            for a in range(nb):
                ys_ref[a, re, :] = yr[a]
                ys_ref[a, im, :] = yi[a]
        for a in range(nb):
            y = _dot(inv, ys_ref[a].astype(BF16))
            rows = slice(base + a * p, base + (a + 1) * p)
            za = z_ref[rows, :].astype(F32)
            o_ref[rows, :] = (gate_ref[rows, :] * (y + bias_ref[...] * za)).astype(o_ref.dtype)


def _long_conv(z, z_slab, gate, gate_slab, spectra, order, bias, layer, fwd_lo, inv, seq_len, p,
               out_dtype, seqs_per_step=1):
    _, m, ct = z.shape
    d = spectra.shape[-1]
    nb = seq_len // p
    nd = 2 * nb - 1
    rows = seqs_per_step * seq_len
    return pl.pallas_call(
        functools.partial(_long_conv_kernel, p=p, nb=nb, n_seq=seqs_per_step),
        grid=(d // ct, m // rows),
        in_specs=[
            pl.BlockSpec((None, rows, ct), lambda j, b: (z_slab + j, b, 0)),
            pl.BlockSpec((None, rows, ct), lambda j, b: (gate_slab + j, b, 0)),
            pl.BlockSpec((None, nd, 2 * p, ct), lambda j, b: (order, 0, 0, j)),
            pl.BlockSpec((None, None, 1, ct), lambda j, b: (layer, order, 0, j)),
            pl.BlockSpec((2 * p, p), lambda j, b: (0, 0)),
            pl.BlockSpec((p, 2 * p), lambda j, b: (0, 0)),
        ],
        out_specs=pl.BlockSpec((None, rows, ct), lambda j, b: (j, b, 0)),
        out_shape=jax.ShapeDtypeStruct((d // ct, m, ct), out_dtype),
        scratch_shapes=[pltpu.VMEM((nb, 2 * p, ct), F32), pltpu.VMEM((nb, 2 * p, ct), F32)],
        compiler_params=_params(2),
        name="long_conv",
    )(z, gate, spectra, bias.reshape(bias.shape[0], bias.shape[1], 1, d), fwd_lo, inv)


def _retention_kernel(*refs, n_chunks, n_seq, has_s0, want_state):
    it = iter(refs)
    dec_ref, q_ref, k_ref, v_ref, sg_ref, gn_ref = (next(it) for _ in range(6))
    s0_ref = next(it) if has_s0 else None
    o_ref = next(it)
    sfin_ref = next(it) if want_state else None
    acc_ref, s_ref, dmat_ref, xi_ref, zeta_ref = (next(it) for _ in range(5))

    c_len = RET_CHUNK
    dk = q_ref.shape[1]

    log_g = -jnp.exp(dec_ref[...])
    ii = lax.broadcasted_iota(jnp.int32, (c_len, c_len), 0)
    jj = lax.broadcasted_iota(jnp.int32, (c_len, c_len), 1)
    idx = lax.broadcasted_iota(jnp.int32, (c_len, 1), 0).astype(F32)
    g_chunk = []
    for direction in range(2):
        lg = log_g[direction]
        diff = (ii - jj) if direction == 0 else (jj - ii)
        dmat_ref[direction] = jnp.where(
            diff >= 0, jnp.exp(lg * jnp.maximum(diff, 0).astype(F32)), 0.0)
        if direction == 0:
            xi = jnp.exp(lg * (idx + 1.0))
            zeta = jnp.exp(lg * (c_len - 1.0 - idx))
        else:
            xi = jnp.exp(lg * (c_len - idx))
            zeta = jnp.exp(lg * idx)
        xi_ref[direction] = jnp.broadcast_to(xi, (c_len, dk))
        zeta_ref[direction] = jnp.broadcast_to(zeta, (c_len, dk))
        g_chunk.append(jnp.exp(lg * c_len))

    def init_state(s):
        for direction in range(2):
            if has_s0:
                s_ref[direction] = s0_ref[s, direction].astype(F32)
            else:
                s_ref[direction] = jnp.zeros(s_ref.shape[1:], F32)

    def write_state(s):
        if want_state:
            for direction in range(2):
                sfin_ref[s, direction] = s_ref[direction].astype(sfin_ref.dtype)

    def chunk(c):
        return pl.ds(pl.multiple_of(c * c_len, c_len), c_len)

    def scan_chunk(c, direction):
        rows = chunk(c)
        qc = q_ref[rows, :]
        kc = k_ref[rows, :]
        vc = v_ref[rows, :]
        s = s_ref[direction]
        scores = lax.dot_general(qc.astype(BF16), kc.astype(BF16), (((1,), (1,)), ((), ())),
                                 preferred_element_type=F32) * dmat_ref[direction]
        inner = _dot(scores.astype(BF16), vc)
        cross = _dot((qc * xi_ref[direction]).astype(BF16), s.astype(BF16))
        upd = lax.dot_general((kc * zeta_ref[direction]).astype(BF16), vc,
                              (((0,), (0,)), ((), ())), preferred_element_type=F32)
        s_ref[direction] = g_chunk[direction] * s + upd
        return inner + cross

    def norm_gate_store(rows, o):
        mu = jnp.mean(o, axis=-1, keepdims=True)
        var = jnp.mean(jnp.square(o - mu), axis=-1, keepdims=True)
        o = (o - mu) * lax.rsqrt(var + EPS)
        o = o * gn_ref[...]
        o_ref[rows, :] = (o * sg_ref[rows, :]).astype(o_ref.dtype)

    if n_chunks == 1:
        for s in range(n_seq):
            init_state(s)
            norm_gate_store(chunk(s), scan_chunk(s, 0) + scan_chunk(s, 1))
            write_state(s)
    else:
        assert n_chunks % 2 == 0 and n_seq == 1
        init_state(0)

        def first_visit(t, carry):
            acc_ref[chunk(t), :] = scan_chunk(t, 0)
            acc_ref[chunk(n_chunks - 1 - t), :] = scan_chunk(n_chunks - 1 - t, 1)
            return carry

        def second_visit(t, carry):
            cf, cb = chunk(t), chunk(n_chunks - 1 - t)
            norm_gate_store(cf, acc_ref[cf, :] + scan_chunk(t, 0))
            norm_gate_store(cb, acc_ref[cb, :] + scan_chunk(n_chunks - 1 - t, 1))
            return carry

        lax.fori_loop(0, n_chunks // 2, first_visit, 0)
        lax.fori_loop(n_chunks // 2, n_chunks, second_visit, 0)
        write_state(0)


def _retention(qk, v, sg, decay, gn, layer, seq_len, s0, want_state, state_dtype):
    nh, m, dv = v.shape
    assert nh == RET_HEADS and qk.shape[0] == 2 * nh
    dk = qk.shape[2]
    n_seq = m // seq_len
    has_s0 = s0 is not None
    c_len = RET_CHUNK
    n_chunks = seq_len // c_len
    seqs = 2 if n_chunks == 1 and n_seq % 2 == 0 else 1
    rows = seqs * seq_len

    in_specs = [
        pl.BlockSpec((None, 2, None, 1, 1), lambda b, h: (layer, 0, h, 0, 0)),
        pl.BlockSpec((None, rows, dk), lambda b, h: (h, b, 0)),
        pl.BlockSpec((None, rows, dk), lambda b, h: (nh + h, b, 0)),
        pl.BlockSpec((None, rows, dv), lambda b, h: (h, b, 0)),
        pl.BlockSpec((None, rows, dv), lambda b, h: (h, b, 0)),
        pl.BlockSpec((None, 1, dv), lambda b, h: (layer, 0, h)),
    ]
    args = [decay.reshape(decay.shape[0], 2, nh, 1, 1), qk, qk, v, sg,
            gn.reshape(gn.shape[0], 1, nh * dv)]
    state_spec = pl.BlockSpec((seqs, None, 2, None, dk, dv),
                              lambda b, h: (b, layer, 0, h, 0, 0))
    if has_s0:
        in_specs.append(state_spec)
        args.append(s0)
    out_specs = [pl.BlockSpec((None, rows, dv), lambda b, h: (h, b, 0))]
    out_shape = [jax.ShapeDtypeStruct((nh, m, dv), BF16)]
    if want_state:
        out_specs.append(state_spec)
        out_shape.append(jax.ShapeDtypeStruct((n_seq, decay.shape[0], 2, nh, dk, dv),
                                              state_dtype))
    outs = pl.pallas_call(
        functools.partial(_retention_kernel, n_chunks=n_chunks, n_seq=seqs, has_s0=has_s0,
                          want_state=want_state),
        grid=(n_seq // seqs, nh),
        in_specs=in_specs,
        out_specs=out_specs,
        out_shape=out_shape,
        scratch_shapes=[pltpu.VMEM((rows, dv), F32), pltpu.VMEM((2, dk, dv), F32),
                        pltpu.VMEM((2, c_len, c_len), F32), pltpu.VMEM((2, c_len, dk), F32),
                        pltpu.VMEM((2, c_len, dk), F32)],
        compiler_params=_params(2),
        name="retention",
    )(*args)
    return (outs[0], outs[1]) if want_state else (outs[0], None)


def _rope_tables(seq_len, dk):
    rows = seq_len // GRID_W
    pos_row = jnp.repeat(jnp.arange(rows, dtype=F32), GRID_W)
    pos_col = jnp.tile(jnp.arange(GRID_W, dtype=F32), rows)
    n = dk // 4
    inv = jnp.exp(-math.log(ROPE_BASE) * jnp.arange(n, dtype=F32) / n)
    ang_r = pos_row[:, None] * inv[None, :]
    ang_c = pos_col[:, None] * inv[None, :]
    cos = jnp.concatenate([jnp.cos(ang_r)] * 2 + [jnp.cos(ang_c)] * 2, axis=-1)
    sin = jnp.concatenate([-jnp.sin(ang_r), jnp.sin(ang_r), -jnp.sin(ang_c), jnp.sin(ang_c)],
                          axis=-1)
    return cos, sin


def _trunk(x, mod, seq_len, conv_block, tm, rope_tabs, s0, want_state, p):
    m, d = x.shape
    mod0, mod1 = mod[0], mod[1]
    nh = RET_HEADS
    dv = p['ret_gn'].shape[1] // nh
    dk = (p['ret_w_in'].shape[2] - 2 * nh * dv) // (2 * nh)

    h = _norm_mod(x, p['norm_mix'], 0, mod0, seq_len, 0, 1)
    u = _hy_in(h, p['hy_w_in'], p['hy_b_in'], p['hy_conv_w'], p['hy_conv_b'], 0, seq_len,
               tm=max(tm, seq_len))
    fwd_lo, fwd_hi, inv = _odd_dft_mats(conv_block)
    spectra = _hyena_filter_spectra(
        seq_len, conv_block, p['hy_filt_w1'][0], p['hy_filt_b1'][0], p['hy_filt_freq'][0],
        p['hy_filt_w2'][0], p['hy_filt_b2'][0], p['hy_filt_w3'][0], fwd_lo, fwd_hi)
    slabs = d // u.shape[2]
    seqs = max(1, 1024 // seq_len)
    z = _long_conv(u, 0, u, slabs, spectra, 0, p['hy_bias_d'], 0, fwd_lo, inv, seq_len,
                   conv_block, F32, seqs)
    z = _long_conv(z, 0, u, 2 * slabs, spectra, 1, p['hy_bias_d'], 0, fwd_lo, inv, seq_len,
                   conv_block, BF16, seqs)
    tr, tr_ffn = tm // 2, tm // 4
    x, h = _resid_norm_proj(z, p['hy_w_out_bf16'], p['hy_b_out'], 0, x, mod0, 2, seq_len,
                            p['norm_ffn'], 0, mod0, 3, 4, tr)
    a = _swiglu_up(h, p['ffn_w1'], p['ffn_w3'], 0, tm=tm)
    x, h = _resid_norm_proj(a, p['ffn_w2_bf16'], None, 0, x, mod0, 5, seq_len,
                            p['norm_mix'], 1, mod1, 0, 1, tr_ffn)

    qk = _ret_qk_proj(h, p['ret_w_in'], 0, nh, dk, seq_len, rope_tabs, tm=tm)
    v = _proj(h, p['ret_w_in'], 0, 2 * nh * dk, nh * dv, dv, BF16, tm=tm)
    sg = _proj(h, p['ret_w_in'], 0, 2 * nh * dk + nh * dv, nh * dv, dv, F32, silu=True, tm=tm)
    og, s_fin = _retention(qk, v, sg, p['ret_decay'], p['ret_gn'], 0, seq_len, s0,
                           want_state, x.dtype)
    x, h = _resid_norm_proj(og, p['ret_w_out_bf16'], None, 0, x, mod1, 2, seq_len,
                            p['norm_ffn'], 1, mod1, 3, 4, tr)
    a = _swiglu_up(h, p['ffn_w1'], p['ffn_w3'], 1, tm=tm)
    y = _resid_norm_proj(a, p['ffn_w2_bf16'], None, 1, x, mod1, 5, seq_len,
                         p['norm_final'], 0, None, 0, 0, tr_ffn)
    return y, s_fin


def kernel(x_prompt, x_sample, state_ret, c, c_ctx, w_ada, b_ada, norm_mix, norm_ffn, norm_final,
           ffn_w1, ffn_w3, ffn_w2, hy_w_in, hy_b_in, hy_conv_w, hy_conv_b, hy_filt_w1, hy_filt_b1,
           hy_filt_freq, hy_filt_w2, hy_filt_b2, hy_filt_w3, hy_bias_d, hy_w_out, hy_b_out,
           ret_w_in, ret_decay, ret_gn, ret_w_out):
    p = dict(norm_mix=norm_mix, norm_ffn=norm_ffn, norm_final=norm_final,
             ffn_w1=ffn_w1, ffn_w3=ffn_w3, ffn_w2=ffn_w2, hy_w_in=hy_w_in, hy_b_in=hy_b_in,
             hy_conv_w=hy_conv_w, hy_conv_b=hy_conv_b, hy_filt_w1=hy_filt_w1,
             hy_filt_b1=hy_filt_b1, hy_filt_freq=hy_filt_freq, hy_filt_w2=hy_filt_w2,
             hy_filt_b2=hy_filt_b2, hy_filt_w3=hy_filt_w3, hy_bias_d=hy_bias_d,
             hy_w_out=hy_w_out, hy_b_out=hy_b_out, ret_w_in=ret_w_in, ret_decay=ret_decay,
             ret_gn=ret_gn, ret_w_out=ret_w_out)
    for name in ('hy_w_out', 'ret_w_out', 'ffn_w2'):
        p[name + '_bf16'] = p[name].astype(BF16)
    n_ctx, ctx_len, d = x_prompt.shape
    n_dec, dec_len, _ = x_sample.shape
    depth = w_ada.shape[0]

    cond_rows = 16
    cond = jnp.concatenate(
        [c_ctx[None, :], c, jnp.zeros((cond_rows - 1 - n_dec, d), c.dtype)], axis=0)
    mod = _ada_mod(cond, w_ada, b_ada).reshape(depth, cond_rows, 6, d)
    mod_ctx = mod[:, 0:1]
    mod_dec = mod[:, 1:1 + n_dec]

    y_prompt, ctx_state = _trunk(
        x_prompt.reshape(n_ctx * ctx_len, d), mod_ctx, ctx_len, conv_block=ctx_len, tm=1024,
        rope_tabs=None, s0=None, want_state=True, p=p)
    dk = state_ret.shape[-2]
    y_sample, _ = _trunk(
        x_sample.reshape(n_dec * dec_len, d), mod_dec, dec_len, conv_block=512, tm=1024,
        rope_tabs=_rope_tables(dec_len, dk), s0=state_ret, want_state=False, p=p)

    return (y_prompt.reshape(x_prompt.shape), y_sample.reshape(x_sample.shape), ctx_state)
```

```python
import functools
import math

import numpy as np
import jax
import jax.numpy as jnp
from jax import lax
from jax.experimental import pallas as pl
from jax.experimental.pallas import tpu as pltpu

F32 = jnp.float32
BF16 = jnp.bfloat16

EPS = 1e-6
GRID_W = 64
HY_BANDS = 16
HY_DECAY_PCT_SHORT = 0.3
HY_DECAY_PCT_LONG = 1.5
HY_TARGET = 1e-2
RET_HEADS = 8
ROPE_BASE = 10000.0
RET_CHUNK = 256
MAC_ROWS = 8
ROW_BLOCKS = 2
V7X_MXU_K = 256
K_CHUNK_MAX = 3072

V7X_VMEM_BYTES = 64 * 1024 * 1024
VMEM_LIMIT = V7X_VMEM_BYTES - 8 * 1024 * 1024
LANES = 128
HIGHEST = lax.Precision.HIGHEST


def _params(n_axes):
    return pltpu.CompilerParams(
        dimension_semantics=("arbitrary",) * n_axes, vmem_limit_bytes=VMEM_LIMIT)


def _dot(a, b):
    return jnp.dot(a, b, preferred_element_type=F32)


def _ada_kernel(c_ref, w_ref, b_ref, o_ref):
    s = jax.nn.silu(c_ref[...]).astype(BF16)
    o_ref[...] = _dot(s, w_ref[...].astype(BF16)) + b_ref[...]


def _ada_mod(cond, w_ada, b_ada, tn=1536):
    depth, d, n = w_ada.shape
    rows = cond.shape[0]
    return pl.pallas_call(
        _ada_kernel,
        grid=(depth, n // tn),
        in_specs=[
            pl.BlockSpec((rows, d), lambda l, j: (0, 0)),
            pl.BlockSpec((None, d, tn), lambda l, j: (l, 0, j)),
            pl.BlockSpec((None, 1, tn), lambda l, j: (l, 0, j)),
        ],
        out_specs=pl.BlockSpec((None, rows, tn), lambda l, j: (l, 0, j)),
        out_shape=jax.ShapeDtypeStruct((depth, rows, n), F32),
        compiler_params=_params(2),
        name="ada_mod",
    )(cond, w_ada, b_ada.reshape(depth, 1, n))


def _seq_of_tile(tm, seq_len, n_mod):
    if n_mod == 1:
        return lambda i: 0
    assert seq_len % tm == 0
    return lambda i: (i * tm) // seq_len


def _norm_mod_kernel(x_ref, g_ref, mod_ref, o_ref, *, shift_idx, scale_idx):
    x = x_ref[...]
    y = x * lax.rsqrt(jnp.mean(x * x, axis=-1, keepdims=True) + EPS)
    y = y * g_ref[...]
    scale = mod_ref[scale_idx:scale_idx + 1, :]
    shift = mod_ref[shift_idx:shift_idx + 1, :]
    o_ref[...] = (y * (1.0 + scale) + shift).astype(o_ref.dtype)


def _norm_mod(x, g, layer, mod, seq_len, shift_idx, scale_idx, tm=512):
    m, d = x.shape
    seq = _seq_of_tile(tm, seq_len, mod.shape[0])
    return pl.pallas_call(
        functools.partial(_norm_mod_kernel, shift_idx=shift_idx, scale_idx=scale_idx),
        grid=(m // tm,),
        in_specs=[
            pl.BlockSpec((tm, d), lambda i: (i, 0)),
            pl.BlockSpec((None, 1, d), lambda i: (layer, 0, 0)),
            pl.BlockSpec((None, 6, d), lambda i: (seq(i), 0, 0)),
        ],
        out_specs=pl.BlockSpec((tm, d), lambda i: (i, 0)),
        out_shape=jax.ShapeDtypeStruct((m, d), BF16),
        compiler_params=_params(1),
        name="norm_mod",
    )(x, g.reshape(g.shape[0], 1, d), mod)


def _cast_weights_once(pairs):
    @pl.when(pl.program_id(1) == 0)
    def _():
        for w_ref, wb_ref in pairs:
            wb_ref[...] = w_ref[...].astype(BF16)


def _w_spec(k, tn, layer, col0=0):
    assert col0 % tn == 0
    return pl.BlockSpec((None, k, tn), lambda j, i: (layer, 0, col0 // tn + j))


def _vec_spec(tn, layer, col0=0):
    return pl.BlockSpec((None, 1, tn), lambda j, i: (layer, 0, col0 // tn + j))


def _rotate_pairs(x, half):
    parts = [pltpu.roll(x[:, g * half:(g + 1) * half], half // 2, 1)
             for g in range(x.shape[1] // half)]
    return jnp.concatenate(parts, axis=1)


def _ret_qk_kernel(*refs, dk, k_tile0, k_scale, use_rope):
    if use_rope:
        a_ref, w_ref, cos_ref, sin_ref, o_ref, wb_ref = refs
    else:
        a_ref, w_ref, o_ref, wb_ref = refs
    _cast_weights_once([(w_ref, wb_ref)])
    scale = jnp.where(pl.program_id(0) >= k_tile0, k_scale, 1.0).astype(F32)
    rb = a_ref.shape[0] // ROW_BLOCKS
    for r in range(ROW_BLOCKS):
        rows = slice(r * rb, (r + 1) * rb)
        y = _dot(a_ref[rows, :], wb_ref[...]) * scale
        for hb in range(y.shape[1] // dk):
            x = y[:, hb * dk:(hb + 1) * dk]
            if use_rope:
                x = x * cos_ref[rows, :] + _rotate_pairs(x, dk // 2) * sin_ref[rows, :]
            o_ref[hb, rows, :] = x


def _ret_qk_proj(a, w, layer, nh, dk, seq_len, rope_tabs, tm=1024, tn=1024):
    m, k = a.shape
    n = 2 * nh * dk
    use_rope = rope_tabs is not None
    in_specs = [pl.BlockSpec((tm, k), lambda j, i: (i, 0)), _w_spec(k, tn, layer)]
    args = [a, w]
    if use_rope:
        assert seq_len % tm == 0
        tiles_per_seq = seq_len // tm
        tab_spec = pl.BlockSpec((tm, dk), lambda j, i: (i % tiles_per_seq, 0))
        in_specs += [tab_spec, tab_spec]
        args += list(rope_tabs)
    return pl.pallas_call(
        functools.partial(_ret_qk_kernel, dk=dk, k_tile0=(nh * dk) // tn, k_scale=dk ** -0.5,
                          use_rope=use_rope),
        grid=(n // tn, m // tm),
        in_specs=in_specs,
        out_specs=pl.BlockSpec((tn // dk, tm, dk), lambda j, i: (j, i, 0)),
        out_shape=jax.ShapeDtypeStruct((n // dk, m, dk), F32),
        scratch_shapes=[pltpu.VMEM((k, tn), BF16)],
        compiler_params=_params(2),
        name="ret_qk_proj",
    )(*args)


def _proj_kernel(a_ref, w_ref, o_ref, wb_ref, *, silu):
    _cast_weights_once([(w_ref, wb_ref)])
    oc = o_ref.shape[2]
    rb = a_ref.shape[0] // ROW_BLOCKS
    for r in range(ROW_BLOCKS):
        rows = slice(r * rb, (r + 1) * rb)
        y = _dot(a_ref[rows, :], wb_ref[...])
        if silu:
            y = jax.nn.silu(y)
        for c in range(o_ref.shape[0]):
            o_ref[c, rows, :] = y[:, c * oc:(c + 1) * oc].astype(o_ref.dtype)


def _proj(a, w, layer, col0, n, oc, out_dtype, silu=False, tm=1024, tn=1024):
    m, k = a.shape
    return pl.pallas_call(
        functools.partial(_proj_kernel, silu=silu),
        grid=(n // tn, m // tm),
        in_specs=[pl.BlockSpec((tm, k), lambda j, i: (i, 0)), _w_spec(k, tn, layer, col0)],
        out_specs=pl.BlockSpec((tn // oc, tm, oc), lambda j, i: (j, i, 0)),
        out_shape=jax.ShapeDtypeStruct((n // oc, m, oc), out_dtype),
        scratch_shapes=[pltpu.VMEM((k, tn), BF16)],
        compiler_params=_params(2),
        name="proj",
    )(a, w)


def _hy_in_kernel(a_ref, w_ref, b_ref, cw_ref, cb_ref, o_ref, wb_ref, *, seq_len, row_blocks):
    _cast_weights_once([(w_ref, wb_ref)])
    n_slab, tm, ct = o_ref.shape
    rq = tm // row_blocks
    wb = wb_ref[...]
    bias = b_ref[...]
    us = [_dot(a_ref[q * rq:(q + 1) * rq, :], wb) + bias for q in range(row_blocks)]
    w0, w1, w2, cb = cw_ref[0:1, :], cw_ref[1:2, :], cw_ref[2:3, :], cb_ref[...]

    def tap_sum(prev, cur, nxt):
        acc = cur * w1
        if prev is not None:
            acc = prev * w0 + acc
        if nxt is not None:
            acc = acc + nxt * w2
        return acc + cb

    def store(rows, val):
        for s in range(n_slab):
            o_ref[s, rows, :] = val[:, s * ct:(s + 1) * ct]

    for q, u in enumerate(us):
        r0 = q * rq
        store(slice(r0, r0 + rq),
              tap_sum(pltpu.roll(u, 1, 0), u, pltpu.roll(u, rq - 1, 0)))
        prev = None if r0 % seq_len == 0 else us[q - 1][rq - 1:rq]
        store(slice(r0, r0 + 1), tap_sum(prev, u[0:1], u[1:2]))
        nxt = None if (r0 + rq) % seq_len == 0 else us[q + 1][0:1]
        store(slice(r0 + rq - 1, r0 + rq), tap_sum(u[rq - 2:rq - 1], u[rq - 1:rq], nxt))
        for r in range(seq_len, rq, seq_len):
            store(slice(r0 + r, r0 + r + 1), tap_sum(None, u[r:r + 1], u[r + 1:r + 2]))
            store(slice(r0 + r - 1, r0 + r), tap_sum(u[r - 2:r - 1], u[r - 1:r], None))


def _hy_in(a, w, b, cw, cb, layer, seq_len, tm, tn=512, ct=256, row_blocks=4):
    m, k = a.shape
    n = w.shape[2]
    assert tm % seq_len == 0
    taps = cw.shape[1]
    return pl.pallas_call(
        functools.partial(_hy_in_kernel, seq_len=seq_len, row_blocks=row_blocks),
        grid=(n // tn, m // tm),
        in_specs=[
            pl.BlockSpec((tm, k), lambda j, i: (i, 0)),
            _w_spec(k, tn, layer),
            _vec_spec(tn, layer),
            pl.BlockSpec((None, taps, tn), lambda j, i: (layer, 0, j)),
            _vec_spec(tn, layer),
        ],
        out_specs=pl.BlockSpec((tn // ct, tm, ct), lambda j, i: (j, i, 0)),
        out_shape=jax.ShapeDtypeStruct((n // ct, m, ct), F32),
        scratch_shapes=[pltpu.VMEM((k, tn), BF16)],
        compiler_params=_params(2),
        name="hy_in",
    )(a, w, b.reshape(b.shape[0], 1, n), cw, cb.reshape(cb.shape[0], 1, n))


def _swiglu_kernel(a_ref, w1_ref, w3_ref, o_ref, w1b_ref, w3b_ref):
    _cast_weights_once([(w1_ref, w1b_ref), (w3_ref, w3b_ref)])
    a = a_ref[...]
    o_ref[...] = (jax.nn.silu(_dot(a, w1b_ref[...])) * _dot(a, w3b_ref[...])).astype(BF16)


def _swiglu_up(a, w1, w3, layer, tm=1024, tn=512):
    m, k = a.shape
    n = w1.shape[2]
    return pl.pallas_call(
        _swiglu_kernel,
        grid=(n // tn, m // tm),
        in_specs=[pl.BlockSpec((tm, k), lambda j, i: (i, 0)),
                  _w_spec(k, tn, layer), _w_spec(k, tn, layer)],
        out_specs=pl.BlockSpec((tm, tn), lambda j, i: (i, j)),
        out_shape=jax.ShapeDtypeStruct((m, n), BF16),
        scratch_shapes=[pltpu.VMEM((k, tn), BF16), pltpu.VMEM((k, tn), BF16)],
        compiler_params=_params(2),
        name="swiglu_up",
    )(a, w1, w3)


def _resid_norm_kernel(*refs, layer, gate_idx, has_bias, final, shift_idx, scale_idx,
                       row_blocks, k_chunk):
    it = iter(refs)
    a_ref, w_hbm_ref = next(it), next(it)
    b_ref = next(it) if has_bias else None
    x_ref, mod_ref = next(it), next(it)
    nmod_ref = None if final else next(it)
    gain_ref = next(it)
    xo_ref = None if final else next(it)
    h_ref = next(it)
    w_ref, w_sem = next(it), next(it)
    tm = x_ref.shape[0]
    rb = tm // row_blocks

    @pl.when(pl.program_id(0) == 0)
    def _():
        copy = pltpu.make_async_copy(w_hbm_ref.at[layer], w_ref, w_sem)
        copy.start()
        copy.wait()

    def finish(rows, y):
        if has_bias:
            y = y + b_ref[...]
        xn = x_ref[rows, :] + mod_ref[gate_idx:gate_idx + 1, :] * y
        if not final:
            xo_ref[rows, :] = xn
        hn = xn * lax.rsqrt(jnp.mean(xn * xn, axis=-1, keepdims=True) + EPS)
        hn = hn * gain_ref[...]
        if not final:
            hn = hn * (1.0 + nmod_ref[scale_idx:scale_idx + 1, :]) \
                + nmod_ref[shift_idx:shift_idx + 1, :]
        h_ref[rows, :] = hn.astype(h_ref.dtype)

    kk = w_ref.shape[0]
    for r in range(row_blocks):
        rows = slice(r * rb, (r + 1) * rb)
        y = None
        for k0 in range(0, kk, k_chunk):
            if len(a_ref.shape) == 3:
                cw = a_ref.shape[2]
                a = jnp.concatenate([a_ref[c, rows, :]
                                     for c in range(k0 // cw, (k0 + k_chunk) // cw)], axis=1)
            else:
                a = a_ref[rows, k0:k0 + k_chunk]
            part = _dot(a, w_ref[k0:k0 + k_chunk, :])
            y = part if y is None else y + part
        finish(rows, y)


def _resid_norm_proj(a, w, bias, layer, x, mod, gate_idx, seq_len, gain, gain_layer,
                     next_mod, shift_idx, scale_idx, tm, row_blocks=2):
    _, kk, d = w.shape
    k_chunk = next(c for c in range(min(kk, K_CHUNK_MAX), 0, -V7X_MXU_K) if kk % c == 0)
    final = next_mod is None
    seq = _seq_of_tile(tm, seq_len, mod.shape[0])
    if a.ndim == 3:
        n_slab, m, cw = a.shape
        assert n_slab * cw == kk
        a_spec = pl.BlockSpec((n_slab, tm, cw), lambda i: (0, i, 0))
    else:
        m = a.shape[0]
        a_spec = pl.BlockSpec((tm, kk), lambda i: (i, 0))
    in_specs = [a_spec, pl.BlockSpec(memory_space=pl.ANY)]
    args = [a, w]
    if bias is not None:
        in_specs.append(pl.BlockSpec((None, 1, d), lambda i: (layer, 0, 0)))
        args.append(bias.reshape(bias.shape[0], 1, d))
    mod_spec = pl.BlockSpec((None, 6, d), lambda i: (seq(i), 0, 0))
    in_specs += [pl.BlockSpec((tm, d), lambda i: (i, 0)), mod_spec]
    args += [x, mod]
    if not final:
        in_specs.append(mod_spec)
        args.append(next_mod)
    if gain.ndim == 1:
        gain = gain[None]
    in_specs.append(pl.BlockSpec((None, 1, d), lambda i: (gain_layer, 0, 0)))
    args.append(gain.reshape(gain.shape[0], 1, d))
    row_spec = pl.BlockSpec((tm, d), lambda i: (i, 0))
    if final:
        out_specs = [row_spec]
        out_shape = [jax.ShapeDtypeStruct((m, d), F32)]
    else:
        out_specs = [row_spec, row_spec]
        out_shape = [jax.ShapeDtypeStruct((m, d), F32), jax.ShapeDtypeStruct((m, d), BF16)]
    outs = pl.pallas_call(
        functools.partial(_resid_norm_kernel, layer=layer, gate_idx=gate_idx,
                          has_bias=bias is not None, final=final, shift_idx=shift_idx,
                          scale_idx=scale_idx, row_blocks=row_blocks, k_chunk=k_chunk),
        grid=(m // tm,),
        in_specs=in_specs,
        out_specs=out_specs,
        out_shape=out_shape,
        scratch_shapes=[pltpu.VMEM((kk, d), BF16), pltpu.SemaphoreType.DMA(())],
        compiler_params=_params(1),
        name="resid_norm_proj",
    )(*args)
    return outs[0] if final else (outs[0], outs[1])


def _odd_dft_mats(p):
    f = np.arange(p, dtype=np.int64)
    m = np.arange(2 * p, dtype=np.int64)
    phase = ((2 * f[:, None] + 1) * m[None, :]) % (4 * p)
    ang = np.pi * phase.astype(np.float64) / (2 * p)
    fwd = np.concatenate([np.cos(ang), -np.sin(ang)], axis=0)
    fwd_lo = fwd[:, :p]
    fwd_hi = fwd[:, p:].copy()
    fwd_hi[:, 0] = 0.0
    t = np.arange(p, dtype=np.int64)
    phase_i = (t[:, None] * (2 * f[None, :] + 1)) % (4 * p)
    ang_i = np.pi * phase_i.astype(np.float64) / (2 * p)
    inv = np.concatenate([np.cos(ang_i), -np.sin(ang_i)], axis=1) / p
    as_bf16 = lambda a: jnp.asarray(a, F32).astype(BF16)
    return as_bf16(fwd_lo), as_bf16(fwd_hi), as_bf16(inv)


def _filter_mlp_kernel(w1_ref, b1_ref, fr_ref, w2_ref, b2_ref, o_ref, *, seq_len):
    rows = 2 * seq_len
    r = lax.broadcasted_iota(jnp.int32, (rows, LANES), 0)
    lane = lax.broadcasted_iota(jnp.int32, (rows, LANES), 1)
    t = jnp.abs(r - seq_len).astype(F32) / seq_len
    band = jnp.where(lane <= HY_BANDS, lane, lane - HY_BANDS).astype(F32)
    ang = 2.0 * math.pi * t * band
    feat = jnp.where(lane == 0, t,
                     jnp.where(lane <= HY_BANDS, jnp.cos(ang),
                               jnp.where(lane <= 2 * HY_BANDS, jnp.sin(ang), 0.0)))
    z = jnp.dot(feat, w1_ref[...], precision=HIGHEST, preferred_element_type=F32)
    z = jnp.sin(fr_ref[0:1, :] * (z + b1_ref[...]))
    z = jnp.dot(z, w2_ref[...], precision=HIGHEST, preferred_element_type=F32)
    o_ref[...] = jnp.sin(fr_ref[1:2, :] * (z + b2_ref[...]))


def _filter_time_kernel(z_ref, wf_ref, wb_ref, dl_ref, o_ref, *, seq_len):
    z = z_ref[...].astype(BF16)
    ff = _dot(z, wf_ref[...].astype(BF16))
    fb = _dot(z, wb_ref[...].astype(BF16))
    n = lax.broadcasted_iota(jnp.int32, ff.shape, 0) - seq_len
    t = jnp.abs(n).astype(F32) / seq_len
    window = jnp.exp(-t * dl_ref[...])
    kk = jnp.where(n >= 0, ff, fb) * window
    o_ref[...] = jnp.where(n == -seq_len, 0.0, kk)


def _filter_spec_kernel(ka_ref, kb_ref, flo_ref, fhi_ref, o_ref):
    o_ref[...] = (_dot(flo_ref[...], ka_ref[...].astype(BF16))
                  - _dot(fhi_ref[...], kb_ref[...].astype(BF16)))


def _hyena_filter_spectra(seq_len, p, w1, b1, freq, w2, b2, w3, fwd_lo, fwd_hi, ct=512):
    hidden = w1.shape[1]
    n_order = 2
    d = w3.shape[1] // (2 * n_order)
    nb = seq_len // p
    nd = 2 * nb - 1
    rows = 2 * seq_len
    w1p = jnp.pad(w1, ((0, LANES - w1.shape[0]), (0, 0)))
    full = lambda shape: pl.BlockSpec(shape, lambda *_: (0,) * len(shape))
    z = pl.pallas_call(
        functools.partial(_filter_mlp_kernel, seq_len=seq_len),
        grid=(1,),
        in_specs=[full((LANES, hidden)), full((1, hidden)), full((2, hidden)),
                  full((hidden, hidden)), full((1, hidden))],
        out_specs=full((rows, hidden)),
        out_shape=jax.ShapeDtypeStruct((rows, hidden), F32),
        compiler_params=_params(1),
        name="filter_mlp",
    )(w1p, b1.reshape(1, hidden), freq, w2, b2.reshape(1, hidden))

    min_decay = abs(math.log(HY_TARGET) / HY_DECAY_PCT_LONG)
    max_decay = abs(math.log(HY_TARGET) / HY_DECAY_PCT_SHORT)
    deltas = jnp.linspace(min_decay, max_decay, d, dtype=F32).reshape(1, d)
    nct = d // ct
    kk = pl.pallas_call(
        functools.partial(_filter_time_kernel, seq_len=seq_len),
        grid=(n_order, nct),
        in_specs=[
            pl.BlockSpec((rows, hidden), lambda o, j: (0, 0)),
            pl.BlockSpec((hidden, ct), lambda o, j: (0, (2 * o) * nct + j)),
            pl.BlockSpec((hidden, ct), lambda o, j: (0, (2 * o + 1) * nct + j)),
            pl.BlockSpec((1, ct), lambda o, j: (0, j)),
        ],
        out_specs=pl.BlockSpec((None, rows, ct), lambda o, j: (o, 0, j)),
        out_shape=jax.ShapeDtypeStruct((n_order, rows, d), F32),
        compiler_params=_params(2),
        name="filter_time",
    )(z, w3, w3, deltas)

    cs = min(d, 2 * ct)
    return pl.pallas_call(
        _filter_spec_kernel,
        grid=(n_order, d // cs, nd),
        in_specs=[
            pl.BlockSpec((None, p, cs), lambda o, j, e: (o, e + 1, j)),
            pl.BlockSpec((None, p, cs), lambda o, j, e: (o, e, j)),
            pl.BlockSpec((2 * p, p), lambda o, j, e: (0, 0)),
            pl.BlockSpec((2 * p, p), lambda o, j, e: (0, 0)),
        ],
        out_specs=pl.BlockSpec((None, None, 2 * p, cs), lambda o, j, e: (o, e, 0, j)),
        out_shape=jax.ShapeDtypeStruct((n_order, nd, 2 * p, d), F32),
        compiler_params=_params(3),
        name="filter_spec",
    )(kk, kk, fwd_lo, fwd_hi)


def _long_conv_kernel(z_ref, gate_ref, k_ref, bias_ref, fwd_ref, inv_ref, o_ref,
                      zs_ref, ys_ref, *, p, nb, n_seq):
    fwd = fwd_ref[...]
    inv = inv_ref[...]
    seq_len = p * nb
    for s in range(n_seq):
        base = s * seq_len
        for b in range(nb):
            zs_ref[b] = _dot(fwd, z_ref[base + b * p:base + (b + 1) * p, :].astype(BF16))
        for r in range(0, p, MAC_ROWS):
            re = slice(r, r + MAC_ROWS)
            im = slice(p + r, p + r + MAC_ROWS)
            kr = [k_ref[e, re, :] for e in range(2 * nb - 1)]
            ki = [k_ref[e, im, :] for e in range(2 * nb - 1)]
            yr = [None] * nb
            yi = [None] * nb
            for b in range(nb):
                zr = zs_ref[b, re, :]
                zi = zs_ref[b, im, :]
                for a in range(nb):
                    e = a - b + nb - 1
                    tr = kr[e] * zr - ki[e] * zi
                    ti = kr[e] * zi + ki[e] * zr
                    yr[a] = tr if yr[a] is None else yr[a] + tr
                    yi[a] = ti if yi[a] is None else yi[a] + ti
            for a in range(nb):
                ys_ref[a, re, :] = yr[a]
                ys_ref[a, im, :] = yi[a]
        for a in range(nb):
            y = _dot(inv, ys_ref[a].astype(BF16))
            rows = slice(base + a * p, base + (a + 1) * p)
            za = z_ref[rows, :].astype(F32)
            o_ref[rows, :] = (gate_ref[rows, :] * (y + bias_ref[...] * za)).astype(o_ref.dtype)


def _long_conv(z, z_slab, gate, gate_slab, spectra, order, bias, layer, fwd_lo, inv, seq_len, p,
               out_dtype, seqs_per_step=1):
    _, m, ct = z.shape
    d = spectra.shape[-1]
    nb = seq_len // p
    nd = 2 * nb - 1
    rows = seqs_per_step * seq_len
    return pl.pallas_call(
        functools.partial(_long_conv_kernel, p=p, nb=nb, n_seq=seqs_per_step),
        grid=(d // ct, m // rows),
        in_specs=[
            pl.BlockSpec((None, rows, ct), lambda j, b: (z_slab + j, b, 0)),
            pl.BlockSpec((None, rows, ct), lambda j, b: (gate_slab + j, b, 0)),
            pl.BlockSpec((None, nd, 2 * p, ct), lambda j, b: (order, 0, 0, j)),
            pl.BlockSpec((None, None, 1, ct), lambda j, b: (layer, order, 0, j)),
            pl.BlockSpec((2 * p, p), lambda j, b: (0, 0)),
            pl.BlockSpec((p, 2 * p), lambda j, b: (0, 0)),
        ],
        out_specs=pl.BlockSpec((None, rows, ct), lambda j, b: (j, b, 0)),
        out_shape=jax.ShapeDtypeStruct((d // ct, m, ct), out_dtype),
        scratch_shapes=[pltpu.VMEM((nb, 2 * p, ct), F32), pltpu.VMEM((nb, 2 * p, ct), F32)],
        compiler_params=_params(2),
        name="long_conv",
    )(z, gate, spectra, bias.reshape(bias.shape[0], bias.shape[1], 1, d), fwd_lo, inv)


def _retention_kernel(*refs, n_chunks, n_seq, has_s0, want_state):
    it = iter(refs)
    dec_ref, q_ref, k_ref, v_ref, sg_ref, gn_ref = (next(it) for _ in range(6))
    s0_ref = next(it) if has_s0 else None
    o_ref = next(it)
    sfin_ref = next(it) if want_state else None
    acc_ref, s_ref, dmat_ref, xi_ref, zeta_ref = (next(it) for _ in range(5))

    c_len = RET_CHUNK
    dk = q_ref.shape[1]

    log_g = -jnp.exp(dec_ref[...])
    g_chunk = [jnp.exp(log_g[direction] * c_len) for direction in range(2)]

    @pl.when(pl.program_id(1) == 0)
    def _():
        ii = lax.broadcasted_iota(jnp.int32, (c_len, c_len), 0)
        jj = lax.broadcasted_iota(jnp.int32, (c_len, c_len), 1)
        idx = lax.broadcasted_iota(jnp.int32, (c_len, 1), 0).astype(F32)
        for direction in range(2):
            lg = log_g[direction]
            diff = (ii - jj) if direction == 0 else (jj - ii)
            dmat_ref[direction] = jnp.where(
                diff >= 0, jnp.exp(lg * jnp.maximum(diff, 0).astype(F32)), 0.0)
            if direction == 0:
                xi = jnp.exp(lg * (idx + 1.0))
                zeta = jnp.exp(lg * (c_len - 1.0 - idx))
            else:
                xi = jnp.exp(lg * (c_len - idx))
                zeta = jnp.exp(lg * idx)
            xi_ref[direction] = jnp.broadcast_to(xi, (c_len, dk))
            zeta_ref[direction] = jnp.broadcast_to(zeta, (c_len, dk))

    def init_state(s):
        for direction in range(2):
            if has_s0:
                s_ref[direction] = s0_ref[s, direction].astype(F32)
            else:
                s_ref[direction] = jnp.zeros(s_ref.shape[1:], F32)

    def write_state(s):
        if want_state:
            for direction in range(2):
                sfin_ref[s, direction] = s_ref[direction].astype(sfin_ref.dtype)

    def chunk(c):
        return pl.ds(pl.multiple_of(c * c_len, c_len), c_len)

    def scan_chunk(c, direction):
        rows = chunk(c)
        qc = q_ref[rows, :]
        kc = k_ref[rows, :]
        vc = v_ref[rows, :]
        s = s_ref[direction]
        scores = lax.dot_general(qc.astype(BF16), kc.astype(BF16), (((1,), (1,)), ((), ())),
                                 preferred_element_type=F32) * dmat_ref[direction]
        inner = _dot(scores.astype(BF16), vc)
        cross = _dot((qc * xi_ref[direction]).astype(BF16), s.astype(BF16))
        upd = lax.dot_general((kc * zeta_ref[direction]).astype(BF16), vc,
                              (((0,), (0,)), ((), ())), preferred_element_type=F32)
        s_ref[direction] = g_chunk[direction] * s + upd
        return inner + cross

    def norm_gate_store(rows, o):
        mu = jnp.mean(o, axis=-1, keepdims=True)
        var = jnp.mean(jnp.square(o - mu), axis=-1, keepdims=True)
        o = (o - mu) * lax.rsqrt(var + EPS)
        o = o * gn_ref[...]
        o_ref[rows, :] = (o * sg_ref[rows, :]).astype(o_ref.dtype)

    if n_chunks == 1:
        for s in range(n_seq):
            init_state(s)
            norm_gate_store(chunk(s), scan_chunk(s, 0) + scan_chunk(s, 1))
            write_state(s)
    else:
        assert n_chunks % 2 == 0 and n_seq == 1
        init_state(0)

        def first_visit(t, carry):
            acc_ref[chunk(t), :] = scan_chunk(t, 0)
            acc_ref[chunk(n_chunks - 1 - t), :] = scan_chunk(n_chunks - 1 - t, 1)
            return carry

        def second_visit(t, carry):
            cf, cb = chunk(t), chunk(n_chunks - 1 - t)
            norm_gate_store(cf, acc_ref[cf, :] + scan_chunk(t, 0))
            norm_gate_store(cb, acc_ref[cb, :] + scan_chunk(n_chunks - 1 - t, 1))
            return carry

        lax.fori_loop(0, n_chunks // 2, first_visit, 0)
        lax.fori_loop(n_chunks // 2, n_chunks, second_visit, 0)
        write_state(0)


def _retention(qk, v, sg, decay, gn, layer, seq_len, s0, want_state, state_dtype):
    nh, m, dv = v.shape
    assert nh == RET_HEADS and qk.shape[0] == 2 * nh
    dk = qk.shape[2]
    n_seq = m // seq_len
    has_s0 = s0 is not None
    c_len = RET_CHUNK
    n_chunks = seq_len // c_len
    seqs = 2 if n_chunks == 1 and n_seq % 2 == 0 else 1
    rows = seqs * seq_len

    in_specs = [
        pl.BlockSpec((None, 2, None, 1, 1), lambda h, b: (layer, 0, h, 0, 0)),
        pl.BlockSpec((None, rows, dk), lambda h, b: (h, b, 0)),
        pl.BlockSpec((None, rows, dk), lambda h, b: (nh + h, b, 0)),
        pl.BlockSpec((None, rows, dv), lambda h, b: (h, b, 0)),
        pl.BlockSpec((None, rows, dv), lambda h, b: (h, b, 0)),
        pl.BlockSpec((None, 1, dv), lambda h, b: (layer, 0, h)),
    ]
    args = [decay.reshape(decay.shape[0], 2, nh, 1, 1), qk, qk, v, sg,
            gn.reshape(gn.shape[0], 1, nh * dv)]
    state_spec = pl.BlockSpec((seqs, None, 2, None, dk, dv),
                              lambda h, b: (b, layer, 0, h, 0, 0))
    if has_s0:
        in_specs.append(state_spec)
        args.append(s0)
    out_specs = [pl.BlockSpec((None, rows, dv), lambda h, b: (h, b, 0))]
    out_shape = [jax.ShapeDtypeStruct((nh, m, dv), BF16)]
    if want_state:
        out_specs.append(state_spec)
        out_shape.append(jax.ShapeDtypeStruct((n_seq, decay.shape[0], 2, nh, dk, dv),
                                              state_dtype))
    outs = pl.pallas_call(
        functools.partial(_retention_kernel, n_chunks=n_chunks, n_seq=seqs, has_s0=has_s0,
                          want_state=want_state),
        grid=(nh, n_seq // seqs),
        in_specs=in_specs,
        out_specs=out_specs,
        out_shape=out_shape,
        scratch_shapes=[pltpu.VMEM((rows, dv), F32), pltpu.VMEM((2, dk, dv), F32),
                        pltpu.VMEM((2, c_len, c_len), F32), pltpu.VMEM((2, c_len, dk), F32),
                        pltpu.VMEM((2, c_len, dk), F32)],
        compiler_params=_params(2),
        name="retention",
    )(*args)
    return (outs[0], outs[1]) if want_state else (outs[0], None)


def _rope_tables(seq_len, dk):
    rows = seq_len // GRID_W
    pos_row = jnp.repeat(jnp.arange(rows, dtype=F32), GRID_W)
    pos_col = jnp.tile(jnp.arange(GRID_W, dtype=F32), rows)
    n = dk // 4
    inv = jnp.exp(-math.log(ROPE_BASE) * jnp.arange(n, dtype=F32) / n)
    ang_r = pos_row[:, None] * inv[None, :]
    ang_c = pos_col[:, None] * inv[None, :]
    cos = jnp.concatenate([jnp.cos(ang_r)] * 2 + [jnp.cos(ang_c)] * 2, axis=-1)
    sin = jnp.concatenate([-jnp.sin(ang_r), jnp.sin(ang_r), -jnp.sin(ang_c), jnp.sin(ang_c)],
                          axis=-1)
    return cos, sin


def _trunk(x, mod, seq_len, conv_block, tm, rope_tabs, s0, want_state, p):
    m, d = x.shape
    mod0, mod1 = mod[0], mod[1]
    nh = RET_HEADS
    dv = p['ret_gn'].shape[1] // nh
    dk = (p['ret_w_in'].shape[2] - 2 * nh * dv) // (2 * nh)

    h = _norm_mod(x, p['norm_mix'], 0, mod0, seq_len, 0, 1)
    u = _hy_in(h, p['hy_w_in'], p['hy_b_in'], p['hy_conv_w'], p['hy_conv_b'], 0, seq_len,
               tm=max(tm, seq_len))
    fwd_lo, fwd_hi, inv = _odd_dft_mats(conv_block)
    spectra = _hyena_filter_spectra(
        seq_len, conv_block, p['hy_filt_w1'][0], p['hy_filt_b1'][0], p['hy_filt_freq'][0],
        p['hy_filt_w2'][0], p['hy_filt_b2'][0], p['hy_filt_w3'][0], fwd_lo, fwd_hi)
    slabs = d // u.shape[2]
    seqs = max(1, 1024 // seq_len)
    z = _long_conv(u, 0, u, slabs, spectra, 0, p['hy_bias_d'], 0, fwd_lo, inv, seq_len,
                   conv_block, F32, seqs)
    z = _long_conv(z, 0, u, 2 * slabs, spectra, 1, p['hy_bias_d'], 0, fwd_lo, inv, seq_len,
                   conv_block, BF16, seqs)
    tr, tr_ffn = tm // 2, tm // 4
    x, h = _resid_norm_proj(z, p['hy_w_out_bf16'], p['hy_b_out'], 0, x, mod0, 2, seq_len,
                            p['norm_ffn'], 0, mod0, 3, 4, tr)
    a = _swiglu_up(h, p['ffn_w1'], p['ffn_w3'], 0, tm=tm)
    x, h = _resid_norm_proj(a, p['ffn_w2_bf16'], None, 0, x, mod0, 5, seq_len,
                            p['norm_mix'], 1, mod1, 0, 1, tr_ffn)

    qk = _ret_qk_proj(h, p['ret_w_in'], 0, nh, dk, seq_len, rope_tabs, tm=tm)
    v = _proj(h, p['ret_w_in'], 0, 2 * nh * dk, nh * dv, dv, BF16, tm=tm)
    sg = _proj(h, p['ret_w_in'], 0, 2 * nh * dk + nh * dv, nh * dv, dv, F32, silu=True, tm=tm)
    og, s_fin = _retention(qk, v, sg, p['ret_decay'], p['ret_gn'], 0, seq_len, s0,
                           want_state, x.dtype)
    x, h = _resid_norm_proj(og, p['ret_w_out_bf16'], None, 0, x, mod1, 2, seq_len,
                            p['norm_ffn'], 1, mod1, 3, 4, tr)
    a = _swiglu_up(h, p['ffn_w1'], p['ffn_w3'], 1, tm=tm)
    y = _resid_norm_proj(a, p['ffn_w2_bf16'], None, 1, x, mod1, 5, seq_len,
                         p['norm_final'], 0, None, 0, 0, tr_ffn)
    return y, s_fin


def kernel(x_prompt, x_sample, state_ret, c, c_ctx, w_ada, b_ada, norm_mix, norm_ffn, norm_final,
           ffn_w1, ffn_w3, ffn_w2, hy_w_in, hy_b_in, hy_conv_w, hy_conv_b, hy_filt_w1, hy_filt_b1,
           hy_filt_freq, hy_filt_w2, hy_filt_b2, hy_filt_w3, hy_bias_d, hy_w_out, hy_b_out,
           ret_w_in, ret_decay, ret_gn, ret_w_out):
    p = dict(norm_mix=norm_mix, norm_ffn=norm_ffn, norm_final=norm_final,
             ffn_w1=ffn_w1, ffn_w3=ffn_w3, ffn_w2=ffn_w2, hy_w_in=hy_w_in, hy_b_in=hy_b_in,
             hy_conv_w=hy_conv_w, hy_conv_b=hy_conv_b, hy_filt_w1=hy_filt_w1,
             hy_filt_b1=hy_filt_b1, hy_filt_freq=hy_filt_freq, hy_filt_w2=hy_filt_w2,
             hy_filt_b2=hy_filt_b2, hy_filt_w3=hy_filt_w3, hy_bias_d=hy_bias_d,
             hy_w_out=hy_w_out, hy_b_out=hy_b_out, ret_w_in=ret_w_in, ret_decay=ret_decay,
             ret_gn=ret_gn, ret_w_out=ret_w_out)
    for name in ('hy_w_out', 'ret_w_out', 'ffn_w2'):
        p[name + '_bf16'] = p[name].astype(BF16)
    n_ctx, ctx_len, d = x_prompt.shape
    n_dec, dec_len, _ = x_sample.shape
    depth = w_ada.shape[0]

    cond_rows = 16
    cond = jnp.concatenate(
        [c_ctx[None, :], c, jnp.zeros((cond_rows - 1 - n_dec, d), c.dtype)], axis=0)
    mod = _ada_mod(cond, w_ada, b_ada).reshape(depth, cond_rows, 6, d)
    mod_ctx = mod[:, 0:1]
    mod_dec = mod[:, 1:1 + n_dec]

    y_prompt, ctx_state = _trunk(
        x_prompt.reshape(n_ctx * ctx_len, d), mod_ctx, ctx_len, conv_block=ctx_len, tm=1024,
        rope_tabs=None, s0=None, want_state=True, p=p)
    dk = state_ret.shape[-2]
    y_sample, _ = _trunk(
        x_sample.reshape(n_dec * dec_len, d), mod_dec, dec_len, conv_block=512, tm=1024,
        rope_tabs=_rope_tables(dec_len, dk), s0=state_ret, want_state=False, p=p)

    return (y_prompt.reshape(x_prompt.shape), y_sample.reshape(x_sample.shape), ctx_state)
```

```python
import functools
import math

import numpy as np
import jax
import jax.numpy as jnp
from jax import lax
from jax.experimental import pallas as pl
from jax.experimental.pallas import tpu as pltpu

F32 = jnp.float32
BF16 = jnp.bfloat16

EPS = 1e-6
GRID_W = 64
HY_BANDS = 16
HY_DECAY_PCT_SHORT = 0.3
HY_DECAY_PCT_LONG = 1.5
HY_TARGET = 1e-2
RET_HEADS = 8
ROPE_BASE = 10000.0
RET_CHUNK = 256
MAC_ROWS = 8
ROW_BLOCKS = 2
V7X_MXU_K = 256
K_CHUNK_MAX = 3072

V7X_VMEM_BYTES = 64 * 1024 * 1024
VMEM_LIMIT = V7X_VMEM_BYTES - 8 * 1024 * 1024
LANES = 128
HIGHEST = lax.Precision.HIGHEST


def _params(n_axes):
    return pltpu.CompilerParams(
        dimension_semantics=("arbitrary",) * n_axes, vmem_limit_bytes=VMEM_LIMIT)


def _dot(a, b):
    return jnp.dot(a, b, preferred_element_type=F32)


def _ada_kernel(c_ref, w_ref, b_ref, o_ref):
    s = jax.nn.silu(c_ref[...]).astype(BF16)
    o_ref[...] = _dot(s, w_ref[...].astype(BF16)) + b_ref[...]


def _ada_mod(cond, w_ada, b_ada, tn=1536):
    depth, d, n = w_ada.shape
    rows = cond.shape[0]
    return pl.pallas_call(
        _ada_kernel,
        grid=(depth, n // tn),
        in_specs=[
            pl.BlockSpec((rows, d), lambda l, j: (0, 0)),
            pl.BlockSpec((None, d, tn), lambda l, j: (l, 0, j)),
            pl.BlockSpec((None, 1, tn), lambda l, j: (l, 0, j)),
        ],
        out_specs=pl.BlockSpec((None, rows, tn), lambda l, j: (l, 0, j)),
        out_shape=jax.ShapeDtypeStruct((depth, rows, n), F32),
        compiler_params=_params(2),
        name="ada_mod",
    )(cond, w_ada, b_ada.reshape(depth, 1, n))


def _seq_of_tile(tm, seq_len, n_mod):
    if n_mod == 1:
        return lambda i: 0
    assert seq_len % tm == 0
    return lambda i: (i * tm) // seq_len


def _norm_mod_kernel(x_ref, g_ref, mod_ref, o_ref, *, shift_idx, scale_idx):
    x = x_ref[...]
    y = x * lax.rsqrt(jnp.mean(x * x, axis=-1, keepdims=True) + EPS)
    y = y * g_ref[...]
    scale = mod_ref[scale_idx:scale_idx + 1, :]
    shift = mod_ref[shift_idx:shift_idx + 1, :]
    o_ref[...] = (y * (1.0 + scale) + shift).astype(o_ref.dtype)


def _norm_mod(x, g, layer, mod, seq_len, shift_idx, scale_idx, tm=512):
    m, d = x.shape
    seq = _seq_of_tile(tm, seq_len, mod.shape[0])
    return pl.pallas_call(
        functools.partial(_norm_mod_kernel, shift_idx=shift_idx, scale_idx=scale_idx),
        grid=(m // tm,),
        in_specs=[
            pl.BlockSpec((tm, d), lambda i: (i, 0)),
            pl.BlockSpec((None, 1, d), lambda i: (layer, 0, 0)),
            pl.BlockSpec((None, 6, d), lambda i: (seq(i), 0, 0)),
        ],
        out_specs=pl.BlockSpec((tm, d), lambda i: (i, 0)),
        out_shape=jax.ShapeDtypeStruct((m, d), BF16),
        compiler_params=_params(1),
        name="norm_mod",
    )(x, g.reshape(g.shape[0], 1, d), mod)


def _cast_weights_once(pairs):
    @pl.when(pl.program_id(1) == 0)
    def _():
        for w_ref, wb_ref in pairs:
            wb_ref[...] = w_ref[...].astype(BF16)


def _w_spec(k, tn, layer, col0=0):
    assert col0 % tn == 0
    return pl.BlockSpec((None, k, tn), lambda j, i: (layer, 0, col0 // tn + j))


def _vec_spec(tn, layer, col0=0):
    return pl.BlockSpec((None, 1, tn), lambda j, i: (layer, 0, col0 // tn + j))


def _rotate_pairs(x, half):
    parts = [pltpu.roll(x[:, g * half:(g + 1) * half], half // 2, 1)
             for g in range(x.shape[1] // half)]
    return jnp.concatenate(parts, axis=1)


def _ret_qk_kernel(*refs, dk, k_tile0, k_scale, use_rope):
    if use_rope:
        a_ref, w_ref, cos_ref, sin_ref, o_ref, wb_ref = refs
    else:
        a_ref, w_ref, o_ref, wb_ref = refs
    _cast_weights_once([(w_ref, wb_ref)])
    scale = jnp.where(pl.program_id(0) >= k_tile0, k_scale, 1.0).astype(F32)
    rb = a_ref.shape[0] // ROW_BLOCKS
    for r in range(ROW_BLOCKS):
        rows = slice(r * rb, (r + 1) * rb)
        y = _dot(a_ref[rows, :], wb_ref[...]) * scale
        for hb in range(y.shape[1] // dk):
            x = y[:, hb * dk:(hb + 1) * dk]
            if use_rope:
                x = x * cos_ref[rows, :] + _rotate_pairs(x, dk // 2) * sin_ref[rows, :]
            o_ref[hb, rows, :] = x


def _ret_qk_proj(a, w, layer, nh, dk, seq_len, rope_tabs, tm=1024, tn=1024):
    m, k = a.shape
    n = 2 * nh * dk
    use_rope = rope_tabs is not None
    in_specs = [pl.BlockSpec((tm, k), lambda j, i: (i, 0)), _w_spec(k, tn, layer)]
    args = [a, w]
    if use_rope:
        assert seq_len % tm == 0
        tiles_per_seq = seq_len // tm
        tab_spec = pl.BlockSpec((tm, dk), lambda j, i: (i % tiles_per_seq, 0))
        in_specs += [tab_spec, tab_spec]
        args += list(rope_tabs)
    return pl.pallas_call(
        functools.partial(_ret_qk_kernel, dk=dk, k_tile0=(nh * dk) // tn, k_scale=dk ** -0.5,
                          use_rope=use_rope),
        grid=(n // tn, m // tm),
        in_specs=in_specs,
        out_specs=pl.BlockSpec((tn // dk, tm, dk), lambda j, i: (j, i, 0)),
        out_shape=jax.ShapeDtypeStruct((n // dk, m, dk), F32),
        scratch_shapes=[pltpu.VMEM((k, tn), BF16)],
        compiler_params=_params(2),
        name="ret_qk_proj",
    )(*args)


def _proj_kernel(a_ref, w_ref, o_ref, wb_ref, *, silu):
    _cast_weights_once([(w_ref, wb_ref)])
    oc = o_ref.shape[2]
    rb = a_ref.shape[0] // ROW_BLOCKS
    for r in range(ROW_BLOCKS):
        rows = slice(r * rb, (r + 1) * rb)
        y = _dot(a_ref[rows, :], wb_ref[...])
        if silu:
            y = jax.nn.silu(y)
        for c in range(o_ref.shape[0]):
            o_ref[c, rows, :] = y[:, c * oc:(c + 1) * oc].astype(o_ref.dtype)


def _proj(a, w, layer, col0, n, oc, out_dtype, silu=False, tm=1024, tn=1024):
    m, k = a.shape
    return pl.pallas_call(
        functools.partial(_proj_kernel, silu=silu),
        grid=(n // tn, m // tm),
        in_specs=[pl.BlockSpec((tm, k), lambda j, i: (i, 0)), _w_spec(k, tn, layer, col0)],
        out_specs=pl.BlockSpec((tn // oc, tm, oc), lambda j, i: (j, i, 0)),
        out_shape=jax.ShapeDtypeStruct((n // oc, m, oc), out_dtype),
        scratch_shapes=[pltpu.VMEM((k, tn), BF16)],
        compiler_params=_params(2),
        name="proj",
    )(a, w)


def _hy_in_kernel(a_ref, w_ref, b_ref, cw_ref, cb_ref, o_ref, wb_ref, *, seq_len, row_blocks):
    _cast_weights_once([(w_ref, wb_ref)])
    n_slab, tm, ct = o_ref.shape
    rq = tm // row_blocks
    wb = wb_ref[...]
    bias = b_ref[...]
    us = [_dot(a_ref[q * rq:(q + 1) * rq, :], wb) + bias for q in range(row_blocks)]
    w0, w1, w2, cb = cw_ref[0:1, :], cw_ref[1:2, :], cw_ref[2:3, :], cb_ref[...]

    def tap_sum(prev, cur, nxt):
        acc = cur * w1
        if prev is not None:
            acc = prev * w0 + acc
        if nxt is not None:
            acc = acc + nxt * w2
        return acc + cb

    def store(rows, val):
        for s in range(n_slab):
            o_ref[s, rows, :] = val[:, s * ct:(s + 1) * ct]

    for q, u in enumerate(us):
        r0 = q * rq
        store(slice(r0, r0 + rq),
              tap_sum(pltpu.roll(u, 1, 0), u, pltpu.roll(u, rq - 1, 0)))
        prev = None if r0 % seq_len == 0 else us[q - 1][rq - 1:rq]
        store(slice(r0, r0 + 1), tap_sum(prev, u[0:1], u[1:2]))
        nxt = None if (r0 + rq) % seq_len == 0 else us[q + 1][0:1]
        store(slice(r0 + rq - 1, r0 + rq), tap_sum(u[rq - 2:rq - 1], u[rq - 1:rq], nxt))
        for r in range(seq_len, rq, seq_len):
            store(slice(r0 + r, r0 + r + 1), tap_sum(None, u[r:r + 1], u[r + 1:r + 2]))
            store(slice(r0 + r - 1, r0 + r), tap_sum(u[r - 2:r - 1], u[r - 1:r], None))


def _hy_in(a, w, b, cw, cb, layer, seq_len, tm, tn=512, ct=256, row_blocks=4):
    m, k = a.shape
    n = w.shape[2]
    assert tm % seq_len == 0
    taps = cw.shape[1]
    return pl.pallas_call(
        functools.partial(_hy_in_kernel, seq_len=seq_len, row_blocks=row_blocks),
        grid=(n // tn, m // tm),
        in_specs=[
            pl.BlockSpec((tm, k), lambda j, i: (i, 0)),
            _w_spec(k, tn, layer),
            _vec_spec(tn, layer),
            pl.BlockSpec((None, taps, tn), lambda j, i: (layer, 0, j)),
            _vec_spec(tn, layer),
        ],
        out_specs=pl.BlockSpec((tn // ct, tm, ct), lambda j, i: (j, i, 0)),
        out_shape=jax.ShapeDtypeStruct((n // ct, m, ct), F32),
        scratch_shapes=[pltpu.VMEM((k, tn), BF16)],
        compiler_params=_params(2),
        name="hy_in",
    )(a, w, b.reshape(b.shape[0], 1, n), cw, cb.reshape(cb.shape[0], 1, n))


def _swiglu_kernel(a_ref, w1_ref, w3_ref, o_ref, w1b_ref, w3b_ref):
    _cast_weights_once([(w1_ref, w1b_ref), (w3_ref, w3b_ref)])
    rb = a_ref.shape[0] // ROW_BLOCKS
    for r in range(ROW_BLOCKS):
        rows = slice(r * rb, (r + 1) * rb)
        a = a_ref[rows, :]
        o_ref[rows, :] = (jax.nn.silu(_dot(a, w1b_ref[...]))
                          * _dot(a, w3b_ref[...])).astype(BF16)


def _swiglu_up(a, w1, w3, layer, tm=1024, tn=512):
    m, k = a.shape
    n = w1.shape[2]
    return pl.pallas_call(
        _swiglu_kernel,
        grid=(n // tn, m // tm),
        in_specs=[pl.BlockSpec((tm, k), lambda j, i: (i, 0)),
                  _w_spec(k, tn, layer), _w_spec(k, tn, layer)],
        out_specs=pl.BlockSpec((tm, tn), lambda j, i: (i, j)),
        out_shape=jax.ShapeDtypeStruct((m, n), BF16),
        scratch_shapes=[pltpu.VMEM((k, tn), BF16), pltpu.VMEM((k, tn), BF16)],
        compiler_params=_params(2),
        name="swiglu_up",
    )(a, w1, w3)


def _resid_norm_kernel(*refs, layer, gate_idx, has_bias, final, shift_idx, scale_idx,
                       row_blocks, k_chunk):
    it = iter(refs)
    a_ref, w_hbm_ref = next(it), next(it)
    b_ref = next(it) if has_bias else None
    x_ref, mod_ref = next(it), next(it)
    nmod_ref = None if final else next(it)
    gain_ref = next(it)
    xo_ref = None if final else next(it)
    h_ref = next(it)
    w_ref, w_sem = next(it), next(it)
    tm = x_ref.shape[0]
    rb = tm // row_blocks

    @pl.when(pl.program_id(0) == 0)
    def _():
        copy = pltpu.make_async_copy(w_hbm_ref.at[layer], w_ref, w_sem)
        copy.start()
        copy.wait()

    def finish(rows, y):
        if has_bias:
            y = y + b_ref[...]
        xn = x_ref[rows, :] + mod_ref[gate_idx:gate_idx + 1, :] * y
        if not final:
            xo_ref[rows, :] = xn
        hn = xn * lax.rsqrt(jnp.mean(xn * xn, axis=-1, keepdims=True) + EPS)
        hn = hn * gain_ref[...]
        if not final:
            hn = hn * (1.0 + nmod_ref[scale_idx:scale_idx + 1, :]) \
                + nmod_ref[shift_idx:shift_idx + 1, :]
        h_ref[rows, :] = hn.astype(h_ref.dtype)

    kk = w_ref.shape[0]
    for r in range(row_blocks):
        rows = slice(r * rb, (r + 1) * rb)
        y = None
        for k0 in range(0, kk, k_chunk):
            if len(a_ref.shape) == 3:
                cw = a_ref.shape[2]
                a = jnp.concatenate([a_ref[c, rows, :]
                                     for c in range(k0 // cw, (k0 + k_chunk) // cw)], axis=1)
            else:
                a = a_ref[rows, k0:k0 + k_chunk]
            part = _dot(a, w_ref[k0:k0 + k_chunk, :])
            y = part if y is None else y + part
        finish(rows, y)


def _resid_norm_proj(a, w, bias, layer, x, mod, gate_idx, seq_len, gain, gain_layer,
                     next_mod, shift_idx, scale_idx, tm, row_blocks=2):
    _, kk, d = w.shape
    k_chunk = next(c for c in range(min(kk, K_CHUNK_MAX), 0, -V7X_MXU_K) if kk % c == 0)
    final = next_mod is None
    seq = _seq_of_tile(tm, seq_len, mod.shape[0])
    if a.ndim == 3:
        n_slab, m, cw = a.shape
        assert n_slab * cw == kk
        a_spec = pl.BlockSpec((n_slab, tm, cw), lambda i: (0, i, 0))
    else:
        m = a.shape[0]
        a_spec = pl.BlockSpec((tm, kk), lambda i: (i, 0))
    in_specs = [a_spec, pl.BlockSpec(memory_space=pl.ANY)]
    args = [a, w]
    if bias is not None:
        in_specs.append(pl.BlockSpec((None, 1, d), lambda i: (layer, 0, 0)))
        args.append(bias.reshape(bias.shape[0], 1, d))
    mod_spec = pl.BlockSpec((None, 6, d), lambda i: (seq(i), 0, 0))
    in_specs += [pl.BlockSpec((tm, d), lambda i: (i, 0)), mod_spec]
    args += [x, mod]
    if not final:
        in_specs.append(mod_spec)
        args.append(next_mod)
    if gain.ndim == 1:
        gain = gain[None]
    in_specs.append(pl.BlockSpec((None, 1, d), lambda i: (gain_layer, 0, 0)))
    args.append(gain.reshape(gain.shape[0], 1, d))
    row_spec = pl.BlockSpec((tm, d), lambda i: (i, 0))
    if final:
        out_specs = [row_spec]
        out_shape = [jax.ShapeDtypeStruct((m, d), F32)]
    else:
        out_specs = [row_spec, row_spec]
        out_shape = [jax.ShapeDtypeStruct((m, d), F32), jax.ShapeDtypeStruct((m, d), BF16)]
    outs = pl.pallas_call(
        functools.partial(_resid_norm_kernel, layer=layer, gate_idx=gate_idx,
                          has_bias=bias is not None, final=final, shift_idx=shift_idx,
                          scale_idx=scale_idx, row_blocks=row_blocks, k_chunk=k_chunk),
        grid=(m // tm,),
        in_specs=in_specs,
        out_specs=out_specs,
        out_shape=out_shape,
        scratch_shapes=[pltpu.VMEM((kk, d), BF16), pltpu.SemaphoreType.DMA(())],
        compiler_params=_params(1),
        name="resid_norm_proj",
    )(*args)
    return outs[0] if final else (outs[0], outs[1])


def _odd_dft_mats(p):
    f = np.arange(p, dtype=np.int64)
    m = np.arange(2 * p, dtype=np.int64)
    phase = ((2 * f[:, None] + 1) * m[None, :]) % (4 * p)
    ang = np.pi * phase.astype(np.float64) / (2 * p)
    fwd = np.concatenate([np.cos(ang), -np.sin(ang)], axis=0)
    fwd_lo = fwd[:, :p]
    fwd_hi = fwd[:, p:].copy()
    fwd_hi[:, 0] = 0.0
    t = np.arange(p, dtype=np.int64)
    phase_i = (t[:, None] * (2 * f[None, :] + 1)) % (4 * p)
    ang_i = np.pi * phase_i.astype(np.float64) / (2 * p)
    inv = np.concatenate([np.cos(ang_i), -np.sin(ang_i)], axis=1) / p
    as_bf16 = lambda a: jnp.asarray(a, F32).astype(BF16)
    return as_bf16(fwd_lo), as_bf16(fwd_hi), as_bf16(inv)


def _filter_mlp_kernel(w1_ref, b1_ref, fr_ref, w2_ref, b2_ref, o_ref, *, seq_len):
    rows = 2 * seq_len
    r = lax.broadcasted_iota(jnp.int32, (rows, LANES), 0)
    lane = lax.broadcasted_iota(jnp.int32, (rows, LANES), 1)
    t = jnp.abs(r - seq_len).astype(F32) / seq_len
    band = jnp.where(lane <= HY_BANDS, lane, lane - HY_BANDS).astype(F32)
    ang = 2.0 * math.pi * t * band
    feat = jnp.where(lane == 0, t,
                     jnp.where(lane <= HY_BANDS, jnp.cos(ang),
                               jnp.where(lane <= 2 * HY_BANDS, jnp.sin(ang), 0.0)))
    z = jnp.dot(feat, w1_ref[...], precision=HIGHEST, preferred_element_type=F32)
    z = jnp.sin(fr_ref[0:1, :] * (z + b1_ref[...]))
    z = jnp.dot(z, w2_ref[...], precision=HIGHEST, preferred_element_type=F32)
    o_ref[...] = jnp.sin(fr_ref[1:2, :] * (z + b2_ref[...]))


def _filter_time_kernel(z_ref, wf_ref, wb_ref, dl_ref, o_ref, *, seq_len):
    z = z_ref[...].astype(BF16)
    ff = _dot(z, wf_ref[...].astype(BF16))
    fb = _dot(z, wb_ref[...].astype(BF16))
    n = lax.broadcasted_iota(jnp.int32, ff.shape, 0) - seq_len
    t = jnp.abs(n).astype(F32) / seq_len
    window = jnp.exp(-t * dl_ref[...])
    kk = jnp.where(n >= 0, ff, fb) * window
    o_ref[...] = jnp.where(n == -seq_len, 0.0, kk)


def _filter_spec_kernel(ka_ref, kb_ref, flo_ref, fhi_ref, o_ref):
    o_ref[...] = (_dot(flo_ref[...], ka_ref[...].astype(BF16))
                  - _dot(fhi_ref[...], kb_ref[...].astype(BF16)))


def _hyena_filter_spectra(seq_len, p, w1, b1, freq, w2, b2, w3, fwd_lo, fwd_hi, ct=512):
    hidden = w1.shape[1]
    n_order = 2
    d = w3.shape[1] // (2 * n_order)
    nb = seq_len // p
    nd = 2 * nb - 1
    rows = 2 * seq_len
    w1p = jnp.pad(w1, ((0, LANES - w1.shape[0]), (0, 0)))
    full = lambda shape: pl.BlockSpec(shape, lambda *_: (0,) * len(shape))
    z = pl.pallas_call(
        functools.partial(_filter_mlp_kernel, seq_len=seq_len),
        grid=(1,),
        in_specs=[full((LANES, hidden)), full((1, hidden)), full((2, hidden)),
                  full((hidden, hidden)), full((1, hidden))],
        out_specs=full((rows, hidden)),
        out_shape=jax.ShapeDtypeStruct((rows, hidden), F32),
        compiler_params=_params(1),
        name="filter_mlp",
    )(w1p, b1.reshape(1, hidden), freq, w2, b2.reshape(1, hidden))

    min_decay = abs(math.log(HY_TARGET) / HY_DECAY_PCT_LONG)
    max_decay = abs(math.log(HY_TARGET) / HY_DECAY_PCT_SHORT)
    deltas = jnp.linspace(min_decay, max_decay, d, dtype=F32).reshape(1, d)
    nct = d // ct
    kk = pl.pallas_call(
        functools.partial(_filter_time_kernel, seq_len=seq_len),
        grid=(n_order, nct),
        in_specs=[
            pl.BlockSpec((rows, hidden), lambda o, j: (0, 0)),
            pl.BlockSpec((hidden, ct), lambda o, j: (0, (2 * o) * nct + j)),
            pl.BlockSpec((hidden, ct), lambda o, j: (0, (2 * o + 1) * nct + j)),
            pl.BlockSpec((1, ct), lambda o, j: (0, j)),
        ],
        out_specs=pl.BlockSpec((None, rows, ct), lambda o, j: (o, 0, j)),
        out_shape=jax.ShapeDtypeStruct((n_order, rows, d), F32),
        compiler_params=_params(2),
        name="filter_time",
    )(z, w3, w3, deltas)

    cs = min(d, 2 * ct)
    return pl.pallas_call(
        _filter_spec_kernel,
        grid=(n_order, d // cs, nd),
        in_specs=[
            pl.BlockSpec((None, p, cs), lambda o, j, e: (o, e + 1, j)),
            pl.BlockSpec((None, p, cs), lambda o, j, e: (o, e, j)),
            pl.BlockSpec((2 * p, p), lambda o, j, e: (0, 0)),
            pl.BlockSpec((2 * p, p), lambda o, j, e: (0, 0)),
        ],
        out_specs=pl.BlockSpec((None, None, 2 * p, cs), lambda o, j, e: (o, e, 0, j)),
        out_shape=jax.ShapeDtypeStruct((n_order, nd, 2 * p, d), F32),
        compiler_params=_params(3),
        name="filter_spec",
    )(kk, kk, fwd_lo, fwd_hi)


def _long_conv_kernel(z_ref, gate_ref, k_ref, bias_ref, fwd_ref, inv_ref, o_ref,
                      zs_ref, ys_ref, *, p, nb, n_seq):
    fwd = fwd_ref[...]
    inv = inv_ref[...]
    seq_len = p * nb
    for s in range(n_seq):
        base = s * seq_len
        for b in range(nb):
            zs_ref[b] = _dot(fwd, z_ref[base + b * p:base + (b + 1) * p, :].astype(BF16))
        for r in range(0, p, MAC_ROWS):
            re = slice(r, r + MAC_ROWS)
            im = slice(p + r, p + r + MAC_ROWS)
            kr = [k_ref[e, re, :] for e in range(2 * nb - 1)]
            ki = [k_ref[e, im, :] for e in range(2 * nb - 1)]
            yr = [None] * nb
            yi = [None] * nb
            for b in range(nb):
                zr = zs_ref[b, re, :]
                zi = zs_ref[b, im, :]
                for a in range(nb):
                    e = a - b + nb - 1
                    tr = kr[e] * zr - ki[e] * zi
                    ti = kr[e] * zi + ki[e] * zr
                    yr[a] = tr if yr[a] is None else yr[a] + tr
                    yi[a] = ti if yi[a] is None else yi[a] + ti
            for a in range(nb):
                ys_ref[a, re, :] = yr[a]
                ys_ref[a, im, :] = yi[a]
        for a in range(nb):
            y = _dot(inv, ys_ref[a].astype(BF16))
            rows = slice(base + a * p, base + (a + 1) * p)
            za = z_ref[rows, :].astype(F32)
            o_ref[rows, :] = (gate_ref[rows, :] * (y + bias_ref[...] * za)).astype(o_ref.dtype)


def _long_conv(z, z_slab, gate, gate_slab, spectra, order, bias, layer, fwd_lo, inv, seq_len, p,
               out_dtype, seqs_per_step=1):
    _, m, ct = z.shape
    d = spectra.shape[-1]
    nb = seq_len // p
    nd = 2 * nb - 1
    rows = seqs_per_step * seq_len
    return pl.pallas_call(
        functools.partial(_long_conv_kernel, p=p, nb=nb, n_seq=seqs_per_step),
        grid=(d // ct, m // rows),
        in_specs=[
            pl.BlockSpec((None, rows, ct), lambda j, b: (z_slab + j, b, 0)),
            pl.BlockSpec((None, rows, ct), lambda j, b: (gate_slab + j, b, 0)),
            pl.BlockSpec((None, nd, 2 * p, ct), lambda j, b: (order, 0, 0, j)),
            pl.BlockSpec((None, None, 1, ct), lambda j, b: (layer, order, 0, j)),
            pl.BlockSpec((2 * p, p), lambda j, b: (0, 0)),
            pl.BlockSpec((p, 2 * p), lambda j, b: (0, 0)),
        ],
        out_specs=pl.BlockSpec((None, rows, ct), lambda j, b: (j, b, 0)),
        out_shape=jax.ShapeDtypeStruct((d // ct, m, ct), out_dtype),
        scratch_shapes=[pltpu.VMEM((nb, 2 * p, ct), F32), pltpu.VMEM((nb, 2 * p, ct), F32)],
        compiler_params=_params(2),
        name="long_conv",
    )(z, gate, spectra, bias.reshape(bias.shape[0], bias.shape[1], 1, d), fwd_lo, inv)


def _retention_kernel(*refs, n_chunks, n_seq, has_s0, want_state):
    it = iter(refs)
    dec_ref, q_ref, k_ref, v_ref, sg_ref, gn_ref = (next(it) for _ in range(6))
    s0_ref = next(it) if has_s0 else None
    o_ref = next(it)
    sfin_ref = next(it) if want_state else None
    acc_ref, s_ref, dmat_ref, xi_ref, zeta_ref = (next(it) for _ in range(5))

    c_len = RET_CHUNK
    dk = q_ref.shape[1]

    log_g = -jnp.exp(dec_ref[...])
    g_chunk = [jnp.exp(log_g[direction] * c_len) for direction in range(2)]

    @pl.when(pl.program_id(1) == 0)
    def _():
        ii = lax.broadcasted_iota(jnp.int32, (c_len, c_len), 0)
        jj = lax.broadcasted_iota(jnp.int32, (c_len, c_len), 1)
        idx = lax.broadcasted_iota(jnp.int32, (c_len, 1), 0).astype(F32)
        for direction in range(2):
            lg = log_g[direction]
            diff = (ii - jj) if direction == 0 else (jj - ii)
            dmat_ref[direction] = jnp.where(
                diff >= 0, jnp.exp(lg * jnp.maximum(diff, 0).astype(F32)), 0.0)
            if direction == 0:
                xi = jnp.exp(lg * (idx + 1.0))
                zeta = jnp.exp(lg * (c_len - 1.0 - idx))
            else:
                xi = jnp.exp(lg * (c_len - idx))
                zeta = jnp.exp(lg * idx)
            xi_ref[direction] = jnp.broadcast_to(xi, (c_len, dk))
            zeta_ref[direction] = jnp.broadcast_to(zeta, (c_len, dk))

    def init_state(s):
        for direction in range(2):
            if has_s0:
                s_ref[direction] = s0_ref[s, direction].astype(F32)
            else:
                s_ref[direction] = jnp.zeros(s_ref.shape[1:], F32)

    def write_state(s):
        if want_state:
            for direction in range(2):
                sfin_ref[s, direction] = s_ref[direction].astype(sfin_ref.dtype)

    def chunk(c):
        return pl.ds(pl.multiple_of(c * c_len, c_len), c_len)

    def scan_chunk(c, direction):
        rows = chunk(c)
        qc = q_ref[rows, :]
        kc = k_ref[rows, :]
        vc = v_ref[rows, :]
        s = s_ref[direction]
        scores = lax.dot_general(qc.astype(BF16), kc.astype(BF16), (((1,), (1,)), ((), ())),
                                 preferred_element_type=F32) * dmat_ref[direction]
        inner = _dot(scores.astype(BF16), vc)
        cross = _dot((qc * xi_ref[direction]).astype(BF16), s.astype(BF16))
        upd = lax.dot_general((kc * zeta_ref[direction]).astype(BF16), vc,
                              (((0,), (0,)), ((), ())), preferred_element_type=F32)
        s_ref[direction] = g_chunk[direction] * s + upd
        return inner + cross

    def norm_gate_store(rows, o):
        mu = jnp.mean(o, axis=-1, keepdims=True)
        var = jnp.mean(jnp.square(o - mu), axis=-1, keepdims=True)
        o = (o - mu) * lax.rsqrt(var + EPS)
        o = o * gn_ref[...]
        o_ref[rows, :] = (o * sg_ref[rows, :]).astype(o_ref.dtype)

    if n_chunks == 1:
        for s in range(n_seq):
            init_state(s)
            norm_gate_store(chunk(s), scan_chunk(s, 0) + scan_chunk(s, 1))
            write_state(s)
    else:
        assert n_chunks % 2 == 0 and n_seq == 1
        init_state(0)

        def first_visit(t, carry):
            acc_ref[chunk(t), :] = scan_chunk(t, 0)
            acc_ref[chunk(n_chunks - 1 - t), :] = scan_chunk(n_chunks - 1 - t, 1)
            return carry

        def second_visit(t, carry):
            cf, cb = chunk(t), chunk(n_chunks - 1 - t)
            norm_gate_store(cf, acc_ref[cf, :] + scan_chunk(t, 0))
            norm_gate_store(cb, acc_ref[cb, :] + scan_chunk(n_chunks - 1 - t, 1))
            return carry

        lax.fori_loop(0, n_chunks // 2, first_visit, 0)
        lax.fori_loop(n_chunks // 2, n_chunks, second_visit, 0)
        write_state(0)


def _retention(qk, v, sg, decay, gn, layer, seq_len, s0, want_state, state_dtype):
    nh, m, dv = v.shape
    assert nh == RET_HEADS and qk.shape[0] == 2 * nh
    dk = qk.shape[2]
    n_seq = m // seq_len
    has_s0 = s0 is not None
    c_len = RET_CHUNK
    n_chunks = seq_len // c_len
    seqs = 2 if n_chunks == 1 and n_seq % 2 == 0 else 1
    rows = seqs * seq_len

    in_specs = [
        pl.BlockSpec((None, 2, None, 1, 1), lambda h, b: (layer, 0, h, 0, 0)),
        pl.BlockSpec((None, rows, dk), lambda h, b: (h, b, 0)),
        pl.BlockSpec((None, rows, dk), lambda h, b: (nh + h, b, 0)),
        pl.BlockSpec((None, rows, dv), lambda h, b: (h, b, 0)),
        pl.BlockSpec((None, rows, dv), lambda h, b: (h, b, 0)),
        pl.BlockSpec((None, 1, dv), lambda h, b: (layer, 0, h)),
    ]
    args = [decay.reshape(decay.shape[0], 2, nh, 1, 1), qk, qk, v, sg,
            gn.reshape(gn.shape[0], 1, nh * dv)]
    state_spec = pl.BlockSpec((seqs, None, 2, None, dk, dv),
                              lambda h, b: (b, layer, 0, h, 0, 0))
    if has_s0:
        in_specs.append(state_spec)
        args.append(s0)
    out_specs = [pl.BlockSpec((None, rows, dv), lambda h, b: (h, b, 0))]
    out_shape = [jax.ShapeDtypeStruct((nh, m, dv), BF16)]
    if want_state:
        out_specs.append(state_spec)
        out_shape.append(jax.ShapeDtypeStruct((n_seq, decay.shape[0], 2, nh, dk, dv),
                                              state_dtype))
    outs = pl.pallas_call(
        functools.partial(_retention_kernel, n_chunks=n_chunks, n_seq=seqs, has_s0=has_s0,
                          want_state=want_state),
        grid=(nh, n_seq // seqs),
        in_specs=in_specs,
        out_specs=out_specs,
        out_shape=out_shape,
        scratch_shapes=[pltpu.VMEM((rows, dv), F32), pltpu.VMEM((2, dk, dv), F32),
                        pltpu.VMEM((2, c_len, c_len), F32), pltpu.VMEM((2, c_len, dk), F32),
                        pltpu.VMEM((2, c_len, dk), F32)],
        compiler_params=_params(2),
        name="retention",
    )(*args)
    return (outs[0], outs[1]) if want_state else (outs[0], None)


def _rope_tables(seq_len, dk):
    rows = seq_len // GRID_W
    pos_row = jnp.repeat(jnp.arange(rows, dtype=F32), GRID_W)
    pos_col = jnp.tile(jnp.arange(GRID_W, dtype=F32), rows)
    n = dk // 4
    inv = jnp.exp(-math.log(ROPE_BASE) * jnp.arange(n, dtype=F32) / n)
    ang_r = pos_row[:, None] * inv[None, :]
    ang_c = pos_col[:, None] * inv[None, :]
    cos = jnp.concatenate([jnp.cos(ang_r)] * 2 + [jnp.cos(ang_c)] * 2, axis=-1)
    sin = jnp.concatenate([-jnp.sin(ang_r), jnp.sin(ang_r), -jnp.sin(ang_c), jnp.sin(ang_c)],
                          axis=-1)
    return cos, sin


def _trunk(x, mod, seq_len, conv_block, tm, rope_tabs, s0, want_state, p):
    m, d = x.shape
    mod0, mod1 = mod[0], mod[1]
    nh = RET_HEADS
    dv = p['ret_gn'].shape[1] // nh
    dk = (p['ret_w_in'].shape[2] - 2 * nh * dv) // (2 * nh)

    h = _norm_mod(x, p['norm_mix'], 0, mod0, seq_len, 0, 1)
    u = _hy_in(h, p['hy_w_in'], p['hy_b_in'], p['hy_conv_w'], p['hy_conv_b'], 0, seq_len,
               tm=max(tm, seq_len))
    fwd_lo, fwd_hi, inv = _odd_dft_mats(conv_block)
    spectra = _hyena_filter_spectra(
        seq_len, conv_block, p['hy_filt_w1'][0], p['hy_filt_b1'][0], p['hy_filt_freq'][0],
        p['hy_filt_w2'][0], p['hy_filt_b2'][0], p['hy_filt_w3'][0], fwd_lo, fwd_hi)
    slabs = d // u.shape[2]
    seqs = max(1, 1024 // seq_len)
    z = _long_conv(u, 0, u, slabs, spectra, 0, p['hy_bias_d'], 0, fwd_lo, inv, seq_len,
                   conv_block, F32, seqs)
    z = _long_conv(z, 0, u, 2 * slabs, spectra, 1, p['hy_bias_d'], 0, fwd_lo, inv, seq_len,
                   conv_block, BF16, seqs)
    tr, tr_ffn = tm // 2, tm // 4
    tm_up = min(2 * tm, m // 4)
    x, h = _resid_norm_proj(z, p['hy_w_out_bf16'], p['hy_b_out'], 0, x, mod0, 2, seq_len,
                            p['norm_ffn'], 0, mod0, 3, 4, tr)
    a = _swiglu_up(h, p['ffn_w1'], p['ffn_w3'], 0, tm=tm_up)
    x, h = _resid_norm_proj(a, p['ffn_w2_bf16'], None, 0, x, mod0, 5, seq_len,
                            p['norm_mix'], 1, mod1, 0, 1, tr_ffn)

    qk = _ret_qk_proj(h, p['ret_w_in'], 0, nh, dk, seq_len, rope_tabs, tm=tm)
    v = _proj(h, p['ret_w_in'], 0, 2 * nh * dk, nh * dv, dv, BF16, tm=tm)
    sg = _proj(h, p['ret_w_in'], 0, 2 * nh * dk + nh * dv, nh * dv, dv, F32, silu=True, tm=tm)
    og, s_fin = _retention(qk, v, sg, p['ret_decay'], p['ret_gn'], 0, seq_len, s0,
                           want_state, x.dtype)
    x, h = _resid_norm_proj(og, p['ret_w_out_bf16'], None, 0, x, mod1, 2, seq_len,
                            p['norm_ffn'], 1, mod1, 3, 4, tr)
    a = _swiglu_up(h, p['ffn_w1'], p['ffn_w3'], 1, tm=tm_up)
    y = _resid_norm_proj(a, p['ffn_w2_bf16'], None, 1, x, mod1, 5, seq_len,
                         p['norm_final'], 0, None, 0, 0, tr_ffn)
    return y, s_fin


def kernel(x_prompt, x_sample, state_ret, c, c_ctx, w_ada, b_ada, norm_mix, norm_ffn, norm_final,
           ffn_w1, ffn_w3, ffn_w2, hy_w_in, hy_b_in, hy_conv_w, hy_conv_b, hy_filt_w1, hy_filt_b1,
           hy_filt_freq, hy_filt_w2, hy_filt_b2, hy_filt_w3, hy_bias_d, hy_w_out, hy_b_out,
           ret_w_in, ret_decay, ret_gn, ret_w_out):
    p = dict(norm_mix=norm_mix, norm_ffn=norm_ffn, norm_final=norm_final,
             ffn_w1=ffn_w1, ffn_w3=ffn_w3, ffn_w2=ffn_w2, hy_w_in=hy_w_in, hy_b_in=hy_b_in,
             hy_conv_w=hy_conv_w, hy_conv_b=hy_conv_b, hy_filt_w1=hy_filt_w1,
             hy_filt_b1=hy_filt_b1, hy_filt_freq=hy_filt_freq, hy_filt_w2=hy_filt_w2,
             hy_filt_b2=hy_filt_b2, hy_filt_w3=hy_filt_w3, hy_bias_d=hy_bias_d,
             hy_w_out=hy_w_out, hy_b_out=hy_b_out, ret_w_in=ret_w_in, ret_decay=ret_decay,
             ret_gn=ret_gn, ret_w_out=ret_w_out)
    for name in ('hy_w_out', 'ret_w_out', 'ffn_w2'):
        p[name + '_bf16'] = p[name].astype(BF16)
    n_ctx, ctx_len, d = x_prompt.shape
    n_dec, dec_len, _ = x_sample.shape
    depth = w_ada.shape[0]

    cond_rows = 16
    cond = jnp.concatenate(
        [c_ctx[None, :], c, jnp.zeros((cond_rows - 1 - n_dec, d), c.dtype)], axis=0)
    mod = _ada_mod(cond, w_ada, b_ada).reshape(depth, cond_rows, 6, d)
    mod_ctx = mod[:, 0:1]
    mod_dec = mod[:, 1:1 + n_dec]

    y_prompt, ctx_state = _trunk(
        x_prompt.reshape(n_ctx * ctx_len, d), mod_ctx, ctx_len, conv_block=ctx_len, tm=1024,
        rope_tabs=None, s0=None, want_state=True, p=p)
    dk = state_ret.shape[-2]
    y_sample, _ = _trunk(
        x_sample.reshape(n_dec * dec_len, d), mod_dec, dec_len, conv_block=512, tm=1024,
        rope_tabs=_rope_tables(dec_len, dk), s0=state_ret, want_state=False, p=p)

    return (y_prompt.reshape(x_prompt.shape), y_sample.reshape(x_sample.shape), ctx_state)
```

```python
import functools
import math

import numpy as np
import jax
import jax.numpy as jnp
from jax import lax
from jax.experimental import pallas as pl
from jax.experimental.pallas import tpu as pltpu

F32 = jnp.float32
BF16 = jnp.bfloat16

EPS = 1e-6
GRID_W = 64
HY_BANDS = 16
HY_DECAY_PCT_SHORT = 0.3
HY_DECAY_PCT_LONG = 1.5
HY_TARGET = 1e-2
RET_HEADS = 8
ROPE_BASE = 10000.0
RET_CHUNK = 256
MAC_ROWS = 8
ROW_BLOCKS = 2
V7X_MXU_K = 256
K_CHUNK_MAX = 3072

V7X_VMEM_BYTES = 64 * 1024 * 1024
VMEM_LIMIT = V7X_VMEM_BYTES - 8 * 1024 * 1024
LANES = 128
HIGHEST = lax.Precision.HIGHEST


def _params(n_axes):
    return pltpu.CompilerParams(
        dimension_semantics=("arbitrary",) * n_axes, vmem_limit_bytes=VMEM_LIMIT)


def _dot(a, b):
    return jnp.dot(a, b, preferred_element_type=F32)


def _ada_kernel(c_ref, w_ref, b_ref, o_ref):
    s = jax.nn.silu(c_ref[...]).astype(BF16)
    o_ref[...] = _dot(s, w_ref[...].astype(BF16)) + b_ref[...]


def _ada_mod(cond, w_ada, b_ada, tn=1536):
    depth, d, n = w_ada.shape
    rows = cond.shape[0]
    return pl.pallas_call(
        _ada_kernel,
        grid=(depth, n // tn),
        in_specs=[
            pl.BlockSpec((rows, d), lambda l, j: (0, 0)),
            pl.BlockSpec((None, d, tn), lambda l, j: (l, 0, j)),
            pl.BlockSpec((None, 1, tn), lambda l, j: (l, 0, j)),
        ],
        out_specs=pl.BlockSpec((None, rows, tn), lambda l, j: (l, 0, j)),
        out_shape=jax.ShapeDtypeStruct((depth, rows, n), F32),
        compiler_params=_params(2),
        name="ada_mod",
    )(cond, w_ada, b_ada.reshape(depth, 1, n))


def _seq_of_tile(tm, seq_len, n_mod):
    if n_mod == 1:
        return lambda i: 0
    assert seq_len % tm == 0
    return lambda i: (i * tm) // seq_len


def _norm_mod_kernel(x_ref, g_ref, mod_ref, o_ref, *, shift_idx, scale_idx):
    x = x_ref[...]
    y = x * lax.rsqrt(jnp.mean(x * x, axis=-1, keepdims=True) + EPS)
    y = y * g_ref[...]
    scale = mod_ref[scale_idx:scale_idx + 1, :]
    shift = mod_ref[shift_idx:shift_idx + 1, :]
    o_ref[...] = (y * (1.0 + scale) + shift).astype(o_ref.dtype)


def _norm_mod(x, g, layer, mod, seq_len, shift_idx, scale_idx, tm=512):
    m, d = x.shape
    seq = _seq_of_tile(tm, seq_len, mod.shape[0])
    return pl.pallas_call(
        functools.partial(_norm_mod_kernel, shift_idx=shift_idx, scale_idx=scale_idx),
        grid=(m // tm,),
        in_specs=[
            pl.BlockSpec((tm, d), lambda i: (i, 0)),
            pl.BlockSpec((None, 1, d), lambda i: (layer, 0, 0)),
            pl.BlockSpec((None, 6, d), lambda i: (seq(i), 0, 0)),
        ],
        out_specs=pl.BlockSpec((tm, d), lambda i: (i, 0)),
        out_shape=jax.ShapeDtypeStruct((m, d), BF16),
        compiler_params=_params(1),
        name="norm_mod",
    )(x, g.reshape(g.shape[0], 1, d), mod)


def _cast_weights_once(pairs):
    @pl.when(pl.program_id(1) == 0)
    def _():
        for w_ref, wb_ref in pairs:
            wb_ref[...] = w_ref[...].astype(BF16)


def _w_spec(k, tn, layer, col0=0):
    assert col0 % tn == 0
    return pl.BlockSpec((None, k, tn), lambda j, i: (layer, 0, col0 // tn + j))


def _vec_spec(tn, layer, col0=0):
    return pl.BlockSpec((None, 1, tn), lambda j, i: (layer, 0, col0 // tn + j))


def _rotate_pairs(x, half):
    parts = [pltpu.roll(x[:, g * half:(g + 1) * half], half // 2, 1)
             for g in range(x.shape[1] // half)]
    return jnp.concatenate(parts, axis=1)


def _ret_qk_kernel(*refs, dk, k_tile0, k_scale, use_rope):
    if use_rope:
        a_ref, w_ref, cos_ref, sin_ref, o_ref, wb_ref = refs
    else:
        a_ref, w_ref, o_ref, wb_ref = refs
    _cast_weights_once([(w_ref, wb_ref)])
    scale = jnp.where(pl.program_id(0) >= k_tile0, k_scale, 1.0).astype(F32)
    rb = a_ref.shape[0] // ROW_BLOCKS
    for r in range(ROW_BLOCKS):
        rows = slice(r * rb, (r + 1) * rb)
        y = _dot(a_ref[rows, :], wb_ref[...]) * scale
        for hb in range(y.shape[1] // dk):
            x = y[:, hb * dk:(hb + 1) * dk]
            if use_rope:
                x = x * cos_ref[rows, :] + _rotate_pairs(x, dk // 2) * sin_ref[rows, :]
            o_ref[hb, rows, :] = x


def _ret_qk_proj(a, w, layer, nh, dk, seq_len, rope_tabs, tm=1024, tn=1024):
    m, k = a.shape
    n = 2 * nh * dk
    use_rope = rope_tabs is not None
    in_specs = [pl.BlockSpec((tm, k), lambda j, i: (i, 0)), _w_spec(k, tn, layer)]
    args = [a, w]
    if use_rope:
        assert seq_len % tm == 0
        tiles_per_seq = seq_len // tm
        tab_spec = pl.BlockSpec((tm, dk), lambda j, i: (i % tiles_per_seq, 0))
        in_specs += [tab_spec, tab_spec]
        args += list(rope_tabs)
    return pl.pallas_call(
        functools.partial(_ret_qk_kernel, dk=dk, k_tile0=(nh * dk) // tn, k_scale=dk ** -0.5,
                          use_rope=use_rope),
        grid=(n // tn, m // tm),
        in_specs=in_specs,
        out_specs=pl.BlockSpec((tn // dk, tm, dk), lambda j, i: (j, i, 0)),
        out_shape=jax.ShapeDtypeStruct((n // dk, m, dk), F32),
        scratch_shapes=[pltpu.VMEM((k, tn), BF16)],
        compiler_params=_params(2),
        name="ret_qk_proj",
    )(*args)


def _proj_kernel(a_ref, w_ref, o_ref, wb_ref, *, silu):
    _cast_weights_once([(w_ref, wb_ref)])
    oc = o_ref.shape[2]
    rb = a_ref.shape[0] // ROW_BLOCKS
    for r in range(ROW_BLOCKS):
        rows = slice(r * rb, (r + 1) * rb)
        y = _dot(a_ref[rows, :], wb_ref[...])
        if silu:
            y = jax.nn.silu(y)
        for c in range(o_ref.shape[0]):
            o_ref[c, rows, :] = y[:, c * oc:(c + 1) * oc].astype(o_ref.dtype)


def _proj(a, w, layer, col0, n, oc, out_dtype, silu=False, tm=1024, tn=1024):
    m, k = a.shape
    return pl.pallas_call(
        functools.partial(_proj_kernel, silu=silu),
        grid=(n // tn, m // tm),
        in_specs=[pl.BlockSpec((tm, k), lambda j, i: (i, 0)), _w_spec(k, tn, layer, col0)],
        out_specs=pl.BlockSpec((tn // oc, tm, oc), lambda j, i: (j, i, 0)),
        out_shape=jax.ShapeDtypeStruct((n // oc, m, oc), out_dtype),
        scratch_shapes=[pltpu.VMEM((k, tn), BF16)],
        compiler_params=_params(2),
        name="proj",
    )(a, w)


def _hy_in_kernel(a_ref, w_ref, b_ref, cw_ref, cb_ref, o_ref, wb_ref, *, seq_len, row_blocks):
    _cast_weights_once([(w_ref, wb_ref)])
    n_slab, tm, ct = o_ref.shape
    rq = tm // row_blocks
    wb = wb_ref[...]
    bias = b_ref[...]
    us = [_dot(a_ref[q * rq:(q + 1) * rq, :], wb) + bias for q in range(row_blocks)]
    w0, w1, w2, cb = cw_ref[0:1, :], cw_ref[1:2, :], cw_ref[2:3, :], cb_ref[...]

    def tap_sum(prev, cur, nxt):
        acc = cur * w1
        if prev is not None:
            acc = prev * w0 + acc
        if nxt is not None:
            acc = acc + nxt * w2
        return acc + cb

    def store(rows, val):
        for s in range(n_slab):
            o_ref[s, rows, :] = val[:, s * ct:(s + 1) * ct]

    for q, u in enumerate(us):
        r0 = q * rq
        store(slice(r0, r0 + rq),
              tap_sum(pltpu.roll(u, 1, 0), u, pltpu.roll(u, rq - 1, 0)))
        prev = None if r0 % seq_len == 0 else us[q - 1][rq - 1:rq]
        store(slice(r0, r0 + 1), tap_sum(prev, u[0:1], u[1:2]))
        nxt = None if (r0 + rq) % seq_len == 0 else us[q + 1][0:1]
        store(slice(r0 + rq - 1, r0 + rq), tap_sum(u[rq - 2:rq - 1], u[rq - 1:rq], nxt))
        for r in range(seq_len, rq, seq_len):
            store(slice(r0 + r, r0 + r + 1), tap_sum(None, u[r:r + 1], u[r + 1:r + 2]))
            store(slice(r0 + r - 1, r0 + r), tap_sum(u[r - 2:r - 1], u[r - 1:r], None))


def _hy_in(a, w, b, cw, cb, layer, seq_len, tm, tn=512, ct=256, row_blocks=4):
    m, k = a.shape
    n = w.shape[2]
    assert tm % seq_len == 0
    taps = cw.shape[1]
    return pl.pallas_call(
        functools.partial(_hy_in_kernel, seq_len=seq_len, row_blocks=row_blocks),
        grid=(n // tn, m // tm),
        in_specs=[
            pl.BlockSpec((tm, k), lambda j, i: (i, 0)),
            _w_spec(k, tn, layer),
            _vec_spec(tn, layer),
            pl.BlockSpec((None, taps, tn), lambda j, i: (layer, 0, j)),
            _vec_spec(tn, layer),
        ],
        out_specs=pl.BlockSpec((tn // ct, tm, ct), lambda j, i: (j, i, 0)),
        out_shape=jax.ShapeDtypeStruct((n // ct, m, ct), F32),
        scratch_shapes=[pltpu.VMEM((k, tn), BF16)],
        compiler_params=_params(2),
        name="hy_in",
    )(a, w, b.reshape(b.shape[0], 1, n), cw, cb.reshape(cb.shape[0], 1, n))


def _swiglu_kernel(a_ref, w1_ref, w3_ref, o_ref, w1b_ref, w3b_ref):
    _cast_weights_once([(w1_ref, w1b_ref), (w3_ref, w3b_ref)])
    rb = a_ref.shape[0] // ROW_BLOCKS
    for r in range(ROW_BLOCKS):
        rows = slice(r * rb, (r + 1) * rb)
        a = a_ref[rows, :]
        o_ref[rows, :] = (jax.nn.silu(_dot(a, w1b_ref[...]))
                          * _dot(a, w3b_ref[...])).astype(BF16)


def _swiglu_up(a, w1, w3, layer, tm=1024, tn=512):
    m, k = a.shape
    n = w1.shape[2]
    return pl.pallas_call(
        _swiglu_kernel,
        grid=(n // tn, m // tm),
        in_specs=[pl.BlockSpec((tm, k), lambda j, i: (i, 0)),
                  _w_spec(k, tn, layer), _w_spec(k, tn, layer)],
        out_specs=pl.BlockSpec((tm, tn), lambda j, i: (i, j)),
        out_shape=jax.ShapeDtypeStruct((m, n), BF16),
        scratch_shapes=[pltpu.VMEM((k, tn), BF16), pltpu.VMEM((k, tn), BF16)],
        compiler_params=_params(2),
        name="swiglu_up",
    )(a, w1, w3)


def _resid_norm_kernel(*refs, layer, gate_idx, has_bias, final, shift_idx, scale_idx,
                       row_blocks, k_chunk):
    it = iter(refs)
    a_ref, w_hbm_ref = next(it), next(it)
    b_ref = next(it) if has_bias else None
    x_ref, mod_ref = next(it), next(it)
    nmod_ref = None if final else next(it)
    gain_ref = next(it)
    xo_ref = None if final else next(it)
    h_ref = next(it)
    w_ref, w_sem = next(it), next(it)
    tm = x_ref.shape[0]
    rb = tm // row_blocks

    @pl.when(pl.program_id(0) == 0)
    def _():
        copy = pltpu.make_async_copy(w_hbm_ref.at[layer], w_ref, w_sem)
        copy.start()
        copy.wait()

    def finish(rows, y):
        if has_bias:
            y = y + b_ref[...]
        xn = x_ref[rows, :] + mod_ref[gate_idx:gate_idx + 1, :] * y
        if not final:
            xo_ref[rows, :] = xn
        hn = xn * lax.rsqrt(jnp.mean(xn * xn, axis=-1, keepdims=True) + EPS)
        hn = hn * gain_ref[...]
        if not final:
            hn = hn * (1.0 + nmod_ref[scale_idx:scale_idx + 1, :]) \
                + nmod_ref[shift_idx:shift_idx + 1, :]
        h_ref[rows, :] = hn.astype(h_ref.dtype)

    kk = w_ref.shape[0]
    for r in range(row_blocks):
        rows = slice(r * rb, (r + 1) * rb)
        y = None
        for k0 in range(0, kk, k_chunk):
            if len(a_ref.shape) == 3:
                cw = a_ref.shape[2]
                a = jnp.concatenate([a_ref[c, rows, :]
                                     for c in range(k0 // cw, (k0 + k_chunk) // cw)], axis=1)
            else:
                a = a_ref[rows, k0:k0 + k_chunk]
            part = _dot(a, w_ref[k0:k0 + k_chunk, :])
            y = part if y is None else y + part
        finish(rows, y)


def _resid_norm_proj(a, w, bias, layer, x, mod, gate_idx, seq_len, gain, gain_layer,
                     next_mod, shift_idx, scale_idx, tm, row_blocks=2):
    _, kk, d = w.shape
    k_chunk = next(c for c in range(min(kk, K_CHUNK_MAX), 0, -V7X_MXU_K) if kk % c == 0)
    final = next_mod is None
    seq = _seq_of_tile(tm, seq_len, mod.shape[0])
    if a.ndim == 3:
        n_slab, m, cw = a.shape
        assert n_slab * cw == kk
        a_spec = pl.BlockSpec((n_slab, tm, cw), lambda i: (0, i, 0))
    else:
        m = a.shape[0]
        a_spec = pl.BlockSpec((tm, kk), lambda i: (i, 0))
    in_specs = [a_spec, pl.BlockSpec(memory_space=pl.ANY)]
    args = [a, w]
    if bias is not None:
        in_specs.append(pl.BlockSpec((None, 1, d), lambda i: (layer, 0, 0)))
        args.append(bias.reshape(bias.shape[0], 1, d))
    mod_spec = pl.BlockSpec((None, 6, d), lambda i: (seq(i), 0, 0))
    in_specs += [pl.BlockSpec((tm, d), lambda i: (i, 0)), mod_spec]
    args += [x, mod]
    if not final:
        in_specs.append(mod_spec)
        args.append(next_mod)
    if gain.ndim == 1:
        gain = gain[None]
    in_specs.append(pl.BlockSpec((None, 1, d), lambda i: (gain_layer, 0, 0)))
    args.append(gain.reshape(gain.shape[0], 1, d))
    row_spec = pl.BlockSpec((tm, d), lambda i: (i, 0))
    if final:
        out_specs = [row_spec]
        out_shape = [jax.ShapeDtypeStruct((m, d), F32)]
    else:
        out_specs = [row_spec, row_spec]
        out_shape = [jax.ShapeDtypeStruct((m, d), F32), jax.ShapeDtypeStruct((m, d), BF16)]
    outs = pl.pallas_call(
        functools.partial(_resid_norm_kernel, layer=layer, gate_idx=gate_idx,
                          has_bias=bias is not None, final=final, shift_idx=shift_idx,
                          scale_idx=scale_idx, row_blocks=row_blocks, k_chunk=k_chunk),
        grid=(m // tm,),
        in_specs=in_specs,
        out_specs=out_specs,
        out_shape=out_shape,
        scratch_shapes=[pltpu.VMEM((kk, d), BF16), pltpu.SemaphoreType.DMA(())],
        compiler_params=_params(1),
        name="resid_norm_proj",
    )(*args)
    return outs[0] if final else (outs[0], outs[1])


def _odd_dft_mats(p):
    f = np.arange(p, dtype=np.int64)
    m = np.arange(2 * p, dtype=np.int64)
    phase = ((2 * f[:, None] + 1) * m[None, :]) % (4 * p)
    ang = np.pi * phase.astype(np.float64) / (2 * p)
    fwd = np.concatenate([np.cos(ang), -np.sin(ang)], axis=0)
    fwd_lo = fwd[:, :p]
    fwd_hi = fwd[:, p:].copy()
    fwd_hi[:, 0] = 0.0
    t = np.arange(p, dtype=np.int64)
    phase_i = (t[:, None] * (2 * f[None, :] + 1)) % (4 * p)
    ang_i = np.pi * phase_i.astype(np.float64) / (2 * p)
    inv = np.concatenate([np.cos(ang_i), -np.sin(ang_i)], axis=1) / p
    as_bf16 = lambda a: jnp.asarray(a, F32).astype(BF16)
    return as_bf16(fwd_lo), as_bf16(fwd_hi), as_bf16(inv)


def _filter_mlp_kernel(w1_ref, b1_ref, fr_ref, w2_ref, b2_ref, o_ref, *, seq_len):
    rows = 2 * seq_len
    r = lax.broadcasted_iota(jnp.int32, (rows, LANES), 0)
    lane = lax.broadcasted_iota(jnp.int32, (rows, LANES), 1)
    t = jnp.abs(r - seq_len).astype(F32) / seq_len
    band = jnp.where(lane <= HY_BANDS, lane, lane - HY_BANDS).astype(F32)
    ang = 2.0 * math.pi * t * band
    feat = jnp.where(lane == 0, t,
                     jnp.where(lane <= HY_BANDS, jnp.cos(ang),
                               jnp.where(lane <= 2 * HY_BANDS, jnp.sin(ang), 0.0)))
    z = jnp.dot(feat, w1_ref[...], precision=HIGHEST, preferred_element_type=F32)
    z = jnp.sin(fr_ref[0:1, :] * (z + b1_ref[...]))
    z = jnp.dot(z, w2_ref[...], precision=HIGHEST, preferred_element_type=F32)
    o_ref[...] = jnp.sin(fr_ref[1:2, :] * (z + b2_ref[...]))


def _filter_time_kernel(z_ref, wf_ref, wb_ref, dl_ref, o_ref, *, seq_len):
    z = z_ref[...].astype(BF16)
    ff = _dot(z, wf_ref[...].astype(BF16))
    fb = _dot(z, wb_ref[...].astype(BF16))
    n = lax.broadcasted_iota(jnp.int32, ff.shape, 0) - seq_len
    t = jnp.abs(n).astype(F32) / seq_len
    window = jnp.exp(-t * dl_ref[...])
    kk = jnp.where(n >= 0, ff, fb) * window
    o_ref[...] = jnp.where(n == -seq_len, 0.0, kk)


def _filter_spec_kernel(ka_ref, kb_ref, flo_ref, fhi_ref, o_ref):
    o_ref[...] = (_dot(flo_ref[...], ka_ref[...].astype(BF16))
                  - _dot(fhi_ref[...], kb_ref[...].astype(BF16)))


def _hyena_filter_spectra(seq_len, p, w1, b1, freq, w2, b2, w3, fwd_lo, fwd_hi, ct=512):
    hidden = w1.shape[1]
    n_order = 2
    d = w3.shape[1] // (2 * n_order)
    nb = seq_len // p
    nd = 2 * nb - 1
    rows = 2 * seq_len
    w1p = jnp.pad(w1, ((0, LANES - w1.shape[0]), (0, 0)))
    full = lambda shape: pl.BlockSpec(shape, lambda *_: (0,) * len(shape))
    z = pl.pallas_call(
        functools.partial(_filter_mlp_kernel, seq_len=seq_len),
        grid=(1,),
        in_specs=[full((LANES, hidden)), full((1, hidden)), full((2, hidden)),
                  full((hidden, hidden)), full((1, hidden))],
        out_specs=full((rows, hidden)),
        out_shape=jax.ShapeDtypeStruct((rows, hidden), F32),
        compiler_params=_params(1),
        name="filter_mlp",
    )(w1p, b1.reshape(1, hidden), freq, w2, b2.reshape(1, hidden))

    min_decay = abs(math.log(HY_TARGET) / HY_DECAY_PCT_LONG)
    max_decay = abs(math.log(HY_TARGET) / HY_DECAY_PCT_SHORT)
    deltas = jnp.linspace(min_decay, max_decay, d, dtype=F32).reshape(1, d)
    nct = d // ct
    kk = pl.pallas_call(
        functools.partial(_filter_time_kernel, seq_len=seq_len),
        grid=(n_order, nct),
        in_specs=[
            pl.BlockSpec((rows, hidden), lambda o, j: (0, 0)),
            pl.BlockSpec((hidden, ct), lambda o, j: (0, (2 * o) * nct + j)),
            pl.BlockSpec((hidden, ct), lambda o, j: (0, (2 * o + 1) * nct + j)),
            pl.BlockSpec((1, ct), lambda o, j: (0, j)),
        ],
        out_specs=pl.BlockSpec((None, rows, ct), lambda o, j: (o, 0, j)),
        out_shape=jax.ShapeDtypeStruct((n_order, rows, d), F32),
        compiler_params=_params(2),
        name="filter_time",
    )(z, w3, w3, deltas)

    cs = min(d, 2 * ct)
    return pl.pallas_call(
        _filter_spec_kernel,
        grid=(n_order, d // cs, nd),
        in_specs=[
            pl.BlockSpec((None, p, cs), lambda o, j, e: (o, e + 1, j)),
            pl.BlockSpec((None, p, cs), lambda o, j, e: (o, e, j)),
            pl.BlockSpec((2 * p, p), lambda o, j, e: (0, 0)),
            pl.BlockSpec((2 * p, p), lambda o, j, e: (0, 0)),
        ],
        out_specs=pl.BlockSpec((None, None, 2 * p, cs), lambda o, j, e: (o, e, 0, j)),
        out_shape=jax.ShapeDtypeStruct((n_order, nd, 2 * p, d), F32),
        compiler_params=_params(3),
        name="filter_spec",
    )(kk, kk, fwd_lo, fwd_hi)


def _long_conv_kernel(z_ref, gate_ref, k_ref, bias_ref, fwd_ref, inv_ref, o_ref,
                      zs_ref, ys_ref, *, p, nb, n_seq):
    fwd = fwd_ref[...]
    inv = inv_ref[...]
    seq_len = p * nb
    for s in range(n_seq):
        base = s * seq_len
        for b in range(nb):
            zs_ref[b] = _dot(fwd, z_ref[base + b * p:base + (b + 1) * p, :].astype(BF16))
        for r in range(0, p, MAC_ROWS):
            re = slice(r, r + MAC_ROWS)
            im = slice(p + r, p + r + MAC_ROWS)
            kr = [k_ref[e, re, :] for e in range(2 * nb - 1)]
            ki = [k_ref[e, im, :] for e in range(2 * nb - 1)]
            yr = [None] * nb
            yi = [None] * nb
            for b in range(nb):
                zr = zs_ref[b, re, :]
                zi = zs_ref[b, im, :]
                for a in range(nb):
                    e = a - b + nb - 1
                    tr = kr[e] * zr - ki[e] * zi
                    ti = kr[e] * zi + ki[e] * zr
                    yr[a] = tr if yr[a] is None else yr[a] + tr
                    yi[a] = ti if yi[a] is None else yi[a] + ti
            for a in range(nb):
                ys_ref[a, re, :] = yr[a]
                ys_ref[a, im, :] = yi[a]
        for a in range(nb):
            y = _dot(inv, ys_ref[a].astype(BF16))
            rows = slice(base + a * p, base + (a + 1) * p)
            za = z_ref[rows, :].astype(F32)
            o_ref[rows, :] = (gate_ref[rows, :] * (y + bias_ref[...] * za)).astype(o_ref.dtype)


def _long_conv(z, z_slab, gate, gate_slab, spectra, order, bias, layer, fwd_lo, inv, seq_len, p,
               out_dtype, seqs_per_step=1):
    _, m, ct = z.shape
    d = spectra.shape[-1]
    nb = seq_len // p
    nd = 2 * nb - 1
    rows = seqs_per_step * seq_len
    return pl.pallas_call(
        functools.partial(_long_conv_kernel, p=p, nb=nb, n_seq=seqs_per_step),
        grid=(d // ct, m // rows),
        in_specs=[
            pl.BlockSpec((None, rows, ct), lambda j, b: (z_slab + j, b, 0)),
            pl.BlockSpec((None, rows, ct), lambda j, b: (gate_slab + j, b, 0)),
            pl.BlockSpec((None, nd, 2 * p, ct), lambda j, b: (order, 0, 0, j)),
            pl.BlockSpec((None, None, 1, ct), lambda j, b: (layer, order, 0, j)),
            pl.BlockSpec((2 * p, p), lambda j, b: (0, 0)),
            pl.BlockSpec((p, 2 * p), lambda j, b: (0, 0)),
        ],
        out_specs=pl.BlockSpec((None, rows, ct), lambda j, b: (j, b, 0)),
        out_shape=jax.ShapeDtypeStruct((d // ct, m, ct), out_dtype),
        scratch_shapes=[pltpu.VMEM((nb, 2 * p, ct), F32), pltpu.VMEM((nb, 2 * p, ct), F32)],
        compiler_params=_params(2),
        name="long_conv",
    )(z, gate, spectra, bias.reshape(bias.shape[0], bias.shape[1], 1, d), fwd_lo, inv)


def _retention_kernel(*refs, n_chunks, n_seq, has_s0, want_state):
    it = iter(refs)
    dec_ref, q_ref, k_ref, v_ref, sg_ref, gn_ref = (next(it) for _ in range(6))
    s0_ref = next(it) if has_s0 else None
    o_ref = next(it)
    sfin_ref = next(it) if want_state else None
    acc_ref, s_ref, dmat_ref, xi_ref, zeta_ref = (next(it) for _ in range(5))

    c_len = RET_CHUNK
    dk = q_ref.shape[1]

    log_g = -jnp.exp(dec_ref[...])
    g_chunk = [jnp.exp(log_g[direction] * c_len) for direction in range(2)]

    @pl.when(pl.program_id(1) == 0)
    def _():
        ii = lax.broadcasted_iota(jnp.int32, (c_len, c_len), 0)
        jj = lax.broadcasted_iota(jnp.int32, (c_len, c_len), 1)
        idx = lax.broadcasted_iota(jnp.int32, (c_len, 1), 0).astype(F32)
        for direction in range(2):
            lg = log_g[direction]
            diff = (ii - jj) if direction == 0 else (jj - ii)
            dmat_ref[direction] = jnp.where(
                diff >= 0, jnp.exp(lg * jnp.maximum(diff, 0).astype(F32)), 0.0)
            if direction == 0:
                xi = jnp.exp(lg * (idx + 1.0))
                zeta = jnp.exp(lg * (c_len - 1.0 - idx))
            else:
                xi = jnp.exp(lg * (c_len - idx))
                zeta = jnp.exp(lg * idx)
            xi_ref[direction] = jnp.broadcast_to(xi, (c_len, dk))
            zeta_ref[direction] = jnp.broadcast_to(zeta, (c_len, dk))

    def init_state(s):
        for direction in range(2):
            if has_s0:
                s_ref[direction] = s0_ref[s, direction].astype(F32)
            else:
                s_ref[direction] = jnp.zeros(s_ref.shape[1:], F32)

    def write_state(s):
        if want_state:
            for direction in range(2):
                sfin_ref[s, direction] = s_ref[direction].astype(sfin_ref.dtype)

    def chunk(c):
        return pl.ds(pl.multiple_of(c * c_len, c_len), c_len)

    def scan_chunk(c, direction):
        rows = chunk(c)
        qc = q_ref[rows, :]
        kc = k_ref[rows, :]
        vc = v_ref[rows, :]
        s = s_ref[direction]
        scores = lax.dot_general(qc.astype(BF16), kc.astype(BF16), (((1,), (1,)), ((), ())),
                                 preferred_element_type=F32) * dmat_ref[direction]
        inner = _dot(scores.astype(BF16), vc)
        cross = _dot((qc * xi_ref[direction]).astype(BF16), s.astype(BF16))
        upd = lax.dot_general((kc * zeta_ref[direction]).astype(BF16), vc,
                              (((0,), (0,)), ((), ())), preferred_element_type=F32)
        s_ref[direction] = g_chunk[direction] * s + upd
        return inner + cross

    def norm_gate_store(rows, o):
        mu = jnp.mean(o, axis=-1, keepdims=True)
        var = jnp.mean(jnp.square(o - mu), axis=-1, keepdims=True)
        o = (o - mu) * lax.rsqrt(var + EPS)
        o = o * gn_ref[...]
        o_ref[rows, :] = (o * sg_ref[rows, :]).astype(o_ref.dtype)

    if n_chunks == 1:
        for s in range(n_seq):
            init_state(s)
            norm_gate_store(chunk(s), scan_chunk(s, 0) + scan_chunk(s, 1))
            write_state(s)
    else:
        assert n_chunks % 2 == 0 and n_seq == 1
        init_state(0)

        def first_visit(t, carry):
            acc_ref[chunk(t), :] = scan_chunk(t, 0)
            acc_ref[chunk(n_chunks - 1 - t), :] = scan_chunk(n_chunks - 1 - t, 1)
            return carry

        def second_visit(t, carry):
            cf, cb = chunk(t), chunk(n_chunks - 1 - t)
            norm_gate_store(cf, acc_ref[cf, :] + scan_chunk(t, 0))
            norm_gate_store(cb, acc_ref[cb, :] + scan_chunk(n_chunks - 1 - t, 1))
            return carry

        lax.fori_loop(0, n_chunks // 2, first_visit, 0)
        lax.fori_loop(n_chunks // 2, n_chunks, second_visit, 0)
        write_state(0)


def _retention(qk, v, sg, decay, gn, layer, seq_len, s0, want_state, state_dtype):
    nh, m, dv = v.shape
    assert nh == RET_HEADS and qk.shape[0] == 2 * nh
    dk = qk.shape[2]
    n_seq = m // seq_len
    has_s0 = s0 is not None
    c_len = RET_CHUNK
    n_chunks = seq_len // c_len
    seqs = 2 if n_chunks == 1 and n_seq % 2 == 0 else 1
    rows = seqs * seq_len

    in_specs = [
        pl.BlockSpec((None, 2, None, 1, 1), lambda h, b: (layer, 0, h, 0, 0)),
        pl.BlockSpec((None, rows, dk), lambda h, b: (h, b, 0)),
        pl.BlockSpec((None, rows, dk), lambda h, b: (nh + h, b, 0)),
        pl.BlockSpec((None, rows, dv), lambda h, b: (h, b, 0)),
        pl.BlockSpec((None, rows, dv), lambda h, b: (h, b, 0)),
        pl.BlockSpec((None, 1, dv), lambda h, b: (layer, 0, h)),
    ]
    args = [decay.reshape(decay.shape[0], 2, nh, 1, 1), qk, qk, v, sg,
            gn.reshape(gn.shape[0], 1, nh * dv)]
    state_spec = pl.BlockSpec((seqs, None, 2, None, dk, dv),
                              lambda h, b: (b, layer, 0, h, 0, 0))
    if has_s0:
        in_specs.append(state_spec)
        args.append(s0)
    out_specs = [pl.BlockSpec((None, rows, dv), lambda h, b: (h, b, 0))]
    out_shape = [jax.ShapeDtypeStruct((nh, m, dv), BF16)]
    if want_state:
        out_specs.append(state_spec)
        out_shape.append(jax.ShapeDtypeStruct((n_seq, decay.shape[0], 2, nh, dk, dv),
                                              state_dtype))
    outs = pl.pallas_call(
        functools.partial(_retention_kernel, n_chunks=n_chunks, n_seq=seqs, has_s0=has_s0,
                          want_state=want_state),
        grid=(nh, n_seq // seqs),
        in_specs=in_specs,
        out_specs=out_specs,
        out_shape=out_shape,
        scratch_shapes=[pltpu.VMEM((rows, dv), F32), pltpu.VMEM((2, dk, dv), F32),
                        pltpu.VMEM((2, c_len, c_len), F32), pltpu.VMEM((2, c_len, dk), F32),
                        pltpu.VMEM((2, c_len, dk), F32)],
        compiler_params=_params(2),
        name="retention",
    )(*args)
    return (outs[0], outs[1]) if want_state else (outs[0], None)


def _rope_tables(seq_len, dk):
    rows = seq_len // GRID_W
    pos_row = jnp.repeat(jnp.arange(rows, dtype=F32), GRID_W)
    pos_col = jnp.tile(jnp.arange(GRID_W, dtype=F32), rows)
    n = dk // 4
    inv = jnp.exp(-math.log(ROPE_BASE) * jnp.arange(n, dtype=F32) / n)
    ang_r = pos_row[:, None] * inv[None, :]
    ang_c = pos_col[:, None] * inv[None, :]
    cos = jnp.concatenate([jnp.cos(ang_r)] * 2 + [jnp.cos(ang_c)] * 2, axis=-1)
    sin = jnp.concatenate([-jnp.sin(ang_r), jnp.sin(ang_r), -jnp.sin(ang_c), jnp.sin(ang_c)],
                          axis=-1)
    return cos, sin


def _trunk(x, mod, seq_len, conv_block, tm, rope_tabs, s0, want_state, p):
    m, d = x.shape
    mod0, mod1 = mod[0], mod[1]
    nh = RET_HEADS
    dv = p['ret_gn'].shape[1] // nh
    dk = (p['ret_w_in'].shape[2] - 2 * nh * dv) // (2 * nh)

    h = _norm_mod(x, p['norm_mix'], 0, mod0, seq_len, 0, 1)
    u = _hy_in(h, p['hy_w_in'], p['hy_b_in'], p['hy_conv_w'], p['hy_conv_b'], 0, seq_len,
               tm=max(tm, seq_len))
    fwd_lo, fwd_hi, inv = _odd_dft_mats(conv_block)
    spectra = _hyena_filter_spectra(
        seq_len, conv_block, p['hy_filt_w1'][0], p['hy_filt_b1'][0], p['hy_filt_freq'][0],
        p['hy_filt_w2'][0], p['hy_filt_b2'][0], p['hy_filt_w3'][0], fwd_lo, fwd_hi)
    slabs = d // u.shape[2]
    seqs = max(1, 2048 // seq_len)
    z = _long_conv(u, 0, u, slabs, spectra, 0, p['hy_bias_d'], 0, fwd_lo, inv, seq_len,
                   conv_block, F32, seqs)
    z = _long_conv(z, 0, u, 2 * slabs, spectra, 1, p['hy_bias_d'], 0, fwd_lo, inv, seq_len,
                   conv_block, BF16, seqs)
    tr, tr_ffn = tm // 2, tm // 4
    tm_up = min(2 * tm, m // 2)
    x, h = _resid_norm_proj(z, p['hy_w_out_bf16'], p['hy_b_out'], 0, x, mod0, 2, seq_len,
                            p['norm_ffn'], 0, mod0, 3, 4, tr)
    a = _swiglu_up(h, p['ffn_w1'], p['ffn_w3'], 0, tm=tm_up)
    x, h = _resid_norm_proj(a, p['ffn_w2_bf16'], None, 0, x, mod0, 5, seq_len,
                            p['norm_mix'], 1, mod1, 0, 1, tr_ffn)

    qk = _ret_qk_proj(h, p['ret_w_in'], 0, nh, dk, seq_len, rope_tabs, tm=tm)
    v = _proj(h, p['ret_w_in'], 0, 2 * nh * dk, nh * dv, dv, BF16, tm=tm)
    sg = _proj(h, p['ret_w_in'], 0, 2 * nh * dk + nh * dv, nh * dv, dv, F32, silu=True, tm=tm)
    og, s_fin = _retention(qk, v, sg, p['ret_decay'], p['ret_gn'], 0, seq_len, s0,
                           want_state, x.dtype)
    x, h = _resid_norm_proj(og, p['ret_w_out_bf16'], None, 0, x, mod1, 2, seq_len,
                            p['norm_ffn'], 1, mod1, 3, 4, tr)
    a = _swiglu_up(h, p['ffn_w1'], p['ffn_w3'], 1, tm=tm_up)
    y = _resid_norm_proj(a, p['ffn_w2_bf16'], None, 1, x, mod1, 5, seq_len,
                         p['norm_final'], 0, None, 0, 0, tr_ffn)
    return y, s_fin


def kernel(x_prompt, x_sample, state_ret, c, c_ctx, w_ada, b_ada, norm_mix, norm_ffn, norm_final,
           ffn_w1, ffn_w3, ffn_w2, hy_w_in, hy_b_in, hy_conv_w, hy_conv_b, hy_filt_w1, hy_filt_b1,
           hy_filt_freq, hy_filt_w2, hy_filt_b2, hy_filt_w3, hy_bias_d, hy_w_out, hy_b_out,
           ret_w_in, ret_decay, ret_gn, ret_w_out):
    p = dict(norm_mix=norm_mix, norm_ffn=norm_ffn, norm_final=norm_final,
             ffn_w1=ffn_w1, ffn_w3=ffn_w3, ffn_w2=ffn_w2, hy_w_in=hy_w_in, hy_b_in=hy_b_in,
             hy_conv_w=hy_conv_w, hy_conv_b=hy_conv_b, hy_filt_w1=hy_filt_w1,
             hy_filt_b1=hy_filt_b1, hy_filt_freq=hy_filt_freq, hy_filt_w2=hy_filt_w2,
             hy_filt_b2=hy_filt_b2, hy_filt_w3=hy_filt_w3, hy_bias_d=hy_bias_d,
             hy_w_out=hy_w_out, hy_b_out=hy_b_out, ret_w_in=ret_w_in, ret_decay=ret_decay,
             ret_gn=ret_gn, ret_w_out=ret_w_out)
    for name in ('hy_w_out', 'ret_w_out', 'ffn_w2'):
        p[name + '_bf16'] = p[name].astype(BF16)
    n_ctx, ctx_len, d = x_prompt.shape
    n_dec, dec_len, _ = x_sample.shape
    depth = w_ada.shape[0]

    cond_rows = 16
    cond = jnp.concatenate(
        [c_ctx[None, :], c, jnp.zeros((cond_rows - 1 - n_dec, d), c.dtype)], axis=0)
    mod = _ada_mod(cond, w_ada, b_ada).reshape(depth, cond_rows, 6, d)
    mod_ctx = mod[:, 0:1]
    mod_dec = mod[:, 1:1 + n_dec]

    y_prompt, ctx_state = _trunk(
        x_prompt.reshape(n_ctx * ctx_len, d), mod_ctx, ctx_len, conv_block=ctx_len, tm=1024,
        rope_tabs=None, s0=None, want_state=True, p=p)
    dk = state_ret.shape[-2]
    y_sample, _ = _trunk(
        x_sample.reshape(n_dec * dec_len, d), mod_dec, dec_len, conv_block=512, tm=1024,
        rope_tabs=_rope_tables(dec_len, dk), s0=state_ret, want_state=False, p=p)

    return (y_prompt.reshape(x_prompt.shape), y_sample.reshape(x_sample.shape), ctx_state)
```

```python
import functools
import math

import numpy as np
import jax
import jax.numpy as jnp
from jax import lax
from jax.experimental import pallas as pl
from jax.experimental.pallas import tpu as pltpu

F32 = jnp.float32
BF16 = jnp.bfloat16

EPS = 1e-6
GRID_W = 64
HY_BANDS = 16
HY_DECAY_PCT_SHORT = 0.3
HY_DECAY_PCT_LONG = 1.5
HY_TARGET = 1e-2
RET_HEADS = 8
ROPE_BASE = 10000.0
RET_CHUNK = 256
MAC_ROWS = 8
ROW_BLOCKS = 2
V7X_MXU_K = 256
K_CHUNK_MAX = 3072

V7X_VMEM_BYTES = 64 * 1024 * 1024
VMEM_LIMIT = V7X_VMEM_BYTES - 8 * 1024 * 1024
LANES = 128
HIGHEST = lax.Precision.HIGHEST


def _params(n_axes):
    return pltpu.CompilerParams(
        dimension_semantics=("arbitrary",) * n_axes, vmem_limit_bytes=VMEM_LIMIT)


def _dot(a, b):
    return jnp.dot(a, b, preferred_element_type=F32)


def _ada_kernel(c_ref, w_ref, b_ref, o_ref):
    s = jax.nn.silu(c_ref[...]).astype(BF16)
    o_ref[...] = _dot(s, w_ref[...].astype(BF16)) + b_ref[...]


def _ada_mod(cond, w_ada, b_ada, tn=1536):
    depth, d, n = w_ada.shape
    rows = cond.shape[0]
    return pl.pallas_call(
        _ada_kernel,
        grid=(depth, n // tn),
        in_specs=[
            pl.BlockSpec((rows, d), lambda l, j: (0, 0)),
            pl.BlockSpec((None, d, tn), lambda l, j: (l, 0, j)),
            pl.BlockSpec((None, 1, tn), lambda l, j: (l, 0, j)),
        ],
        out_specs=pl.BlockSpec((None, rows, tn), lambda l, j: (l, 0, j)),
        out_shape=jax.ShapeDtypeStruct((depth, rows, n), F32),
        compiler_params=_params(2),
        name="ada_mod",
    )(cond, w_ada, b_ada.reshape(depth, 1, n))


def _seq_of_tile(tm, seq_len, n_mod):
    if n_mod == 1:
        return lambda i: 0
    assert seq_len % tm == 0
    return lambda i: (i * tm) // seq_len


def _norm_mod_kernel(x_ref, g_ref, mod_ref, o_ref, *, shift_idx, scale_idx):
    x = x_ref[...]
    y = x * lax.rsqrt(jnp.mean(x * x, axis=-1, keepdims=True) + EPS)
    y = y * g_ref[...]
    scale = mod_ref[scale_idx:scale_idx + 1, :]
    shift = mod_ref[shift_idx:shift_idx + 1, :]
    o_ref[...] = (y * (1.0 + scale) + shift).astype(o_ref.dtype)


def _norm_mod(x, g, layer, mod, seq_len, shift_idx, scale_idx, tm=512):
    m, d = x.shape
    seq = _seq_of_tile(tm, seq_len, mod.shape[0])
    return pl.pallas_call(
        functools.partial(_norm_mod_kernel, shift_idx=shift_idx, scale_idx=scale_idx),
        grid=(m // tm,),
        in_specs=[
            pl.BlockSpec((tm, d), lambda i: (i, 0)),
            pl.BlockSpec((None, 1, d), lambda i: (layer, 0, 0)),
            pl.BlockSpec((None, 6, d), lambda i: (seq(i), 0, 0)),
        ],
        out_specs=pl.BlockSpec((tm, d), lambda i: (i, 0)),
        out_shape=jax.ShapeDtypeStruct((m, d), BF16),
        compiler_params=_params(1),
        name="norm_mod",
    )(x, g.reshape(g.shape[0], 1, d), mod)


def _cast_weights_once(pairs):
    @pl.when(pl.program_id(1) == 0)
    def _():
        for w_ref, wb_ref in pairs:
            wb_ref[...] = w_ref[...].astype(BF16)


def _w_spec(k, tn, layer, col0=0):
    assert col0 % tn == 0
    return pl.BlockSpec((None, k, tn), lambda j, i: (layer, 0, col0 // tn + j))


def _vec_spec(tn, layer, col0=0):
    return pl.BlockSpec((None, 1, tn), lambda j, i: (layer, 0, col0 // tn + j))


def _rotate_pairs(x, half):
    parts = [pltpu.roll(x[:, g * half:(g + 1) * half], half // 2, 1)
             for g in range(x.shape[1] // half)]
    return jnp.concatenate(parts, axis=1)


def _ret_qk_kernel(*refs, dk, k_tile0, k_scale, use_rope):
    if use_rope:
        a_ref, w_ref, cos_ref, sin_ref, o_ref, wb_ref = refs
    else:
        a_ref, w_ref, o_ref, wb_ref = refs
    _cast_weights_once([(w_ref, wb_ref)])
    scale = jnp.where(pl.program_id(0) >= k_tile0, k_scale, 1.0).astype(F32)
    rb = a_ref.shape[0] // ROW_BLOCKS
    for r in range(ROW_BLOCKS):
        rows = slice(r * rb, (r + 1) * rb)
        y = _dot(a_ref[rows, :], wb_ref[...]) * scale
        for hb in range(y.shape[1] // dk):
            x = y[:, hb * dk:(hb + 1) * dk]
            if use_rope:
                x = x * cos_ref[rows, :] + _rotate_pairs(x, dk // 2) * sin_ref[rows, :]
            o_ref[hb, rows, :] = x


def _ret_qk_proj(a, w, layer, nh, dk, seq_len, rope_tabs, tm=1024, tn=1024):
    m, k = a.shape
    n = 2 * nh * dk
    use_rope = rope_tabs is not None
    in_specs = [pl.BlockSpec((tm, k), lambda j, i: (i, 0)), _w_spec(k, tn, layer)]
    args = [a, w]
    if use_rope:
        assert seq_len % tm == 0
        tiles_per_seq = seq_len // tm
        tab_spec = pl.BlockSpec((tm, dk), lambda j, i: (i % tiles_per_seq, 0))
        in_specs += [tab_spec, tab_spec]
        args += list(rope_tabs)
    return pl.pallas_call(
        functools.partial(_ret_qk_kernel, dk=dk, k_tile0=(nh * dk) // tn, k_scale=dk ** -0.5,
                          use_rope=use_rope),
        grid=(n // tn, m // tm),
        in_specs=in_specs,
        out_specs=pl.BlockSpec((tn // dk, tm, dk), lambda j, i: (j, i, 0)),
        out_shape=jax.ShapeDtypeStruct((n // dk, m, dk), F32),
        scratch_shapes=[pltpu.VMEM((k, tn), BF16)],
        compiler_params=_params(2),
        name="ret_qk_proj",
    )(*args)


def _proj_kernel(a_ref, w_ref, o_ref, wb_ref, *, silu):
    _cast_weights_once([(w_ref, wb_ref)])
    oc = o_ref.shape[2]
    rb = a_ref.shape[0] // ROW_BLOCKS
    for r in range(ROW_BLOCKS):
        rows = slice(r * rb, (r + 1) * rb)
        y = _dot(a_ref[rows, :], wb_ref[...])
        if silu:
            y = jax.nn.silu(y)
        for c in range(o_ref.shape[0]):
            o_ref[c, rows, :] = y[:, c * oc:(c + 1) * oc].astype(o_ref.dtype)


def _proj(a, w, layer, col0, n, oc, out_dtype, silu=False, tm=1024, tn=1024):
    m, k = a.shape
    return pl.pallas_call(
        functools.partial(_proj_kernel, silu=silu),
        grid=(n // tn, m // tm),
        in_specs=[pl.BlockSpec((tm, k), lambda j, i: (i, 0)), _w_spec(k, tn, layer, col0)],
        out_specs=pl.BlockSpec((tn // oc, tm, oc), lambda j, i: (j, i, 0)),
        out_shape=jax.ShapeDtypeStruct((n // oc, m, oc), out_dtype),
        scratch_shapes=[pltpu.VMEM((k, tn), BF16)],
        compiler_params=_params(2),
        name="proj",
    )(a, w)


def _hy_in_kernel(a_ref, w_ref, b_ref, cw_ref, cb_ref, o_ref, wb_ref, *, seq_len, row_blocks):
    _cast_weights_once([(w_ref, wb_ref)])
    n_slab, tm, ct = o_ref.shape
    rq = tm // row_blocks
    wb = wb_ref[...]
    bias = b_ref[...]
    us = [_dot(a_ref[q * rq:(q + 1) * rq, :], wb) + bias for q in range(row_blocks)]
    w0, w1, w2, cb = cw_ref[0:1, :], cw_ref[1:2, :], cw_ref[2:3, :], cb_ref[...]

    def tap_sum(prev, cur, nxt):
        acc = cur * w1
        if prev is not None:
            acc = prev * w0 + acc
        if nxt is not None:
            acc = acc + nxt * w2
        return acc + cb

    def store(rows, val):
        for s in range(n_slab):
            o_ref[s, rows, :] = val[:, s * ct:(s + 1) * ct]

    for q, u in enumerate(us):
        r0 = q * rq
        store(slice(r0, r0 + rq),
              tap_sum(pltpu.roll(u, 1, 0), u, pltpu.roll(u, rq - 1, 0)))
        prev = None if r0 % seq_len == 0 else us[q - 1][rq - 1:rq]
        store(slice(r0, r0 + 1), tap_sum(prev, u[0:1], u[1:2]))
        nxt = None if (r0 + rq) % seq_len == 0 else us[q + 1][0:1]
        store(slice(r0 + rq - 1, r0 + rq), tap_sum(u[rq - 2:rq - 1], u[rq - 1:rq], nxt))
        for r in range(seq_len, rq, seq_len):
            store(slice(r0 + r, r0 + r + 1), tap_sum(None, u[r:r + 1], u[r + 1:r + 2]))
            store(slice(r0 + r - 1, r0 + r), tap_sum(u[r - 2:r - 1], u[r - 1:r], None))


def _hy_in(a, w, b, cw, cb, layer, seq_len, tm, tn=512, ct=256, row_blocks=4):
    m, k = a.shape
    n = w.shape[2]
    assert tm % seq_len == 0
    taps = cw.shape[1]
    return pl.pallas_call(
        functools.partial(_hy_in_kernel, seq_len=seq_len, row_blocks=row_blocks),
        grid=(n // tn, m // tm),
        in_specs=[
            pl.BlockSpec((tm, k), lambda j, i: (i, 0)),
            _w_spec(k, tn, layer),
            _vec_spec(tn, layer),
            pl.BlockSpec((None, taps, tn), lambda j, i: (layer, 0, j)),
            _vec_spec(tn, layer),
        ],
        out_specs=pl.BlockSpec((tn // ct, tm, ct), lambda j, i: (j, i, 0)),
        out_shape=jax.ShapeDtypeStruct((n // ct, m, ct), F32),
        scratch_shapes=[pltpu.VMEM((k, tn), BF16)],
        compiler_params=_params(2),
        name="hy_in",
    )(a, w, b.reshape(b.shape[0], 1, n), cw, cb.reshape(cb.shape[0], 1, n))


def _swiglu_kernel(a_ref, w1_ref, w3_ref, o_ref, w1b_ref, w3b_ref):
    _cast_weights_once([(w1_ref, w1b_ref), (w3_ref, w3b_ref)])
    rb = a_ref.shape[0] // ROW_BLOCKS
    for r in range(ROW_BLOCKS):
        rows = slice(r * rb, (r + 1) * rb)
        a = a_ref[rows, :]
        o_ref[rows, :] = (jax.nn.silu(_dot(a, w1b_ref[...]))
                          * _dot(a, w3b_ref[...])).astype(BF16)


def _swiglu_up(a, w1, w3, layer, tm=1024, tn=512):
    m, k = a.shape
    n = w1.shape[2]
    return pl.pallas_call(
        _swiglu_kernel,
        grid=(n // tn, m // tm),
        in_specs=[pl.BlockSpec((tm, k), lambda j, i: (i, 0)),
                  _w_spec(k, tn, layer), _w_spec(k, tn, layer)],
        out_specs=pl.BlockSpec((tm, tn), lambda j, i: (i, j)),
        out_shape=jax.ShapeDtypeStruct((m, n), BF16),
        scratch_shapes=[pltpu.VMEM((k, tn), BF16), pltpu.VMEM((k, tn), BF16)],
        compiler_params=_params(2),
        name="swiglu_up",
    )(a, w1, w3)


def _resid_norm_kernel(*refs, layer, gate_idx, has_bias, final, shift_idx, scale_idx,
                       row_blocks, k_chunk):
    it = iter(refs)
    a_ref, w_hbm_ref = next(it), next(it)
    b_ref = next(it) if has_bias else None
    x_ref, mod_ref = next(it), next(it)
    nmod_ref = None if final else next(it)
    gain_ref = next(it)
    xo_ref = None if final else next(it)
    h_ref = next(it)
    w_ref, w_sem = next(it), next(it)
    tm = x_ref.shape[0]
    rb = tm // row_blocks

    @pl.when(pl.program_id(0) == 0)
    def _():
        copy = pltpu.make_async_copy(w_hbm_ref.at[layer], w_ref, w_sem)
        copy.start()
        copy.wait()

    def finish(rows, y):
        if has_bias:
            y = y + b_ref[...]
        xn = x_ref[rows, :] + mod_ref[gate_idx:gate_idx + 1, :] * y
        if not final:
            xo_ref[rows, :] = xn
        hn = xn * lax.rsqrt(jnp.mean(xn * xn, axis=-1, keepdims=True) + EPS)
        hn = hn * gain_ref[...]
        if not final:
            hn = hn * (1.0 + nmod_ref[scale_idx:scale_idx + 1, :]) \
                + nmod_ref[shift_idx:shift_idx + 1, :]
        h_ref[rows, :] = hn.astype(h_ref.dtype)

    kk = w_ref.shape[0]
    for r in range(row_blocks):
        rows = slice(r * rb, (r + 1) * rb)
        y = None
        for k0 in range(0, kk, k_chunk):
            if len(a_ref.shape) == 3:
                cw = a_ref.shape[2]
                a = jnp.concatenate([a_ref[c, rows, :]
                                     for c in range(k0 // cw, (k0 + k_chunk) // cw)], axis=1)
            else:
                a = a_ref[rows, k0:k0 + k_chunk]
            part = _dot(a, w_ref[k0:k0 + k_chunk, :])
            y = part if y is None else y + part
        finish(rows, y)


def _resid_norm_proj(a, w, bias, layer, x, mod, gate_idx, seq_len, gain, gain_layer,
                     next_mod, shift_idx, scale_idx, tm, row_blocks=2):
    _, kk, d = w.shape
    k_chunk = next(c for c in range(min(kk, K_CHUNK_MAX), 0, -V7X_MXU_K) if kk % c == 0)
    final = next_mod is None
    seq = _seq_of_tile(tm, seq_len, mod.shape[0])
    if a.ndim == 3:
        n_slab, m, cw = a.shape
        assert n_slab * cw == kk
        a_spec = pl.BlockSpec((n_slab, tm, cw), lambda i: (0, i, 0))
    else:
        m = a.shape[0]
        a_spec = pl.BlockSpec((tm, kk), lambda i: (i, 0))
    in_specs = [a_spec, pl.BlockSpec(memory_space=pl.ANY)]
    args = [a, w]
    if bias is not None:
        in_specs.append(pl.BlockSpec((None, 1, d), lambda i: (layer, 0, 0)))
        args.append(bias.reshape(bias.shape[0], 1, d))
    mod_spec = pl.BlockSpec((None, 6, d), lambda i: (seq(i), 0, 0))
    in_specs += [pl.BlockSpec((tm, d), lambda i: (i, 0)), mod_spec]
    args += [x, mod]
    if not final:
        in_specs.append(mod_spec)
        args.append(next_mod)
    if gain.ndim == 1:
        gain = gain[None]
    in_specs.append(pl.BlockSpec((None, 1, d), lambda i: (gain_layer, 0, 0)))
    args.append(gain.reshape(gain.shape[0], 1, d))
    row_spec = pl.BlockSpec((tm, d), lambda i: (i, 0))
    if final:
        out_specs = [row_spec]
        out_shape = [jax.ShapeDtypeStruct((m, d), F32)]
    else:
        out_specs = [row_spec, row_spec]
        out_shape = [jax.ShapeDtypeStruct((m, d), F32), jax.ShapeDtypeStruct((m, d), BF16)]
    outs = pl.pallas_call(
        functools.partial(_resid_norm_kernel, layer=layer, gate_idx=gate_idx,
                          has_bias=bias is not None, final=final, shift_idx=shift_idx,
                          scale_idx=scale_idx, row_blocks=row_blocks, k_chunk=k_chunk),
        grid=(m // tm,),
        in_specs=in_specs,
        out_specs=out_specs,
        out_shape=out_shape,
        scratch_shapes=[pltpu.VMEM((kk, d), BF16), pltpu.SemaphoreType.DMA(())],
        compiler_params=_params(1),
        name="resid_norm_proj",
    )(*args)
    return outs[0] if final else (outs[0], outs[1])


def _odd_dft_mats(p):
    f = np.arange(p, dtype=np.int64)
    m = np.arange(2 * p, dtype=np.int64)
    phase = ((2 * f[:, None] + 1) * m[None, :]) % (4 * p)
    ang = np.pi * phase.astype(np.float64) / (2 * p)
    fwd = np.concatenate([np.cos(ang), -np.sin(ang)], axis=0)
    fwd_lo = fwd[:, :p]
    fwd_hi = fwd[:, p:].copy()
    fwd_hi[:, 0] = 0.0
    t = np.arange(p, dtype=np.int64)
    phase_i = (t[:, None] * (2 * f[None, :] + 1)) % (4 * p)
    ang_i = np.pi * phase_i.astype(np.float64) / (2 * p)
    inv = np.concatenate([np.cos(ang_i), -np.sin(ang_i)], axis=1) / p
    as_bf16 = lambda a: jnp.asarray(a, F32).astype(BF16)
    return as_bf16(fwd_lo), as_bf16(fwd_hi), as_bf16(inv)


def _filter_mlp_kernel(w1_ref, b1_ref, fr_ref, w2_ref, b2_ref, o_ref, *, seq_len):
    rows = 2 * seq_len
    r = lax.broadcasted_iota(jnp.int32, (rows, LANES), 0)
    lane = lax.broadcasted_iota(jnp.int32, (rows, LANES), 1)
    t = jnp.abs(r - seq_len).astype(F32) / seq_len
    band = jnp.where(lane <= HY_BANDS, lane, lane - HY_BANDS).astype(F32)
    ang = 2.0 * math.pi * t * band
    feat = jnp.where(lane == 0, t,
                     jnp.where(lane <= HY_BANDS, jnp.cos(ang),
                               jnp.where(lane <= 2 * HY_BANDS, jnp.sin(ang), 0.0)))
    z = jnp.dot(feat, w1_ref[...], precision=HIGHEST, preferred_element_type=F32)
    z = jnp.sin(fr_ref[0:1, :] * (z + b1_ref[...]))
    z = jnp.dot(z, w2_ref[...], precision=HIGHEST, preferred_element_type=F32)
    o_ref[...] = jnp.sin(fr_ref[1:2, :] * (z + b2_ref[...]))


def _filter_spec_kernel(za_ref, zb_ref, wf_ref, wb_ref, dl_ref, flo_ref, fhi_ref, o_ref, *,
                        seq_len, p):
    e = pl.program_id(2)
    wf = wf_ref[...].astype(BF16)
    wb = wb_ref[...].astype(BF16)

    def kk_block(z_ref, block):
        z = z_ref[...].astype(BF16)
        ff = _dot(z, wf)
        fb = _dot(z, wb)
        n = lax.broadcasted_iota(jnp.int32, ff.shape, 0) + (block * p - seq_len)
        t = jnp.abs(n).astype(F32) / seq_len
        kk = jnp.where(n >= 0, ff, fb) * jnp.exp(-t * dl_ref[...])
        return jnp.where(n == -seq_len, 0.0, kk).astype(BF16)

    o_ref[...] = (_dot(flo_ref[...], kk_block(za_ref, e + 1))
                  - _dot(fhi_ref[...], kk_block(zb_ref, e)))


def _hyena_filter_spectra(seq_len, p, w1, b1, freq, w2, b2, w3, fwd_lo, fwd_hi, ct=512):
    hidden = w1.shape[1]
    n_order = 2
    d = w3.shape[1] // (2 * n_order)
    nb = seq_len // p
    nd = 2 * nb - 1
    rows = 2 * seq_len
    w1p = jnp.pad(w1, ((0, LANES - w1.shape[0]), (0, 0)))
    full = lambda shape: pl.BlockSpec(shape, lambda *_: (0,) * len(shape))
    z = pl.pallas_call(
        functools.partial(_filter_mlp_kernel, seq_len=seq_len),
        grid=(1,),
        in_specs=[full((LANES, hidden)), full((1, hidden)), full((2, hidden)),
                  full((hidden, hidden)), full((1, hidden))],
        out_specs=full((rows, hidden)),
        out_shape=jax.ShapeDtypeStruct((rows, hidden), F32),
        compiler_params=_params(1),
        name="filter_mlp",
    )(w1p, b1.reshape(1, hidden), freq, w2, b2.reshape(1, hidden))

    min_decay = abs(math.log(HY_TARGET) / HY_DECAY_PCT_LONG)
    max_decay = abs(math.log(HY_TARGET) / HY_DECAY_PCT_SHORT)
    deltas = jnp.linspace(min_decay, max_decay, d, dtype=F32).reshape(1, d)
    cs = min(d, 2 * ct)
    ncs = d // cs
    return pl.pallas_call(
        functools.partial(_filter_spec_kernel, seq_len=seq_len, p=p),
        grid=(n_order, ncs, nd),
        in_specs=[
            pl.BlockSpec((p, hidden), lambda o, j, e: (e + 1, 0)),
            pl.BlockSpec((p, hidden), lambda o, j, e: (e, 0)),
            pl.BlockSpec((hidden, cs), lambda o, j, e: (0, (2 * o) * ncs + j)),
            pl.BlockSpec((hidden, cs), lambda o, j, e: (0, (2 * o + 1) * ncs + j)),
            pl.BlockSpec((1, cs), lambda o, j, e: (0, j)),
            pl.BlockSpec((2 * p, p), lambda o, j, e: (0, 0)),
            pl.BlockSpec((2 * p, p), lambda o, j, e: (0, 0)),
        ],
        out_specs=pl.BlockSpec((None, None, 2 * p, cs), lambda o, j, e: (o, e, 0, j)),
        out_shape=jax.ShapeDtypeStruct((n_order, nd, 2 * p, d), F32),
        compiler_params=_params(3),
        name="filter_spec",
    )(z, z, w3, w3, deltas, fwd_lo, fwd_hi)


def _long_conv_kernel(z_ref, gate_ref, k_ref, bias_ref, fwd_ref, inv_ref, o_ref,
                      zs_ref, ys_ref, *, p, nb, n_seq):
    fwd = fwd_ref[...]
    inv = inv_ref[...]
    seq_len = p * nb
    for s in range(n_seq):
        base = s * seq_len
        for b in range(nb):
            zs_ref[b] = _dot(fwd, z_ref[base + b * p:base + (b + 1) * p, :].astype(BF16))
        for r in range(0, p, MAC_ROWS):
            re = slice(r, r + MAC_ROWS)
            im = slice(p + r, p + r + MAC_ROWS)
            kr = [k_ref[e, re, :] for e in range(2 * nb - 1)]
            ki = [k_ref[e, im, :] for e in range(2 * nb - 1)]
            yr = [None] * nb
            yi = [None] * nb
            for b in range(nb):
                zr = zs_ref[b, re, :]
                zi = zs_ref[b, im, :]
                for a in range(nb):
                    e = a - b + nb - 1
                    tr = kr[e] * zr - ki[e] * zi
                    ti = kr[e] * zi + ki[e] * zr
                    yr[a] = tr if yr[a] is None else yr[a] + tr
                    yi[a] = ti if yi[a] is None else yi[a] + ti
            for a in range(nb):
                ys_ref[a, re, :] = yr[a]
                ys_ref[a, im, :] = yi[a]
        for a in range(nb):
            y = _dot(inv, ys_ref[a].astype(BF16))
            rows = slice(base + a * p, base + (a + 1) * p)
            za = z_ref[rows, :].astype(F32)
            o_ref[rows, :] = (gate_ref[rows, :] * (y + bias_ref[...] * za)).astype(o_ref.dtype)


def _long_conv(z, z_slab, gate, gate_slab, spectra, order, bias, layer, fwd_lo, inv, seq_len, p,
               out_dtype, seqs_per_step=1):
    _, m, ct = z.shape
    d = spectra.shape[-1]
    nb = seq_len // p
    nd = 2 * nb - 1
    rows = seqs_per_step * seq_len
    return pl.pallas_call(
        functools.partial(_long_conv_kernel, p=p, nb=nb, n_seq=seqs_per_step),
        grid=(d // ct, m // rows),
        in_specs=[
            pl.BlockSpec((None, rows, ct), lambda j, b: (z_slab + j, b, 0)),
            pl.BlockSpec((None, rows, ct), lambda j, b: (gate_slab + j, b, 0)),
            pl.BlockSpec((None, nd, 2 * p, ct), lambda j, b: (order, 0, 0, j)),
            pl.BlockSpec((None, None, 1, ct), lambda j, b: (layer, order, 0, j)),
            pl.BlockSpec((2 * p, p), lambda j, b: (0, 0)),
            pl.BlockSpec((p, 2 * p), lambda j, b: (0, 0)),
        ],
        out_specs=pl.BlockSpec((None, rows, ct), lambda j, b: (j, b, 0)),
        out_shape=jax.ShapeDtypeStruct((d // ct, m, ct), out_dtype),
        scratch_shapes=[pltpu.VMEM((nb, 2 * p, ct), F32), pltpu.VMEM((nb, 2 * p, ct), F32)],
        compiler_params=_params(2),
        name="long_conv",
    )(z, gate, spectra, bias.reshape(bias.shape[0], bias.shape[1], 1, d), fwd_lo, inv)


def _retention_kernel(*refs, n_chunks, n_seq, has_s0, want_state):
    it = iter(refs)
    dec_ref, q_ref, k_ref, v_ref, sg_ref, gn_ref = (next(it) for _ in range(6))
    s0_ref = next(it) if has_s0 else None
    o_ref = next(it)
    sfin_ref = next(it) if want_state else None
    acc_ref, s_ref, dmat_ref, xi_ref, zeta_ref = (next(it) for _ in range(5))

    c_len = RET_CHUNK
    dk = q_ref.shape[1]

    log_g = -jnp.exp(dec_ref[...])
    g_chunk = [jnp.exp(log_g[direction] * c_len) for direction in range(2)]

    @pl.when(pl.program_id(1) == 0)
    def _():
        ii = lax.broadcasted_iota(jnp.int32, (c_len, c_len), 0)
        jj = lax.broadcasted_iota(jnp.int32, (c_len, c_len), 1)
        idx = lax.broadcasted_iota(jnp.int32, (c_len, 1), 0).astype(F32)
        for direction in range(2):
            lg = log_g[direction]
            diff = (ii - jj) if direction == 0 else (jj - ii)
            dmat_ref[direction] = jnp.where(
                diff >= 0, jnp.exp(lg * jnp.maximum(diff, 0).astype(F32)), 0.0)
            if direction == 0:
                xi = jnp.exp(lg * (idx + 1.0))
                zeta = jnp.exp(lg * (c_len - 1.0 - idx))
            else:
                xi = jnp.exp(lg * (c_len - idx))
                zeta = jnp.exp(lg * idx)
            xi_ref[direction] = jnp.broadcast_to(xi, (c_len, dk))
            zeta_ref[direction] = jnp.broadcast_to(zeta, (c_len, dk))

    def init_state(s):
        for direction in range(2):
            if has_s0:
                s_ref[direction] = s0_ref[s, direction].astype(F32)
            else:
                s_ref[direction] = jnp.zeros(s_ref.shape[1:], F32)

    def write_state(s):
        if want_state:
            for direction in range(2):
                sfin_ref[s, direction] = s_ref[direction].astype(sfin_ref.dtype)

    def chunk(c):
        return pl.ds(pl.multiple_of(c * c_len, c_len), c_len)

    def scan_chunk(c, direction):
        rows = chunk(c)
        qc = q_ref[rows, :]
        kc = k_ref[rows, :]
        vc = v_ref[rows, :]
        s = s_ref[direction]
        scores = lax.dot_general(qc.astype(BF16), kc.astype(BF16), (((1,), (1,)), ((), ())),
                                 preferred_element_type=F32) * dmat_ref[direction]
        inner = _dot(scores.astype(BF16), vc)
        cross = _dot((qc * xi_ref[direction]).astype(BF16), s.astype(BF16))
        upd = lax.dot_general((kc * zeta_ref[direction]).astype(BF16), vc,
                              (((0,), (0,)), ((), ())), preferred_element_type=F32)
        s_ref[direction] = g_chunk[direction] * s + upd
        return inner + cross

    def norm_gate_store(rows, o):
        mu = jnp.mean(o, axis=-1, keepdims=True)
        var = jnp.mean(jnp.square(o - mu), axis=-1, keepdims=True)
        o = (o - mu) * lax.rsqrt(var + EPS)
        o = o * gn_ref[...]
        o_ref[rows, :] = (o * sg_ref[rows, :]).astype(o_ref.dtype)

    if n_chunks == 1:
        for s in range(n_seq):
            init_state(s)
            norm_gate_store(chunk(s), scan_chunk(s, 0) + scan_chunk(s, 1))
            write_state(s)
    else:
        assert n_chunks % 2 == 0 and n_seq == 1
        init_state(0)

        def first_visit(t, carry):
            acc_ref[chunk(t), :] = scan_chunk(t, 0)
            acc_ref[chunk(n_chunks - 1 - t), :] = scan_chunk(n_chunks - 1 - t, 1)
            return carry

        def second_visit(t, carry):
            cf, cb = chunk(t), chunk(n_chunks - 1 - t)
            norm_gate_store(cf, acc_ref[cf, :] + scan_chunk(t, 0))
            norm_gate_store(cb, acc_ref[cb, :] + scan_chunk(n_chunks - 1 - t, 1))
            return carry

        lax.fori_loop(0, n_chunks // 2, first_visit, 0)
        lax.fori_loop(n_chunks // 2, n_chunks, second_visit, 0)
        write_state(0)


def _retention(qk, v, sg, decay, gn, layer, seq_len, s0, want_state, state_dtype):
    nh, m, dv = v.shape
    assert nh == RET_HEADS and qk.shape[0] == 2 * nh
    dk = qk.shape[2]
    n_seq = m // seq_len
    has_s0 = s0 is not None
    c_len = RET_CHUNK
    n_chunks = seq_len // c_len
    seqs = 2 if n_chunks == 1 and n_seq % 2 == 0 else 1
    rows = seqs * seq_len

    in_specs = [
        pl.BlockSpec((None, 2, None, 1, 1), lambda h, b: (layer, 0, h, 0, 0)),
        pl.BlockSpec((None, rows, dk), lambda h, b: (h, b, 0)),
        pl.BlockSpec((None, rows, dk), lambda h, b: (nh + h, b, 0)),
        pl.BlockSpec((None, rows, dv), lambda h, b: (h, b, 0)),
        pl.BlockSpec((None, rows, dv), lambda h, b: (h, b, 0)),
        pl.BlockSpec((None, 1, dv), lambda h, b: (layer, 0, h)),
    ]
    args = [decay.reshape(decay.shape[0], 2, nh, 1, 1), qk, qk, v, sg,
            gn.reshape(gn.shape[0], 1, nh * dv)]
    state_spec = pl.BlockSpec((seqs, None, 2, None, dk, dv),
                              lambda h, b: (b, layer, 0, h, 0, 0))
    if has_s0:
        in_specs.append(state_spec)
        args.append(s0)
    out_specs = [pl.BlockSpec((None, rows, dv), lambda h, b: (h, b, 0))]
    out_shape = [jax.ShapeDtypeStruct((nh, m, dv), BF16)]
    if want_state:
        out_specs.append(state_spec)
        out_shape.append(jax.ShapeDtypeStruct((n_seq, decay.shape[0], 2, nh, dk, dv),
                                              state_dtype))
    outs = pl.pallas_call(
        functools.partial(_retention_kernel, n_chunks=n_chunks, n_seq=seqs, has_s0=has_s0,
                          want_state=want_state),
        grid=(nh, n_seq // seqs),
        in_specs=in_specs,
        out_specs=out_specs,
        out_shape=out_shape,
        scratch_shapes=[pltpu.VMEM((rows, dv), F32), pltpu.VMEM((2, dk, dv), F32),
                        pltpu.VMEM((2, c_len, c_len), F32), pltpu.VMEM((2, c_len, dk), F32),
                        pltpu.VMEM((2, c_len, dk), F32)],
        compiler_params=_params(2),
        name="retention",
    )(*args)
    return (outs[0], outs[1]) if want_state else (outs[0], None)


def _rope_tables(seq_len, dk):
    rows = seq_len // GRID_W
    pos_row = jnp.repeat(jnp.arange(rows, dtype=F32), GRID_W)
    pos_col = jnp.tile(jnp.arange(GRID_W, dtype=F32), rows)
    n = dk // 4
    inv = jnp.exp(-math.log(ROPE_BASE) * jnp.arange(n, dtype=F32) / n)
    ang_r = pos_row[:, None] * inv[None, :]
    ang_c = pos_col[:, None] * inv[None, :]
    cos = jnp.concatenate([jnp.cos(ang_r)] * 2 + [jnp.cos(ang_c)] * 2, axis=-1)
    sin = jnp.concatenate([-jnp.sin(ang_r), jnp.sin(ang_r), -jnp.sin(ang_c), jnp.sin(ang_c)],
                          axis=-1)
    return cos, sin


def _trunk(x, mod, seq_len, conv_block, tm, rope_tabs, s0, want_state, p):
    m, d = x.shape
    mod0, mod1 = mod[0], mod[1]
    nh = RET_HEADS
    dv = p['ret_gn'].shape[1] // nh
    dk = (p['ret_w_in'].shape[2] - 2 * nh * dv) // (2 * nh)

    h = _norm_mod(x, p['norm_mix'], 0, mod0, seq_len, 0, 1)
    u = _hy_in(h, p['hy_w_in'], p['hy_b_in'], p['hy_conv_w'], p['hy_conv_b'], 0, seq_len,
               tm=max(tm, seq_len))
    fwd_lo, fwd_hi, inv = _odd_dft_mats(conv_block)
    spectra = _hyena_filter_spectra(
        seq_len, conv_block, p['hy_filt_w1'][0], p['hy_filt_b1'][0], p['hy_filt_freq'][0],
        p['hy_filt_w2'][0], p['hy_filt_b2'][0], p['hy_filt_w3'][0], fwd_lo, fwd_hi)
    slabs = d // u.shape[2]
    seqs = max(1, 2048 // seq_len)
    z = _long_conv(u, 0, u, slabs, spectra, 0, p['hy_bias_d'], 0, fwd_lo, inv, seq_len,
                   conv_block, F32, seqs)
    z = _long_conv(z, 0, u, 2 * slabs, spectra, 1, p['hy_bias_d'], 0, fwd_lo, inv, seq_len,
                   conv_block, BF16, seqs)
    tr, tr_ffn = tm // 2, tm // 4
    tm_up = min(2 * tm, m // 2)
    x, h = _resid_norm_proj(z, p['hy_w_out_bf16'], p['hy_b_out'], 0, x, mod0, 2, seq_len,
                            p['norm_ffn'], 0, mod0, 3, 4, tr)
    a = _swiglu_up(h, p['ffn_w1'], p['ffn_w3'], 0, tm=tm_up)
    x, h = _resid_norm_proj(a, p['ffn_w2_bf16'], None, 0, x, mod0, 5, seq_len,
                            p['norm_mix'], 1, mod1, 0, 1, tr_ffn)

    qk = _ret_qk_proj(h, p['ret_w_in'], 0, nh, dk, seq_len, rope_tabs, tm=tm)
    v = _proj(h, p['ret_w_in'], 0, 2 * nh * dk, nh * dv, dv, BF16, tm=tm)
    sg = _proj(h, p['ret_w_in'], 0, 2 * nh * dk + nh * dv, nh * dv, dv, F32, silu=True, tm=tm)
    og, s_fin = _retention(qk, v, sg, p['ret_decay'], p['ret_gn'], 0, seq_len, s0,
                           want_state, x.dtype)
    x, h = _resid_norm_proj(og, p['ret_w_out_bf16'], None, 0, x, mod1, 2, seq_len,
                            p['norm_ffn'], 1, mod1, 3, 4, tr)
    a = _swiglu_up(h, p['ffn_w1'], p['ffn_w3'], 1, tm=tm_up)
    y = _resid_norm_proj(a, p['ffn_w2_bf16'], None, 1, x, mod1, 5, seq_len,
                         p['norm_final'], 0, None, 0, 0, tr_ffn)
    return y, s_fin


def kernel(x_prompt, x_sample, state_ret, c, c_ctx, w_ada, b_ada, norm_mix, norm_ffn, norm_final,
           ffn_w1, ffn_w3, ffn_w2, hy_w_in, hy_b_in, hy_conv_w, hy_conv_b, hy_filt_w1, hy_filt_b1,
           hy_filt_freq, hy_filt_w2, hy_filt_b2, hy_filt_w3, hy_bias_d, hy_w_out, hy_b_out,
           ret_w_in, ret_decay, ret_gn, ret_w_out):
    p = dict(norm_mix=norm_mix, norm_ffn=norm_ffn, norm_final=norm_final,
             ffn_w1=ffn_w1, ffn_w3=ffn_w3, ffn_w2=ffn_w2, hy_w_in=hy_w_in, hy_b_in=hy_b_in,
             hy_conv_w=hy_conv_w, hy_conv_b=hy_conv_b, hy_filt_w1=hy_filt_w1,
             hy_filt_b1=hy_filt_b1, hy_filt_freq=hy_filt_freq, hy_filt_w2=hy_filt_w2,
             hy_filt_b2=hy_filt_b2, hy_filt_w3=hy_filt_w3, hy_bias_d=hy_bias_d,
             hy_w_out=hy_w_out, hy_b_out=hy_b_out, ret_w_in=ret_w_in, ret_decay=ret_decay,
             ret_gn=ret_gn, ret_w_out=ret_w_out)
    for name in ('hy_w_out', 'ret_w_out', 'ffn_w2'):
        p[name + '_bf16'] = p[name].astype(BF16)
    n_ctx, ctx_len, d = x_prompt.shape
    n_dec, dec_len, _ = x_sample.shape
    depth = w_ada.shape[0]

    cond_rows = 16
    cond = jnp.concatenate(
        [c_ctx[None, :], c, jnp.zeros((cond_rows - 1 - n_dec, d), c.dtype)], axis=0)
    mod = _ada_mod(cond, w_ada, b_ada).reshape(depth, cond_rows, 6, d)
    mod_ctx = mod[:, 0:1]
    mod_dec = mod[:, 1:1 + n_dec]

    y_prompt, ctx_state = _trunk(
        x_prompt.reshape(n_ctx * ctx_len, d), mod_ctx, ctx_len, conv_block=ctx_len, tm=1024,
        rope_tabs=None, s0=None, want_state=True, p=p)
    dk = state_ret.shape[-2]
    y_sample, _ = _trunk(
        x_sample.reshape(n_dec * dec_len, d), mod_dec, dec_len, conv_block=512, tm=1024,
        rope_tabs=_rope_tables(dec_len, dk), s0=state_ret, want_state=False, p=p)

    return (y_prompt.reshape(x_prompt.shape), y_sample.reshape(x_sample.shape), ctx_state)
```
